```python
import numpy as np
import jax
import jax.numpy as jnp
from jax import lax

D_MODEL = 4096
BATCH = 2
SEQ = 4096
DEPTH = 2

HEAD_DIM = 128
D_MIX = D_MODEL
N_MIX_HEADS = D_MIX // HEAD_DIM
H_MIX = N_MIX_HEADS // 4
NSA_KV = max(1, H_MIX // 4)
NSA_CMP_LEN = 32
NSA_CMP_STRIDE = 16
NSA_CMP_HIDDEN = 256
NSA_SLC_LEN = 64
NSA_TOPK = 16
NSA_WINDOW = 512
DIL_CONFIGS = ((128, 1), (512, 4), (2048, 16))
D_FF = 4 * D_MODEL
PLE_DIM = 256
ROPE_THETA = 10000.0
Q_BLOCK = 128
RMS_EPS = 1e-6
NEG = -1e30
FORCED_SCORE = 1e9

SPLITS = (
    H_MIX * HEAD_DIM,
    NSA_KV * HEAD_DIM, NSA_KV * HEAD_DIM,
    NSA_KV * HEAD_DIM, NSA_KV * HEAD_DIM,
    NSA_KV * HEAD_DIM, NSA_KV * HEAD_DIM,
    H_MIX * 3,
    H_MIX * HEAD_DIM, H_MIX * HEAD_DIM, H_MIX * HEAD_DIM, H_MIX,
    H_MIX * HEAD_DIM, H_MIX * HEAD_DIM, H_MIX * HEAD_DIM,
    H_MIX * HEAD_DIM, H_MIX * HEAD_DIM, H_MIX * HEAD_DIM,
)
N_IN = sum(SPLITS)

kernel_name = "hybrid_nsa_fox_dilated_stickbreak_trunk"


def _rmsnorm(x, g):
    xf = x.astype(jnp.float32)
    y = xf * lax.rsqrt(jnp.mean(xf * xf, axis=-1, keepdims=True) + RMS_EPS)
    return (y * g.astype(jnp.float32)).astype(x.dtype)


def _rope(t, positions):
    half = HEAD_DIM // 2
    inv_freq = ROPE_THETA ** (-jnp.arange(half, dtype=jnp.float32) / half)
    ang = positions.astype(jnp.float32)[..., None] * inv_freq
    cos = jnp.cos(ang)[:, :, None, :]
    sin = jnp.sin(ang)[:, :, None, :]
    tf = t.astype(jnp.float32)
    t1, t2 = tf[..., :half], tf[..., half:]
    return jnp.concatenate([t1 * cos - t2 * sin, t2 * cos + t1 * sin], axis=-1).astype(t.dtype)


def _heads(t, n):
    return t.reshape(t.shape[0], t.shape[1], n, HEAD_DIM)


def _qblock(t, i, axis=1):
    return lax.dynamic_slice_in_dim(t, i * Q_BLOCK, Q_BLOCK, axis=axis)


def _unblock(o):
    o = jnp.moveaxis(o, 0, 1)
    return o.reshape((o.shape[0], o.shape[1] * o.shape[2]) + o.shape[3:])


def _nsa_attention(q, k_cmp, v_cmp, k_slc, v_slc, k_win, v_win, gate_logits,
                   cmp_pe_k, cmp_w1_k, cmp_w2_k, cmp_pe_v, cmp_w1_v, cmp_w2_v):
    B, S, H, D = q.shape
    G = k_cmp.shape[2]
    hpg = H // G
    scale = D ** -0.5
    qg = q.reshape(B, S, G, hpg, D)
    tpos = jnp.arange(S)

    n_cmp = (S - NSA_CMP_LEN) // NSA_CMP_STRIDE + 1
    blk = np.arange(n_cmp)[:, None] * NSA_CMP_STRIDE + np.arange(NSA_CMP_LEN)[None, :]

    def compress(t, pe, w1, w2):
        tb = jnp.take(t, blk, axis=1) + pe[None, None, :, None, :]
        tb = tb.transpose(0, 1, 3, 2, 4).reshape(B, n_cmp, G, NSA_CMP_LEN * D)
        return jax.nn.gelu(tb @ w1) @ w2

    kc = compress(k_cmp, cmp_pe_k, cmp_w1_k, cmp_w2_k)
    vc = compress(v_cmp, cmp_pe_v, cmp_w1_v, cmp_w2_v)
    visible = jnp.asarray(blk[:, -1])[None, :] <= tpos[:, None]
    s = jnp.einsum('bsghd,bcgd->bghsc', qg, kc).astype(jnp.float32) * scale
    s = jnp.where(visible, s, NEG)
    p_cmp = jnp.where(visible, jax.nn.softmax(s, axis=-1), 0.0)
    o_cmp = jnp.einsum('bghsc,bcgd->bsghd', p_cmp.astype(vc.dtype), vc).reshape(B, S, H, D)

    n_slc = S // NSA_SLC_LEN
    ratio = NSA_SLC_LEN // NSA_CMP_STRIDE
    span = NSA_CMP_LEN // NSA_CMP_STRIDE
    jj, mm, nn = np.meshgrid(np.arange(n_slc), np.arange(ratio), np.arange(span), indexing='ij')
    cc = ratio * jj + mm + nn
    keep = cc < n_cmp
    cmp_to_slc = np.zeros((n_cmp, n_slc), np.float32)
    np.add.at(cmp_to_slc, (cc[keep], jj[keep]), 1.0)
    imp = jnp.einsum('bghsc,cj->bgsj', p_cmp, jnp.asarray(cmp_to_slc))
    jblk = jnp.arange(n_slc)[None, :]
    cur = (tpos // NSA_SLC_LEN)[:, None]
    forced = (jblk == 0) | (jblk == cur) | (jblk == cur - 1)
    causal_blk = jblk * NSA_SLC_LEN <= tpos[:, None]
    score = jnp.where(causal_blk, jnp.where(forced, FORCED_SCORE, imp), -1.0)
    k_top = min(NSA_TOPK, n_slc)
    _, sel = lax.top_k(score, k_top)

    k_slc_t = k_slc.transpose(0, 2, 1, 3)
    v_slc_t = v_slc.transpose(0, 2, 1, 3)
    bi = jnp.arange(B)[:, None, None]
    gi = jnp.arange(G)[None, :, None]
    n_sel_tok = k_top * NSA_SLC_LEN
    k_win_p = jnp.pad(k_win, ((0, 0), (NSA_WINDOW, 0), (0, 0), (0, 0)))
    v_win_p = jnp.pad(v_win, ((0, 0), (NSA_WINDOW, 0), (0, 0), (0, 0)))

    def block(i):
        t0 = i * Q_BLOCK
        tq = t0 + jnp.arange(Q_BLOCK)
        qb = _qblock(qg, i)
        selb = _qblock(sel, i, axis=2)
        pos = (selb[..., None] * NSA_SLC_LEN + jnp.arange(NSA_SLC_LEN)).reshape(B, G, Q_BLOCK * n_sel_tok)
        kg = k_slc_t[bi, gi, pos].reshape(B, G, Q_BLOCK, n_sel_tok, D)
        vg = v_slc_t[bi, gi, pos].reshape(B, G, Q_BLOCK, n_sel_tok, D)
        m_sel = pos.reshape(B, G, Q_BLOCK, n_sel_tok) <= tq[None, None, :, None]
        ss = jnp.einsum('bqghd,bgqnd->bghqn', qb, kg).astype(jnp.float32) * scale
        ps = jax.nn.softmax(jnp.where(m_sel[:, :, None], ss, NEG), axis=-1)
        o_s = jnp.einsum('bghqn,bgqnd->bqghd', ps.astype(vg.dtype), vg)
        kw = lax.dynamic_slice_in_dim(k_win_p, t0, NSA_WINDOW + Q_BLOCK, axis=1)
        vw = lax.dynamic_slice_in_dim(v_win_p, t0, NSA_WINDOW + Q_BLOCK, axis=1)
        kwpos = t0 - NSA_WINDOW + jnp.arange(NSA_WINDOW + Q_BLOCK)
        dist = tq[:, None] - kwpos[None, :]
        m_win = (dist >= 0) & (dist < NSA_WINDOW) & (kwpos[None, :] >= 0)
        sw = jnp.einsum('bqghd,bkgd->bghqk', qb, kw).astype(jnp.float32) * scale
        pw = jax.nn.softmax(jnp.where(m_win, sw, NEG), axis=-1)
        o_w = jnp.einsum('bghqk,bkgd->bqghd', pw.astype(vw.dtype), vw)
        return o_s.reshape(B, Q_BLOCK, H, D), o_w.reshape(B, Q_BLOCK, H, D)

    o_slc, o_win = lax.map(block, jnp.arange(S // Q_BLOCK))
    o_slc, o_win = _unblock(o_slc), _unblock(o_win)
    g = jax.nn.sigmoid(gate_logits.astype(jnp.float32))
    out = g[..., 0:1] * o_cmp + g[..., 1:2] * o_slc + g[..., 2:3] * o_win
    return out.astype(q.dtype)


def _fox_attention(q, k, v, log_f):
    B, S, H, D = q.shape
    scale = D ** -0.5
    c = jnp.cumsum(log_f, axis=1).transpose(0, 2, 1)
    kpos = jnp.arange(S)

    def block(i):
        tq = i * Q_BLOCK + jnp.arange(Q_BLOCK)
        qb = _qblock(q, i)
        cb = _qblock(c, i, axis=2)
        s = jnp.einsum('bqhd,bkhd->bhqk', qb, k).astype(jnp.float32) * scale
        s = s + cb[..., None] - c[:, :, None, :]
        s = jnp.where(tq[:, None] >= kpos[None, :], s, NEG)
        p = jax.nn.softmax(s, axis=-1)
        return jnp.einsum('bhqk,bkhd->bqhd', p.astype(v.dtype), v)

    return _unblock(lax.map(block, jnp.arange(S // Q_BLOCK)))


def _dilated_attention(q, k, v):
    B, S, H, D = q.shape
    scale = D ** -0.5

    def block(i):
        tq = i * Q_BLOCK + jnp.arange(Q_BLOCK)
        qb = _qblock(q, i)
        lses, outs = [], []
        for window, dil in DIL_CONFIGS:
            offs = dil * jnp.arange(window // dil + 1)
            idx = tq[:, None] - offs[None, :]
            valid = idx >= 0
            idx = jnp.maximum(idx, 0)
            kg = jnp.take(k, idx, axis=1)
            vg = jnp.take(v, idx, axis=1)
            s = jnp.einsum('bqhd,bqwhd->bhqw', qb, kg).astype(jnp.float32) * scale
            s = jnp.where(valid[None, None], s, NEG)
            lse = jax.nn.logsumexp(s, axis=-1)
            p = jnp.exp(s - lse[..., None])
            outs.append(jnp.einsum('bhqw,bqwhd->bqhd', p.astype(vg.dtype), vg))
            lses.append(lse)
        w = jax.nn.softmax(jnp.stack(lses, axis=0), axis=0)
        w = w.transpose(0, 1, 3, 2)[..., None]
        o = jnp.sum(jnp.stack(outs, axis=0).astype(jnp.float32) * w, axis=0)
        return o.astype(v.dtype)

    return _unblock(lax.map(block, jnp.arange(S // Q_BLOCK)))


def _stick_breaking_attention(q, k, v):
    B, S, H, D = q.shape
    scale = D ** -0.5
    kpos = jnp.arange(S)

    def block(i):
        tq = i * Q_BLOCK + jnp.arange(Q_BLOCK)
        qb = _qblock(q, i)
        z = jnp.einsum('bqhd,bkhd->bhqk', qb, k).astype(jnp.float32) * scale
        strict = tq[:, None] > kpos[None, :]
        log1m = jnp.where(strict, jax.nn.log_sigmoid(-z), 0.0)
        after = lax.cumsum(log1m, axis=3, reverse=True) - log1m
        a = jnp.where(strict, jnp.exp(jax.nn.log_sigmoid(z) + after), 0.0)
        return jnp.einsum('bhqk,bkhd->bqhd', a.astype(v.dtype), v)

    return _unblock(lax.map(block, jnp.arange(S // Q_BLOCK)))


def _layer(h, p_i, positions, norm_attn, w_in, fox_bf, cmp_pe_k, cmp_w1_k, cmp_w2_k,
           cmp_pe_v, cmp_w1_v, cmp_w2_v, w_o, norm_mlp, w_up, w_down, norm_ple, w_ple_gate, w_ple_proj):
    B, S, _ = h.shape
    x = _rmsnorm(h, norm_attn)
    proj = x @ w_in
    (q_a, kc_a, vc_a, ks_a, vs_a, kw_a, vw_a, g_a,
     q_b, k_b, v_b, f_b,
     q_c, k_c, v_c,
     q_d, k_d, v_d) = jnp.split(proj, np.cumsum(SPLITS)[:-1].tolist(), axis=-1)

    o_a = _nsa_attention(
        _rope(_heads(q_a, H_MIX), positions),
        _heads(kc_a, NSA_KV), _heads(vc_a, NSA_KV),
        _rope(_heads(ks_a, NSA_KV), positions), _heads(vs_a, NSA_KV),
        _rope(_heads(kw_a, NSA_KV), positions), _heads(vw_a, NSA_KV),
        g_a.reshape(B, S, H_MIX, 3),
        cmp_pe_k, cmp_w1_k, cmp_w2_k, cmp_pe_v, cmp_w1_v, cmp_w2_v)
    log_f = jax.nn.log_sigmoid(f_b.astype(jnp.float32) + fox_bf.astype(jnp.float32))
    o_b = _fox_attention(_heads(q_b, H_MIX), _heads(k_b, H_MIX), _heads(v_b, H_MIX), log_f)
    o_c = _dilated_attention(_rope(_heads(q_c, H_MIX), positions),
                             _rope(_heads(k_c, H_MIX), positions), _heads(v_c, H_MIX))
    o_d = _stick_breaking_attention(_heads(q_d, H_MIX), _heads(k_d, H_MIX), _heads(v_d, H_MIX))

    mix = jnp.concatenate([o.reshape(B, S, H_MIX * HEAD_DIM) for o in (o_a, o_b, o_c, o_d)], axis=-1)
    h = h + mix @ w_o

    x2 = _rmsnorm(h, norm_mlp)
    h = h + jnp.square(jax.nn.relu(x2 @ w_up)) @ w_down

    gate = jax.nn.sigmoid(_rmsnorm(h, norm_ple) @ w_ple_gate)
    h = h + gate * (p_i @ w_ple_proj)
    return h


def setup_inputs(seed: int = 0) -> dict:
    key = jax.random.key(seed)
    ks = jax.random.split(key, 24)
    f32 = jnp.float32

    def nrm(k, shape, fan_in):
        return jax.random.normal(k, shape, f32) * (fan_in ** -0.5)

    def gain(k, shape):
        return 1.0 + 0.05 * jax.random.normal(k, shape, f32)

    L, Dh = NSA_CMP_LEN, HEAD_DIM
    positions = (jnp.arange(SEQ, dtype=jnp.int32)[None, :]
                 + jax.random.randint(ks[2], (BATCH, 1), 0, 1024, dtype=jnp.int32))
    return {
        "x": jax.random.normal(ks[0], (BATCH, SEQ, D_MODEL), f32),
        "p": jax.random.normal(ks[1], (DEPTH, BATCH, SEQ, PLE_DIM), f32),
        "positions": positions,
        "norm_attn": gain(ks[3], (DEPTH, D_MODEL)),
        "w_in": nrm(ks[4], (DEPTH, D_MODEL, N_IN), D_MODEL),
        "fox_bf": 3.0 + 0.5 * jax.random.normal(ks[5], (DEPTH, H_MIX), f32),
        "cmp_pe_k": 0.5 * jax.random.normal(ks[6], (DEPTH, L, Dh), f32),
        "cmp_w1_k": nrm(ks[7], (DEPTH, L * Dh, NSA_CMP_HIDDEN), L * Dh),
        "cmp_w2_k": nrm(ks[8], (DEPTH, NSA_CMP_HIDDEN, Dh), NSA_CMP_HIDDEN),
        "cmp_pe_v": 0.5 * jax.random.normal(ks[9], (DEPTH, L, Dh), f32),
        "cmp_w1_v": nrm(ks[10], (DEPTH, L * Dh, NSA_CMP_HIDDEN), L * Dh),
        "cmp_w2_v": nrm(ks[11], (DEPTH, NSA_CMP_HIDDEN, Dh), NSA_CMP_HIDDEN),
        "w_o": nrm(ks[12], (DEPTH, D_MIX, D_MODEL), D_MIX),
        "norm_mlp": gain(ks[13], (DEPTH, D_MODEL)),
        "w_up": nrm(ks[14], (DEPTH, D_MODEL, D_FF), D_MODEL),
        "w_down": nrm(ks[15], (DEPTH, D_FF, D_MODEL), D_FF),
        "norm_ple": gain(ks[16], (DEPTH, D_MODEL)),
        "w_ple_gate": nrm(ks[17], (DEPTH, D_MODEL, D_MODEL), D_MODEL),
        "w_ple_proj": nrm(ks[18], (DEPTH, PLE_DIM, D_MODEL), PLE_DIM),
        "norm_final": gain(ks[19], (D_MODEL,)),
    }


def reference(x, p, positions, norm_attn, w_in, fox_bf, cmp_pe_k, cmp_w1_k, cmp_w2_k,
              cmp_pe_v, cmp_w1_v, cmp_w2_v, w_o, norm_mlp, w_up, w_down, norm_ple,
              w_ple_gate, w_ple_proj, norm_final):
    h = x
    for i in range(DEPTH):
        h = _layer(h, p[i], positions, norm_attn[i], w_in[i], fox_bf[i],
                   cmp_pe_k[i], cmp_w1_k[i], cmp_w2_k[i], cmp_pe_v[i], cmp_w1_v[i], cmp_w2_v[i],
                   w_o[i], norm_mlp[i], w_up[i], w_down[i], norm_ple[i], w_ple_gate[i], w_ple_proj[i])
    return _rmsnorm(h, norm_final)
```

```python
import functools
import math

import numpy as np
import jax
import jax.numpy as jnp
from jax import lax
from jax.experimental import pallas as pl
from jax.experimental.pallas import tpu as pltpu

HEAD_DIM = 128
H_MIX = 8
NSA_KV = 2
NSA_HPG = H_MIX // NSA_KV
CMP_LEN = 32
CMP_STRIDE = 16
CMP_HIDDEN = 256
SLC_LEN = 64
TOPK = 16
NSA_WINDOW = 512
DIL_CONFIGS = ((128, 1), (512, 4), (2048, 16))
ROPE_THETA = 10000.0
RMS_EPS = 1e-6
NEG = -1e30
M_INIT = -5e29
FORCED_SCORE = 1e9
SCALE = HEAD_DIM ** -0.5

VMEM_LIMIT_BYTES = 52 * 1024 * 1024

HD_QA, HD_KC, HD_VC, HD_KS, HD_VS, HD_KW, HD_VW = 0, 8, 10, 12, 14, 16, 18
HD_QB, HD_KB, HD_VB = 20, 28, 36
HD_QC, HD_KC2, HD_VC2 = 44, 52, 60
HD_QD, HD_KD, HD_VD = 68, 76, 84
N_HEADS_ALL = 92
N_BIG = N_HEADS_ALL * HEAD_DIM
GATE_COL0 = 0
FORGET_COL0 = 24


def _cparams(sem):
    return pltpu.CompilerParams(dimension_semantics=sem, vmem_limit_bytes=VMEM_LIMIT_BYTES)


def _rmsnorm_kernel(x_ref, g_ref, o_ref):
    x = x_ref[...]
    ms = jnp.mean(x * x, axis=-1, keepdims=True)
    o_ref[...] = (x * lax.rsqrt(ms + RMS_EPS) * g_ref[...]).astype(o_ref.dtype)


def _rmsnorm(x, g, out_dtype, tm=256):
    m, d = x.shape
    return pl.pallas_call(
        _rmsnorm_kernel,
        out_shape=jax.ShapeDtypeStruct((m, d), out_dtype),
        grid=(m // tm,),
        in_specs=[pl.BlockSpec((tm, d), lambda i: (i, 0)),
                  pl.BlockSpec((1, d), lambda i: (0, 0))],
        out_specs=pl.BlockSpec((tm, d), lambda i: (i, 0)),
        compiler_params=_cparams(("parallel",)),
        name="rmsnorm",
    )(x, g.reshape(1, d))


def _mm_kernel(*refs, nk, n_extra, epilogue):
    x_ref, w_ref = refs[0], refs[1]
    extras = refs[2:2 + n_extra]
    o_ref = refs[2 + n_extra]

    def partial():
        return jnp.dot(x_ref[...].astype(jnp.bfloat16), w_ref[...].astype(jnp.bfloat16),
                       preferred_element_type=jnp.float32)

    if nk == 1:
        o_ref[...] = epilogue(partial(), *extras).astype(o_ref.dtype)
        return
    acc_ref = refs[3 + n_extra]
    k = pl.program_id(2)

    @pl.when(k == 0)
    def _():
        acc_ref[...] = partial()

    @pl.when(k > 0)
    def _():
        acc_ref[...] += partial()

    @pl.when(k == nk - 1)
    def _():
        o_ref[...] = epilogue(acc_ref[...], *extras).astype(o_ref.dtype)


def _epi_none(acc):
    return acc


def _epi_relu2(acc):
    r = jnp.maximum(acc, 0.0)
    return r * r


def _epi_residual(acc, h_ref):
    return h_ref[...] + acc


def _epi_ple(acc, h_ref, p_ref, wp_ref):
    pp = jnp.dot(p_ref[...].astype(jnp.bfloat16), wp_ref[...].astype(jnp.bfloat16),
                 preferred_element_type=jnp.float32)
    return h_ref[...] + jax.nn.sigmoid(acc) * pp


def _matmul(x, w, out_dtype, epilogue=_epi_none, extras=(), extra_specs=(), tm=1024, tn=512, tk=2048,
            name="matmul"):
    m, kdim = x.shape
    n = w.shape[1]
    tm, tn, tk = min(tm, m), min(tn, n), min(tk, kdim)
    nk = kdim // tk
    scratch = [pltpu.VMEM((tm, tn), jnp.float32)] if nk > 1 else []
    return pl.pallas_call(
        functools.partial(_mm_kernel, nk=nk, n_extra=len(extras), epilogue=epilogue),
        out_shape=jax.ShapeDtypeStruct((m, n), out_dtype),
        grid=(m // tm, n // tn, nk),
        in_specs=[pl.BlockSpec((tm, tk), lambda i, j, k: (i, k)),
                  pl.BlockSpec((tk, tn), lambda i, j, k: (k, j)),
                  *extra_specs],
        out_specs=pl.BlockSpec((tm, tn), lambda i, j, k: (i, j)),
        scratch_shapes=scratch,
        compiler_params=_cparams(("parallel", "parallel", "arbitrary")),
        name=name,
    )(x, w, *extras)


def _rope_table_kernel(pos_ref, freq_ref, sign_ref, cos_ref, sin_ref):
    ang = pos_ref[...] * freq_ref[...]
    cos_ref[...] = jnp.cos(ang)
    sin_ref[...] = jnp.sin(ang) * sign_ref[...]


def _rope_tables(positions, ts=512):
    n = positions.size
    half = HEAD_DIM // 2
    inv_freq = ROPE_THETA ** (-jnp.arange(half, dtype=jnp.float32) / half)
    freq = jnp.concatenate([inv_freq, inv_freq]).reshape(1, HEAD_DIM)
    sign = jnp.concatenate([-jnp.ones((half,), jnp.float32), jnp.ones((half,), jnp.float32)]).reshape(1, HEAD_DIM)
    pos = positions.astype(jnp.float32).reshape(n, 1)
    row = pl.BlockSpec((1, HEAD_DIM), lambda i: (0, 0))
    return pl.pallas_call(
        _rope_table_kernel,
        out_shape=[jax.ShapeDtypeStruct((n, HEAD_DIM), jnp.float32)] * 2,
        grid=(n // ts,),
        in_specs=[pl.BlockSpec((ts, 1), lambda i: (i, 0)), row, row],
        out_specs=[pl.BlockSpec((ts, HEAD_DIM), lambda i: (i, 0))] * 2,
        compiler_params=_cparams(("parallel",)),
        name="rope_tables",
    )(pos, freq, sign)


_PREP_HEADS = 2


def _prep_kernel(x_ref, cos_ref, sin_ref, o_ref):
    c = pl.program_id(2)
    roped = (c < 4) | (c == 6) | (c == 8) | ((c >= 22) & (c < 30))

    @pl.when(roped)
    def _():
        cos, sin = cos_ref[0], sin_ref[0]
        for hh in range(_PREP_HEADS):
            t = x_ref[0, :, hh * HEAD_DIM:(hh + 1) * HEAD_DIM]
            o_ref[0, hh] = (t * cos + pltpu.roll(t, HEAD_DIM // 2, 1) * sin).astype(o_ref.dtype)

    @pl.when(jnp.logical_not(roped))
    def _():
        for hh in range(_PREP_HEADS):
            o_ref[0, hh] = x_ref[0, :, hh * HEAD_DIM:(hh + 1) * HEAD_DIM].astype(o_ref.dtype)


def _prep_heads(proj, cos, sin, ts=1024):
    b, s, _ = proj.shape
    wcol = _PREP_HEADS * HEAD_DIM
    return pl.pallas_call(
        _prep_kernel,
        out_shape=jax.ShapeDtypeStruct((b, N_HEADS_ALL, s, HEAD_DIM), jnp.bfloat16),
        grid=(b, s // ts, N_BIG // wcol),
        in_specs=[pl.BlockSpec((1, ts, wcol), lambda bi, i, c: (bi, i, c)),
                  pl.BlockSpec((1, ts, HEAD_DIM), lambda bi, i, c: (bi, i, 0)),
                  pl.BlockSpec((1, ts, HEAD_DIM), lambda bi, i, c: (bi, i, 0))],
        out_specs=pl.BlockSpec((1, _PREP_HEADS, ts, HEAD_DIM), lambda bi, i, c: (bi, c, i, 0)),
        compiler_params=_cparams(("parallel", "parallel", "parallel")),
        name="prep_heads",
    )(proj, cos, sin)


def _softplus(z):
    return jnp.maximum(z, 0.0) + jnp.log1p(jnp.exp(-jnp.abs(z)))


def _logf_cumsum_kernel(x_ref, bias_ref, o_ref, *, blk):
    s = x_ref.shape[1]
    r = lax.broadcasted_iota(jnp.int32, (blk, blk), 0)
    c = lax.broadcasted_iota(jnp.int32, (blk, blk), 1)
    tri = jnp.where(c <= r, 1.0, 0.0).astype(jnp.float32)

    def body(i, carry):
        x = x_ref[0, pl.ds(i * blk, blk), :]
        logf = -_softplus(-(x + bias_ref[...]))
        cs = jnp.dot(tri, logf, preferred_element_type=jnp.float32,
                     precision=lax.Precision.HIGHEST) + carry
        o_ref[0, pl.ds(i * blk, blk), :] = cs
        return cs[blk - 1:blk, :]

    lax.fori_loop(0, s // blk, body, jnp.zeros((1, x_ref.shape[2]), jnp.float32))


def _logf_cumsum(small, bias_row, blk=128):
    b, s, w = small.shape
    return pl.pallas_call(
        functools.partial(_logf_cumsum_kernel, blk=blk),
        out_shape=jax.ShapeDtypeStruct((b, s, w), jnp.float32),
        grid=(b,),
        in_specs=[pl.BlockSpec((1, s, w), lambda bi: (bi, 0, 0)),
                  pl.BlockSpec((1, w), lambda bi: (0, 0))],
        out_specs=pl.BlockSpec((1, s, w), lambda bi: (bi, 0, 0)),
        compiler_params=_cparams(("parallel",)),
        name="logf_cumsum",
    )(small, bias_row)


def _flash_kernel(*refs, mode, tq, tk, nkb, window, seg_tiles, has_lse):
    it = iter(refs)
    q_ref, k_ref, v_ref = next(it), next(it), next(it)
    bias_ref = next(it) if mode == "fox" else None
    sel_ref, exp_ref = (next(it), next(it)) if mode == "sel" else (None, None)
    o_ref = next(it)
    lse_ref = next(it) if has_lse else None
    m_scr, l_scr, acc_scr = next(it), next(it), next(it)

    i, j = pl.program_id(2), pl.program_id(3)

    @pl.when(j == 0)
    def _():
        m_scr[...] = jnp.full(m_scr.shape, M_INIT, jnp.float32)
        l_scr[...] = jnp.zeros(l_scr.shape, jnp.float32)
        acc_scr[...] = jnp.zeros(acc_scr.shape, jnp.float32)

    if mode == "band":
        kt = i - (nkb - 1) + j
        needed = kt >= (i // seg_tiles) * seg_tiles
    else:
        kt = j
        needed = j * tk <= i * tq + (tq - 1)

    @pl.when(needed)
    def _():
        q = q_ref[0, 0]
        k = k_ref[0, 0]
        s = lax.dot_general(q, k, (((1,), (1,)), ((), ())), preferred_element_type=jnp.float32) * SCALE
        qpos = i * tq + lax.broadcasted_iota(jnp.int32, (tq, tk), 0)
        kpos = kt * tk + lax.broadcasted_iota(jnp.int32, (tq, tk), 1)
        if mode == "fox":
            s = s + bias_ref[0, 0]
            mask = kpos <= qpos
        elif mode == "sel":
            chosen = jnp.dot(sel_ref[0, 0], exp_ref[...], preferred_element_type=jnp.float32)
            mask = (kpos <= qpos) & (chosen > 0.5)
        else:
            dist = qpos - kpos
            mask = (dist >= 0) & (dist <= window)
        s = jnp.where(mask, s, NEG)
        m_prev = m_scr[...]
        m_new = jnp.maximum(m_prev, jnp.max(s, axis=-1, keepdims=True))
        alpha = jnp.exp(m_prev - m_new)
        p = jnp.exp(s - m_new)
        l_scr[...] = alpha * l_scr[...] + jnp.sum(p, axis=-1, keepdims=True)
        acc_scr[...] = alpha * acc_scr[...] + jnp.dot(p.astype(v_ref.dtype), v_ref[0, 0],
                                                      preferred_element_type=jnp.float32)
        m_scr[...] = m_new

    @pl.when(j == pl.num_programs(3) - 1)
    def _():
        l = l_scr[...]
        o_ref[...] = (acc_scr[...] / l).reshape(o_ref.shape).astype(o_ref.dtype)
        if has_lse:
            lse_ref[...] = jnp.broadcast_to(m_scr[...] + jnp.log(l), (tq, HEAD_DIM)).reshape(lse_ref.shape)


def _flash(q_arr, k_arr, v_arr, *, mode, n_heads, q_off, k_off, v_off, hpg=1, tq, tk,
           window=0, seg_len=None, bias=None, sel=None, expand=None, out_layout, out_dtype, has_lse=False,
           name="flash"):
    b, _, s, d = q_arr.shape
    nq = s // tq
    if mode == "band":
        assert tq == tk
        nkb = -(-window // tk) + 1
        seg_tiles = (seg_len or s) // tq
        nsteps = nkb

        def kt_of(i, j):
            return jnp.maximum(i - (nkb - 1) + j, (i // seg_tiles) * seg_tiles)
    else:
        nkb, seg_tiles = 0, 1
        nsteps = s // tk

        def kt_of(i, j):
            return jnp.minimum(j, (i * tq + tq - 1) // tk)

    in_specs = [pl.BlockSpec((1, 1, tq, d), lambda bi, h, i, j: (bi, q_off + h, i, 0)),
                pl.BlockSpec((1, 1, tk, d), lambda bi, h, i, j: (bi, k_off + h // hpg, kt_of(i, j), 0)),
                pl.BlockSpec((1, 1, tk, d), lambda bi, h, i, j: (bi, v_off + h // hpg, kt_of(i, j), 0))]
    args = [q_arr, k_arr, v_arr]
    if mode == "fox":
        in_specs.append(pl.BlockSpec((1, 1, 1, tk), lambda bi, h, i, j: (bi, h, 0, kt_of(i, j))))
        args.append(bias)
    if mode == "sel":
        in_specs.append(pl.BlockSpec((1, 1, tq, sel.shape[-1]), lambda bi, h, i, j: (bi, h // hpg, i, 0)))
        in_specs.append(pl.BlockSpec((expand.shape[0], tk), lambda bi, h, i, j: (0, kt_of(i, j))))
        args += [sel, expand]
    if out_layout == "bsh":
        o_shape = (b, s, n_heads * d)
        o_spec = pl.BlockSpec((1, tq, d), lambda bi, h, i, j: (bi, i, h))
    else:
        o_shape = (b, n_heads, s, d)
        o_spec = pl.BlockSpec((1, 1, tq, d), lambda bi, h, i, j: (bi, h, i, 0))
    out_shape = [jax.ShapeDtypeStruct(o_shape, out_dtype)]
    out_specs = [o_spec]
    if has_lse:
        out_shape.append(jax.ShapeDtypeStruct((b, n_heads, s, d), jnp.float32))
        out_specs.append(pl.BlockSpec((1, 1, tq, d), lambda bi, h, i, j: (bi, h, i, 0)))
    res = pl.pallas_call(
        functools.partial(_flash_kernel, mode=mode, tq=tq, tk=tk, nkb=nkb, window=window,
                          seg_tiles=seg_tiles, has_lse=has_lse),
        out_shape=out_shape,
        grid=(b, n_heads, nq, nsteps),
        in_specs=in_specs,
        out_specs=out_specs,
        scratch_shapes=[pltpu.VMEM((tq, 1), jnp.float32), pltpu.VMEM((tq, 1), jnp.float32),
                        pltpu.VMEM((tq, d), jnp.float32)],
        compiler_params=_cparams(("parallel", "parallel", "parallel", "arbitrary")),
        name=name,
    )(*args)
    return res if has_lse else res[0]


def _stickbreak_kernel(q_ref, k_ref, v_ref, u_ref, o_ref, run_scr, acc_scr, *, t):
    i, j = pl.program_id(2), pl.program_id(3)

    @pl.when(j == 0)
    def _():
        run_scr[...] = jnp.zeros(run_scr.shape, jnp.float32)
        acc_scr[...] = jnp.zeros(acc_scr.shape, jnp.float32)

    kt = i - j

    @pl.when(kt >= 0)
    def _():
        z = lax.dot_general(q_ref[0, 0], k_ref[0, 0], (((1,), (1,)), ((), ())),
                            preferred_element_type=jnp.float32) * SCALE
        qpos = i * t + lax.broadcasted_iota(jnp.int32, (t, t), 0)
        kpos = kt * t + lax.broadcasted_iota(jnp.int32, (t, t), 1)
        strict = kpos < qpos
        sp = _softplus(z)
        log1m = jnp.where(strict, -sp, 0.0)
        hi = log1m.astype(jnp.bfloat16)
        lo = (log1m - hi.astype(jnp.float32)).astype(jnp.bfloat16)
        after = (jnp.dot(hi, u_ref[...], preferred_element_type=jnp.float32)
                 + jnp.dot(lo, u_ref[...], preferred_element_type=jnp.float32))
        a = jnp.where(strict, jnp.exp((z - sp) + (after + run_scr[...])), 0.0)
        acc_scr[...] += jnp.dot(a.astype(v_ref.dtype), v_ref[0, 0], preferred_element_type=jnp.float32)
        run_scr[...] += jnp.sum(log1m, axis=-1, keepdims=True)

    @pl.when(j == pl.num_programs(3) - 1)
    def _():
        o_ref[...] = acc_scr[...].reshape(o_ref.shape).astype(o_ref.dtype)


def _stickbreak(hm, *, q_off, k_off, v_off, n_heads, t, out_dtype):
    b, _, s, d = hm.shape
    nq = s // t
    later = np.arange(t)[:, None] > np.arange(t)[None, :]
    u = jnp.asarray(later, jnp.bfloat16)

    def kt_of(i, j):
        return jnp.maximum(i - j, 0)

    return pl.pallas_call(
        functools.partial(_stickbreak_kernel, t=t),
        out_shape=jax.ShapeDtypeStruct((b, s, n_heads * d), out_dtype),
        grid=(b, n_heads, nq, nq),
        in_specs=[pl.BlockSpec((1, 1, t, d), lambda bi, h, i, j: (bi, q_off + h, i, 0)),
                  pl.BlockSpec((1, 1, t, d), lambda bi, h, i, j: (bi, k_off + h, kt_of(i, j), 0)),
                  pl.BlockSpec((1, 1, t, d), lambda bi, h, i, j: (bi, v_off + h, kt_of(i, j), 0)),
                  pl.BlockSpec((t, t), lambda bi, h, i, j: (0, 0))],
        out_specs=pl.BlockSpec((1, t, d), lambda bi, h, i, j: (bi, i, h)),
        scratch_shapes=[pltpu.VMEM((t, 1), jnp.float32), pltpu.VMEM((t, d), jnp.float32)],
        compiler_params=_cparams(("parallel", "parallel", "parallel", "arbitrary")),
        name="stickbreak",
    )(hm, hm, hm, u)


def _gelu_tanh(x):
    return 0.5 * x * (1.0 + jnp.tanh(math.sqrt(2.0 / math.pi) * (x + 0.044715 * (x * x * x))))


def _compress_kernel(x_ref, pe_ref, w1_ref, w2_ref, o_ref):
    half = x_ref.shape[3]
    x = x_ref[0, 0]
    w1 = w1_ref[0].astype(jnp.bfloat16)
    first = jnp.dot(x, w1[:half], preferred_element_type=jnp.float32)
    second = jnp.dot(x, w1[half:], preferred_element_type=jnp.float32)
    pe = jnp.broadcast_to(pe_ref[0], (8, pe_ref.shape[2])).astype(jnp.bfloat16)
    pe_term = jnp.dot(pe, w1, preferred_element_type=jnp.float32)[0:1]
    n_chunks = x.shape[0]
    hid = first + pltpu.roll(second, n_chunks - 1, 0) + pe_term
    act = _gelu_tanh(hid)
    o_ref[0, 0] = jnp.dot(act.astype(jnp.bfloat16), w2_ref[0].astype(jnp.bfloat16),
                          preferred_element_type=jnp.float32)


def _compress(chunks, pe, w1, w2):
    b, nh, nc, half = chunks.shape
    return pl.pallas_call(
        _compress_kernel,
        out_shape=jax.ShapeDtypeStruct((b, nh, nc, HEAD_DIM), jnp.float32),
        grid=(b, nh),
        in_specs=[pl.BlockSpec((1, 1, nc, half), lambda bi, h: (bi, h, 0, 0)),
                  pl.BlockSpec((1, 1, 2 * half), lambda bi, h: (h // NSA_KV, 0, 0)),
                  pl.BlockSpec((1, 2 * half, w1.shape[2]), lambda bi, h: (h // NSA_KV, 0, 0)),
                  pl.BlockSpec((1, w2.shape[1], HEAD_DIM), lambda bi, h: (h // NSA_KV, 0, 0))],
        out_specs=pl.BlockSpec((1, 1, nc, HEAD_DIM), lambda bi, h: (bi, h, 0, 0)),
        compiler_params=_cparams(("parallel", "parallel")),
        name="nsa_compress",
    )(chunks, pe, w1, w2)


def _cmp_select_kernel(q_ref, kc_ref, vc_ref, c2s_ref, o_ref, sel_ref, *, tq, n_cmp):
    i = pl.program_id(2)
    nc = kc_ref.shape[2]
    kc = kc_ref[0, 0].astype(jnp.bfloat16)
    vc = vc_ref[0, 0].astype(jnp.bfloat16)
    t = i * tq + lax.broadcasted_iota(jnp.int32, (tq, 1), 0)
    cidx = lax.broadcasted_iota(jnp.int32, (1, nc), 1)
    visible = (cidx * CMP_STRIDE + (CMP_LEN - 1) <= t) & (cidx < n_cmp)
    psum = jnp.zeros((tq, nc), jnp.float32)
    for hh in range(NSA_HPG):
        s = lax.dot_general(q_ref[0, hh], kc, (((1,), (1,)), ((), ())),
                            preferred_element_type=jnp.float32) * SCALE
        s = jnp.where(visible, s, NEG)
        m = jnp.max(s, axis=-1, keepdims=True)
        e = jnp.where(visible, jnp.exp(s - m), 0.0)
        l = jnp.sum(e, axis=-1, keepdims=True)
        p = e / jnp.where(l > 0.0, l, 1.0)
        psum = psum + p
        o_ref[0, :, hh * HEAD_DIM:(hh + 1) * HEAD_DIM] = jnp.dot(
            p.astype(jnp.bfloat16), vc, preferred_element_type=jnp.float32)

    imp = jnp.dot(psum, c2s_ref[...], preferred_element_type=jnp.float32, precision=lax.Precision.HIGHEST)
    w = imp.shape[1]
    n_slc = w // 2
    jblk = lax.broadcasted_iota(jnp.int32, (1, w), 1)
    cur = lax.shift_right_logical(t, int(math.log2(SLC_LEN)))
    forced = (jblk == 0) | (jblk == cur) | (jblk == cur - 1)
    causal_blk = jblk * SLC_LEN <= t
    score = jnp.where(causal_blk, jnp.where(forced, FORCED_SCORE, imp), -1.0)
    rank = jnp.zeros((tq, w), jnp.float32)
    for jp in range(n_slc):
        col = score[:, jp:jp + 1]
        before = (col > score) | ((col == score) & (jblk > jp))
        rank = rank + jnp.where(before, 1.0, 0.0)
    chosen = (rank < float(min(TOPK, n_slc))) & causal_blk
    sel_ref[0, 0] = jnp.where(chosen, 1.0, 0.0).astype(sel_ref.dtype)


def _cmp_select(hm, cmp_kv, c2s, *, tq, n_cmp):
    b, _, s, d = hm.shape
    nc = cmp_kv.shape[2]
    w = c2s.shape[1]
    return pl.pallas_call(
        functools.partial(_cmp_select_kernel, tq=tq, n_cmp=n_cmp),
        out_shape=[jax.ShapeDtypeStruct((b, s, H_MIX * d), jnp.float32),
                   jax.ShapeDtypeStruct((b, NSA_KV, s, w), jnp.bfloat16)],
        grid=(b, NSA_KV, s // tq),
        in_specs=[pl.BlockSpec((1, NSA_HPG, tq, d), lambda bi, g, i: (bi, g, i, 0)),
                  pl.BlockSpec((1, 1, nc, d), lambda bi, g, i: (bi, g, 0, 0)),
                  pl.BlockSpec((1, 1, nc, d), lambda bi, g, i: (bi, NSA_KV + g, 0, 0)),
                  pl.BlockSpec((nc, w), lambda bi, g, i: (0, 0))],
        out_specs=[pl.BlockSpec((1, tq, NSA_HPG * d), lambda bi, g, i: (bi, i, g)),
                   pl.BlockSpec((1, 1, tq, w), lambda bi, g, i: (bi, g, i, 0))],
        compiler_params=_cparams(("parallel", "parallel", "parallel")),
        name="nsa_cmp_select",
    )(hm, cmp_kv, cmp_kv, c2s)


def _nsa_combine_kernel(oc_ref, os_ref, ow_ref, g_ref, o_ref):
    gates = jax.nn.sigmoid(g_ref[0])
    for h in range(H_MIX):
        cols = slice(h * HEAD_DIM, (h + 1) * HEAD_DIM)
        c0 = GATE_COL0 + 3 * h
        out = (gates[:, c0:c0 + 1] * oc_ref[0, :, cols]
               + gates[:, c0 + 1:c0 + 2] * os_ref[0, :, cols]
               + gates[:, c0 + 2:c0 + 3] * ow_ref[0, :, cols])
        o_ref[0, :, cols] = out.astype(o_ref.dtype)


def _nsa_combine(o_cmp, o_slc, o_win, small, tq=512):
    b, s, w = o_cmp.shape
    big = pl.BlockSpec((1, tq, w), lambda bi, i: (bi, i, 0))
    return pl.pallas_call(
        _nsa_combine_kernel,
        out_shape=jax.ShapeDtypeStruct((b, s, w), jnp.bfloat16),
        grid=(b, s // tq),
        in_specs=[big, big, big, pl.BlockSpec((1, tq, small.shape[2]), lambda bi, i: (bi, i, 0))],
        out_specs=big,
        compiler_params=_cparams(("parallel", "parallel")),
        name="nsa_combine",
    )(o_cmp, o_slc, o_win, small)


def _dil_combine_kernel(o1, o2, o3, l1, l2, l3, o_ref):
    a, bb, c = l1[0, 0], l2[0, 0], l3[0, 0]
    m = jnp.maximum(jnp.maximum(a, bb), c)
    ea, eb, ec = jnp.exp(a - m), jnp.exp(bb - m), jnp.exp(c - m)
    tot = ea + eb + ec
    out = o1[0, 0] * (ea / tot) + o2[0, 0] * (eb / tot) + o3[0, 0] * (ec / tot)
    o_ref[0] = out.astype(o_ref.dtype)


def _dil_combine(outs, lses, tq=1024):
    b, h, s, d = outs[0].shape
    spec = pl.BlockSpec((1, 1, tq, d), lambda bi, hh, i: (bi, hh, i, 0))
    return pl.pallas_call(
        _dil_combine_kernel,
        out_shape=jax.ShapeDtypeStruct((b, s, h * d), jnp.bfloat16),
        grid=(b, h, s // tq),
        in_specs=[spec] * 6,
        out_specs=pl.BlockSpec((1, tq, d), lambda bi, hh, i: (bi, i, hh)),
        compiler_params=_cparams(("parallel", "parallel", "parallel")),
        name="dil_combine",
    )(*outs, *lses)


def _to_residue_major(t, dil):
    b, h, s, d = t.shape
    return t.reshape(b, h, s // dil, dil, d).transpose(0, 1, 3, 2, 4).reshape(b, h, s, d)


def _from_residue_major(t, dil):
    b, h, s, d = t.shape
    return t.reshape(b, h, dil, s // dil, d).transpose(0, 1, 3, 2, 4).reshape(b, h, s, d)


def _cmp_to_slc_matrix(n_chunks, n_cmp, n_slc):
    ratio = SLC_LEN // CMP_STRIDE
    span = CMP_LEN // CMP_STRIDE
    jj, mm, nn = np.meshgrid(np.arange(n_slc), np.arange(ratio), np.arange(span), indexing="ij")
    cc = ratio * jj + mm + nn
    keep = cc < n_cmp
    mat = np.zeros((n_chunks, 2 * n_slc), np.float32)
    np.add.at(mat, (cc[keep], jj[keep]), 1.0)
    return jnp.asarray(mat)


def _mixers(hm, small, fox_bf, cmp_pe_k, cmp_w1_k, cmp_w2_k, cmp_pe_v, cmp_w1_v, cmp_w2_v):
    b, _, s, d = hm.shape
    bf16 = jnp.bfloat16

    n_chunks = s // CMP_STRIDE
    n_cmp = (s - CMP_LEN) // CMP_STRIDE + 1
    n_slc = s // SLC_LEN
    chunks = hm[:, HD_KC:HD_KC + 2 * NSA_KV].reshape(b, 2 * NSA_KV, n_chunks, CMP_STRIDE * d)
    cmp_kv = _compress(chunks,
                       jnp.stack([cmp_pe_k, cmp_pe_v]).reshape(2, 1, CMP_LEN * d),
                       jnp.stack([cmp_w1_k, cmp_w1_v]), jnp.stack([cmp_w2_k, cmp_w2_v]))
    o_cmp, sel = _cmp_select(hm, cmp_kv, _cmp_to_slc_matrix(n_chunks, n_cmp, n_slc), tq=256, n_cmp=n_cmp)
    expand = jnp.asarray(np.arange(2 * n_slc)[:, None] == (np.arange(s)[None, :] // SLC_LEN), bf16)
    o_slc = _flash(hm, hm, hm, mode="sel", n_heads=H_MIX, q_off=HD_QA, k_off=HD_KS, v_off=HD_VS, hpg=NSA_HPG,
                   tq=512, tk=512, sel=sel, expand=expand, out_layout="bsh", out_dtype=jnp.float32,
                   name="nsa_selected")
    o_win = _flash(hm, hm, hm, mode="band", n_heads=H_MIX, q_off=HD_QA, k_off=HD_KW, v_off=HD_VW, hpg=NSA_HPG,
                   tq=512, tk=512, window=NSA_WINDOW - 1, out_layout="bsh", out_dtype=jnp.float32,
                   name="nsa_window")
    o_a = _nsa_combine(o_cmp, o_slc, o_win, small)

    bias_row = jnp.zeros((1, small.shape[2]), jnp.float32).at[0, FORGET_COL0:FORGET_COL0 + H_MIX].set(fox_bf)
    csum = _logf_cumsum(small, bias_row)
    key_bias = -csum[:, :, FORGET_COL0:FORGET_COL0 + H_MIX].transpose(0, 2, 1).reshape(b, H_MIX, 1, s)
    o_b = _flash(hm, hm, hm, mode="fox", n_heads=H_MIX, q_off=HD_QB, k_off=HD_KB, v_off=HD_VB,
                 tq=512, tk=512, bias=key_bias, out_layout="bsh", out_dtype=bf16, name="fox")

    outs, lses = [], []
    for window, dil in DIL_CONFIGS:
        if dil == 1:
            src, offs = hm, (HD_QC, HD_KC2, HD_VC2)
        else:
            src, offs = _to_residue_major(hm[:, HD_QC:HD_QC + 3 * H_MIX], dil), (0, H_MIX, 2 * H_MIX)
        o, lse = _flash(src, src, src, mode="band", n_heads=H_MIX, q_off=offs[0], k_off=offs[1], v_off=offs[2],
                        tq=256, tk=256, window=window // dil, seg_len=s // dil, out_layout="bhs",
                        out_dtype=jnp.float32, has_lse=True, name=f"dilated_{dil}")
        if dil != 1:
            o, lse = _from_residue_major(o, dil), _from_residue_major(lse, dil)
        outs.append(o)
        lses.append(lse)
    o_c = _dil_combine(outs, lses)

    o_d = _stickbreak(hm, q_off=HD_QD, k_off=HD_KD, v_off=HD_VD, n_heads=H_MIX, t=256, out_dtype=bf16)

    return jnp.concatenate([o_a, o_b, o_c, o_d], axis=-1)


def _split_w_in(w_in):
    a = H_MIX * HEAD_DIM + 6 * NSA_KV * HEAD_DIM
    g_end = a + 3 * H_MIX
    f0 = g_end + 3 * H_MIX * HEAD_DIM
    f_end = f0 + H_MIX
    big = jnp.concatenate([w_in[:, :a], w_in[:, g_end:f0], w_in[:, f_end:]], axis=1)
    small = jnp.concatenate([w_in[:, a:g_end], w_in[:, f0:f_end],
                             jnp.zeros((w_in.shape[0], HEAD_DIM - 4 * H_MIX), w_in.dtype)], axis=1)
    return big, small


def _layer(h, p_i, cos, sin, batch, norm_attn, w_in, fox_bf, cmp_pe_k, cmp_w1_k, cmp_w2_k,
           cmp_pe_v, cmp_w1_v, cmp_w2_v, w_o, norm_mlp, w_up, w_down, norm_ple, w_ple_gate, w_ple_proj):
    m, d = h.shape
    s = m // batch
    bf16 = jnp.bfloat16
    w_big, w_small = _split_w_in(w_in)
    xn = _rmsnorm(h, norm_attn, bf16)
    proj = _matmul(xn, w_big, jnp.float32, name="in_proj")
    small = _matmul(xn, w_small, jnp.float32, tn=HEAD_DIM, tk=d, name="in_proj_small")
    hm = _prep_heads(proj.reshape(batch, s, N_BIG), cos.reshape(batch, s, HEAD_DIM), sin.reshape(batch, s, HEAD_DIM))
    mix = _mixers(hm, small.reshape(batch, s, HEAD_DIM), fox_bf,
                  cmp_pe_k, cmp_w1_k, cmp_w2_k, cmp_pe_v, cmp_w1_v, cmp_w2_v)

    tm, tn = 1024, 512
    res_spec = pl.BlockSpec((tm, tn), lambda i, j, k: (i, j))
    h = _matmul(mix.reshape(m, -1), w_o, jnp.float32, _epi_residual, (h,), (res_spec,), tm=tm, tn=tn, name="out_proj")

    x2 = _rmsnorm(h, norm_mlp, bf16)
    mid = _matmul(x2, w_up, bf16, _epi_relu2, tm=tm, tn=tn, name="mlp_up")
    h = _matmul(mid, w_down, jnp.float32, _epi_residual, (h,), (res_spec,), tm=tm, tn=tn, name="mlp_down")

    x3 = _rmsnorm(h, norm_ple, bf16)
    ple_dim = p_i.shape[1]
    h = _matmul(x3, w_ple_gate, jnp.float32, _epi_ple, (h, p_i, w_ple_proj),
                (res_spec, pl.BlockSpec((tm, ple_dim), lambda i, j, k: (i, 0)),
                 pl.BlockSpec((ple_dim, tn), lambda i, j, k: (0, j))),
                tm=tm, tn=tn, name="ple_gate")
    return h


def kernel(x, p, positions, norm_attn, w_in, fox_bf, cmp_pe_k, cmp_w1_k, cmp_w2_k, cmp_pe_v, cmp_w1_v, cmp_w2_v,
           w_o, norm_mlp, w_up, w_down, norm_ple, w_ple_gate, w_ple_proj, norm_final):
    batch, s, d = x.shape
    depth = p.shape[0]
    cos, sin = _rope_tables(positions)
    h = x.reshape(batch * s, d)
    for i in range(depth):
        h = _layer(h, p[i].reshape(batch * s, -1), cos, sin, batch, norm_attn[i], w_in[i], fox_bf[i],
                   cmp_pe_k[i], cmp_w1_k[i], cmp_w2_k[i], cmp_pe_v[i], cmp_w1_v[i], cmp_w2_v[i],
                   w_o[i], norm_mlp[i], w_up[i], w_down[i], norm_ple[i], w_ple_gate[i], w_ple_proj[i])
    return _rmsnorm(h, norm_final, x.dtype).reshape(batch, s, d)
```

```python
import functools
import math

import numpy as np
import jax
import jax.numpy as jnp
from jax import lax
from jax.experimental import pallas as pl
from jax.experimental.pallas import tpu as pltpu

HEAD_DIM = 128
H_MIX = 8
NSA_KV = 2
NSA_HPG = H_MIX // NSA_KV
CMP_LEN = 32
CMP_STRIDE = 16
SLC_LEN = 64
TOPK = 16
NSA_WINDOW = 512
DIL_CONFIGS = ((128, 1), (512, 4), (2048, 16))
ROPE_THETA = 10000.0
RMS_EPS = 1e-6
NEG = -1e30
M_INIT = -5e29
FORCED_SCORE = 1e9
LOG2E = math.log2(math.e)
Q_SCALE = HEAD_DIM ** -0.5 * LOG2E

VMEM_LIMIT_BYTES = 52 * 1024 * 1024

HD_QA, HD_KC, HD_VC, HD_KS, HD_VS, HD_KW, HD_VW = 0, 8, 10, 12, 14, 16, 18
HD_QB, HD_KB, HD_VB = 20, 28, 36
HD_QC, HD_KC2, HD_VC2 = 44, 52, 60
HD_QD, HD_KD, HD_VD = 68, 76, 84
N_HEADS_ALL = 92
N_BIG = N_HEADS_ALL * HEAD_DIM
GATE_COL0 = 0
FORGET_COL0 = 24
_ROPED_HEADS = (tuple(range(HD_QA, HD_QA + H_MIX)) + tuple(range(HD_KS, HD_KS + NSA_KV))
                + tuple(range(HD_KW, HD_KW + NSA_KV)) + tuple(range(HD_QC, HD_QC + 2 * H_MIX)))
_QUERY_HEADS = tuple(h for q0 in (HD_QA, HD_QB, HD_QC, HD_QD) for h in range(q0, q0 + H_MIX))


def _cparams(sem):
    return pltpu.CompilerParams(dimension_semantics=sem, vmem_limit_bytes=VMEM_LIMIT_BYTES)


def _dot_nt(a, b):
    return lax.dot_general(a, b, (((1,), (1,)), ((), ())), preferred_element_type=jnp.float32)


def _dot(a, b):
    return jnp.dot(a, b, preferred_element_type=jnp.float32)


def _rmsnorm_kernel(x_ref, g_ref, o_ref):
    x = x_ref[...]
    ms = jnp.mean(x * x, axis=-1, keepdims=True)
    o_ref[...] = (x * lax.rsqrt(ms + RMS_EPS) * g_ref[...]).astype(o_ref.dtype)


def _rmsnorm(x, g, out_dtype, tm=256):
    m, d = x.shape
    return pl.pallas_call(
        _rmsnorm_kernel,
        out_shape=jax.ShapeDtypeStruct((m, d), out_dtype),
        grid=(m // tm,),
        in_specs=[pl.BlockSpec((tm, d), lambda i: (i, 0)),
                  pl.BlockSpec((1, d), lambda i: (0, 0))],
        out_specs=pl.BlockSpec((tm, d), lambda i: (i, 0)),
        compiler_params=_cparams(("parallel",)),
        name="rmsnorm",
    )(x, g.reshape(1, d))


def _mm_kernel(*refs, n_extra, epilogue):
    x_ref, w_ref = refs[0], refs[1]
    extras = refs[2:2 + n_extra]
    o_ref = refs[2 + n_extra]
    acc = _dot(x_ref[...].astype(jnp.bfloat16), w_ref[...].astype(jnp.bfloat16))
    o_ref[...] = epilogue(acc, *extras).astype(o_ref.dtype)


def _epi_none(acc):
    return acc


def _epi_relu2(acc):
    r = jnp.maximum(acc, 0.0)
    return r * r


def _epi_residual(acc, h_ref):
    return h_ref[...] + acc


def _epi_ple(acc, h_ref, p_ref, wp_ref):
    pp = _dot(p_ref[...].astype(jnp.bfloat16), wp_ref[...].astype(jnp.bfloat16))
    return h_ref[...] + jax.nn.sigmoid(acc) * pp


def _matmul(x, w, out_dtype, epilogue=_epi_none, extras=(), extra_specs=(), tm=1024, tn=512, name="matmul"):
    m, kdim = x.shape
    n = w.shape[1]
    tn = min(tn, n)
    return pl.pallas_call(
        functools.partial(_mm_kernel, n_extra=len(extras), epilogue=epilogue),
        out_shape=jax.ShapeDtypeStruct((m, n), out_dtype),
        grid=(m // tm, n // tn),
        in_specs=[pl.BlockSpec((tm, kdim), lambda i, j: (i, 0)),
                  pl.BlockSpec((kdim, tn), lambda i, j: (0, j)),
                  *extra_specs],
        out_specs=pl.BlockSpec((tm, tn), lambda i, j: (i, j)),
        compiler_params=_cparams(("parallel", "arbitrary")),
        name=name,
    )(x, w, *extras)


def _mm_residual_ksplit_kernel(x_ref, w_ref, h_ref, o_ref):
    part = _dot(x_ref[...], w_ref[...].astype(jnp.bfloat16))

    @pl.when(pl.program_id(2) == 0)
    def _():
        o_ref[...] = h_ref[...] + part

    @pl.when(pl.program_id(2) > 0)
    def _():
        o_ref[...] += part


def _matmul_residual_ksplit(x, w, h, tm=2048, tn=1024, tk=512, name="matmul_ksplit"):
    m, kdim = x.shape
    n = w.shape[1]
    return pl.pallas_call(
        _mm_residual_ksplit_kernel,
        out_shape=jax.ShapeDtypeStruct((m, n), jnp.float32),
        grid=(m // tm, n // tn, kdim // tk),
        in_specs=[pl.BlockSpec((tm, tk), lambda i, j, k: (i, k)),
                  pl.BlockSpec((tk, tn), lambda i, j, k: (k, j)),
                  pl.BlockSpec((tm, tn), lambda i, j, k: (i, j), pipeline_mode=pl.Buffered(1))],
        out_specs=pl.BlockSpec((tm, tn), lambda i, j, k: (i, j)),
        compiler_params=_cparams(("parallel", "parallel", "arbitrary")),
        name=name,
    )(x, w, h)


def _rope_table_kernel(pos_ref, freq_ref, sign_ref, cos_ref, sin_ref):
    ang = pos_ref[...] * freq_ref[...]
    cos_ref[...] = jnp.cos(ang)
    sin_ref[...] = jnp.sin(ang) * sign_ref[...]


def _rope_tables(positions, ts=512):
    n = positions.size
    half = HEAD_DIM // 2
    inv_freq = ROPE_THETA ** (-jnp.arange(half, dtype=jnp.float32) / half)
    freq = jnp.concatenate([inv_freq, inv_freq]).reshape(1, HEAD_DIM)
    sign = jnp.concatenate([-jnp.ones((half,), jnp.float32), jnp.ones((half,), jnp.float32)]).reshape(1, HEAD_DIM)
    pos = positions.astype(jnp.float32).reshape(n, 1)
    row = pl.BlockSpec((1, HEAD_DIM), lambda i: (0, 0))
    return pl.pallas_call(
        _rope_table_kernel,
        out_shape=[jax.ShapeDtypeStruct((n, HEAD_DIM), jnp.float32)] * 2,
        grid=(n // ts,),
        in_specs=[pl.BlockSpec((ts, 1), lambda i: (i, 0)), row, row],
        out_specs=[pl.BlockSpec((ts, HEAD_DIM), lambda i: (i, 0))] * 2,
        compiler_params=_cparams(("parallel",)),
        name="rope_tables",
    )(pos, freq, sign)


_FLAG_ROPED, _FLAG_QUERY = 1, 2


def _in_proj_kernel(flags_ref, x_ref, w_ref, cos_ref, sin_ref, o_ref):
    j = pl.program_id(1)
    heads = o_ref.shape[1]
    acc = _dot(x_ref[...], w_ref[...].astype(jnp.bfloat16))
    for hh in range(heads):
        t = acc[:, hh * HEAD_DIM:(hh + 1) * HEAD_DIM]
        flags = flags_ref[j * heads + hh]
        roped = (flags & _FLAG_ROPED) > 0
        scale = jnp.where((flags & _FLAG_QUERY) > 0, Q_SCALE, 1.0)
        cos = jnp.where(roped, cos_ref[...], 1.0)
        sin = jnp.where(roped, sin_ref[...], 0.0)
        o_ref[0, hh] = ((t * cos + pltpu.roll(t, HEAD_DIM // 2, 1) * sin) * scale).astype(o_ref.dtype)


def _in_proj_heads(xn, w_big, cos, sin, batch, tm=1024, tn=512):
    m, d = xn.shape
    s = m // batch
    heads = tn // HEAD_DIM
    q_tiles = s // tm
    flags = np.zeros((N_HEADS_ALL,), np.int32)
    flags[list(_ROPED_HEADS)] |= _FLAG_ROPED
    flags[list(_QUERY_HEADS)] |= _FLAG_QUERY
    table = pl.BlockSpec((tm, HEAD_DIM), lambda i, j, f: (i, 0))
    return pl.pallas_call(
        _in_proj_kernel,
        out_shape=jax.ShapeDtypeStruct((batch, N_HEADS_ALL, s, HEAD_DIM), jnp.bfloat16),
        grid_spec=pltpu.PrefetchScalarGridSpec(
            num_scalar_prefetch=1,
            grid=(m // tm, N_BIG // tn),
            in_specs=[pl.BlockSpec((tm, d), lambda i, j, f: (i, 0)),
                      pl.BlockSpec((d, tn), lambda i, j, f: (0, j)),
                      table, table],
            out_specs=pl.BlockSpec((1, heads, tm, HEAD_DIM), lambda i, j, f: (i // q_tiles, j, i % q_tiles, 0)),
        ),
        compiler_params=_cparams(("parallel", "arbitrary")),
        name="in_proj",
    )(jnp.asarray(flags), xn, w_big, cos, sin)


def _softplus(z):
    return jnp.maximum(z, 0.0) + jnp.log1p(jnp.exp(-jnp.abs(z)))


def _logf_cumsum_kernel(x_ref, bias_ref, o_ref, *, blk):
    s = x_ref.shape[1]
    r = lax.broadcasted_iota(jnp.int32, (blk, blk), 0)
    c = lax.broadcasted_iota(jnp.int32, (blk, blk), 1)
    tri = jnp.where(c <= r, 1.0, 0.0).astype(jnp.float32)

    def body(i, carry):
        x = x_ref[0, pl.ds(i * blk, blk), :]
        logf = -_softplus(-(x + bias_ref[...]))
        cs = jnp.dot(tri, logf, preferred_element_type=jnp.float32,
                     precision=lax.Precision.HIGHEST) + carry
        o_ref[0, pl.ds(i * blk, blk), :] = cs
        return cs[blk - 1:blk, :]

    lax.fori_loop(0, s // blk, body, jnp.zeros((1, x_ref.shape[2]), jnp.float32))


def _logf_cumsum(small, bias_row, blk=128):
    b, s, w = small.shape
    return pl.pallas_call(
        functools.partial(_logf_cumsum_kernel, blk=blk),
        out_shape=jax.ShapeDtypeStruct((b, s, w), jnp.float32),
        grid=(b,),
        in_specs=[pl.BlockSpec((1, s, w), lambda bi: (bi, 0, 0)),
                  pl.BlockSpec((1, w), lambda bi: (0, 0))],
        out_specs=pl.BlockSpec((1, s, w), lambda bi: (bi, 0, 0)),
        compiler_params=_cparams(("parallel",)),
        name="logf_cumsum",
    )(small, bias_row)


def _causal_flash_kernel(*refs, mode, hb, t):
    it = iter(refs)
    q_ref, k_ref, v_ref = next(it), next(it), next(it)
    bias_ref = next(it) if mode == "fox" else None
    sel_ref, onehot_ref = (next(it), next(it)) if mode == "sel" else (None, None)
    o_ref = next(it)
    m_scr, l_scr, acc_scr = next(it), next(it), next(it)
    i = pl.program_id(2)

    m_scr[...] = jnp.full(m_scr.shape, M_INIT, jnp.float32)
    l_scr[...] = jnp.zeros(l_scr.shape, jnp.float32)
    acc_scr[...] = jnp.zeros(acc_scr.shape, jnp.float32)

    def tile(j, diagonal):
        keys = pl.ds(pl.multiple_of(j * t, t), t)
        if diagonal:
            causal = (lax.broadcasted_iota(jnp.int32, (t, t), 1) <= lax.broadcasted_iota(jnp.int32, (t, t), 0))
        if mode == "sel":
            k_shared = jnp.concatenate([k_ref[0, 0, keys, :], onehot_ref[keys, :]], axis=1)
            v_shared = v_ref[0, 0, keys, :]
        for hh in range(hb):
            if mode == "sel":
                s = _dot_nt(jnp.concatenate([q_ref[0, hh], sel_ref[0, 0]], axis=1), k_shared)
                v = v_shared
            else:
                s = _dot_nt(q_ref[0, hh], k_ref[0, hh, keys, :]) + bias_ref[0, hh, j]
                v = v_ref[0, hh, keys, :]
            if diagonal:
                s = jnp.where(causal, s, NEG)
            m_prev = m_scr[hh]
            m_new = jnp.maximum(m_prev, jnp.max(s, axis=-1, keepdims=True))
            alpha = jnp.exp2(m_prev - m_new)
            p = jnp.exp2(s - m_new)
            l_scr[hh] = alpha * l_scr[hh] + jnp.sum(p, axis=-1, keepdims=True)
            acc_scr[hh] = alpha * acc_scr[hh] + _dot(p.astype(v.dtype), v)
            m_scr[hh] = m_new

    def off_diagonal(j, carry):
        tile(j, False)
        return carry

    lax.fori_loop(0, i, off_diagonal, 0)
    tile(i, True)
    for hh in range(hb):
        o_ref[0, :, hh * HEAD_DIM:(hh + 1) * HEAD_DIM] = (acc_scr[hh] / l_scr[hh]).astype(o_ref.dtype)


def _causal_flash(hm, *, mode, q_head0, k_head0, v_head0, hb, t, out_dtype, bias=None, sel=None, onehot=None,
                  name="causal_flash"):
    b, _, s, d = hm.shape
    groups = H_MIX // hb
    kv_heads = hb if mode == "fox" else 1
    in_specs = [pl.BlockSpec((1, hb, t, d), lambda bi, g, i: (bi, q_head0 // hb + g, i, 0)),
                pl.BlockSpec((1, kv_heads, s, d), lambda bi, g, i: (bi, k_head0 // kv_heads + g, 0, 0)),
                pl.BlockSpec((1, kv_heads, s, d), lambda bi, g, i: (bi, v_head0 // kv_heads + g, 0, 0))]
    args = [hm, hm, hm]
    if mode == "fox":
        in_specs.append(pl.BlockSpec((1, hb, s // t, 1, t), lambda bi, g, i: (bi, g, 0, 0, 0)))
        args.append(bias)
    else:
        in_specs.append(pl.BlockSpec((1, 1, t, sel.shape[-1]), lambda bi, g, i: (bi, g, i, 0)))
        in_specs.append(pl.BlockSpec(onehot.shape, lambda bi, g, i: (0, 0)))
        args += [sel, onehot]
    return pl.pallas_call(
        functools.partial(_causal_flash_kernel, mode=mode, hb=hb, t=t),
        out_shape=jax.ShapeDtypeStruct((b, s, H_MIX * d), out_dtype),
        grid=(b, groups, s // t),
        in_specs=in_specs,
        out_specs=pl.BlockSpec((1, t, hb * d), lambda bi, g, i: (bi, i, g)),
        scratch_shapes=[pltpu.VMEM((hb, t, 1), jnp.float32), pltpu.VMEM((hb, t, 1), jnp.float32),
                        pltpu.VMEM((hb, t, d), jnp.float32)],
        compiler_params=_cparams(("parallel", "parallel", "arbitrary")),
        name=name,
    )(*args)


def _band_kernel(*refs, hb, kv_heads, t, wpad, window, seg_len, has_lse):
    q_ref, k_ref, v_ref, o_ref = refs[:4]
    lse_ref = refs[4] if has_lse else None
    s_len = k_ref.shape[2]
    span = t + wpad
    t0 = pl.program_id(2) * t
    seg0 = (t0 // seg_len) * seg_len
    start = jnp.minimum(jnp.maximum(t0 - wpad, seg0), s_len - span)
    start = pl.multiple_of(start, HEAD_DIM)
    keys = pl.ds(start, span)
    qpos = t0 + lax.broadcasted_iota(jnp.int32, (t, span), 0)
    kpos = start + lax.broadcasted_iota(jnp.int32, (t, span), 1)
    dist = qpos - kpos
    mask = (dist >= 0) & (dist <= window) & (kpos >= seg0)
    for hh in range(hb):
        hk = hh * kv_heads // hb
        s = jnp.where(mask, _dot_nt(q_ref[0, hh], k_ref[0, hk, keys, :]), NEG)
        m = jnp.max(s, axis=-1, keepdims=True)
        p = jnp.exp2(s - m)
        l = jnp.sum(p, axis=-1, keepdims=True)
        v = v_ref[0, hk, keys, :]
        out = _dot(p.astype(v.dtype), v) / l
        if has_lse:
            o_ref[0, hh] = out.astype(o_ref.dtype)
            lse_ref[0, hh] = jnp.broadcast_to(m + jnp.log2(l), (t, HEAD_DIM))
        else:
            o_ref[0, :, hh * HEAD_DIM:(hh + 1) * HEAD_DIM] = out.astype(o_ref.dtype)


def _band(src, *, q_head0, k_head0, v_head0, hb, kv_heads, t, wpad, window, seg_len, has_lse, name):
    b, _, s, d = src.shape
    groups = H_MIX // hb
    in_specs = [pl.BlockSpec((1, hb, t, d), lambda bi, g, i: (bi, q_head0 // hb + g, i, 0)),
                pl.BlockSpec((1, kv_heads, s, d), lambda bi, g, i: (bi, k_head0 // kv_heads + g, 0, 0)),
                pl.BlockSpec((1, kv_heads, s, d), lambda bi, g, i: (bi, v_head0 // kv_heads + g, 0, 0))]
    if has_lse:
        hm_spec = pl.BlockSpec((1, hb, t, d), lambda bi, g, i: (bi, g, i, 0))
        out_shape = [jax.ShapeDtypeStruct((b, H_MIX, s, d), jnp.float32)] * 2
        out_specs = [hm_spec, hm_spec]
    else:
        out_shape = jax.ShapeDtypeStruct((b, s, H_MIX * d), jnp.float32)
        out_specs = pl.BlockSpec((1, t, hb * d), lambda bi, g, i: (bi, i, g))
    return pl.pallas_call(
        functools.partial(_band_kernel, hb=hb, kv_heads=kv_heads, t=t, wpad=wpad, window=window,
                          seg_len=seg_len, has_lse=has_lse),
        out_shape=out_shape,
        grid=(b, groups, s // t),
        in_specs=in_specs,
        out_specs=out_specs,
        compiler_params=_cparams(("parallel", "parallel", "arbitrary")),
        name=name,
    )(src, src, src)


def _stickbreak_kernel(q_ref, k_ref, v_ref, u_ref, o_ref, run_scr, acc_scr, *, hb, t):
    i = pl.program_id(2)
    run_scr[...] = jnp.zeros(run_scr.shape, jnp.float32)
    acc_scr[...] = jnp.zeros(acc_scr.shape, jnp.float32)
    u = u_ref[...]

    def tile(j, diagonal):
        keys = pl.ds(pl.multiple_of(j * t, t), t)
        if diagonal:
            strict = (lax.broadcasted_iota(jnp.int32, (t, t), 1) < lax.broadcasted_iota(jnp.int32, (t, t), 0))
        for hh in range(hb):
            z2 = _dot_nt(q_ref[0, hh], k_ref[0, hh, keys, :])
            lg = jnp.log2(1.0 + jnp.exp2(-jnp.abs(z2)))
            sp2 = jnp.maximum(z2, 0.0) + lg
            if diagonal:
                sp2 = jnp.where(strict, sp2, 0.0)
            hi = sp2.astype(jnp.bfloat16)
            lo = (sp2 - hi.astype(jnp.float32)).astype(jnp.bfloat16)
            later = _dot(hi, u) + _dot(lo, u)
            a = jnp.exp2((jnp.minimum(z2, 0.0) - lg) - (later + run_scr[hh]))
            if diagonal:
                a = jnp.where(strict, a, 0.0)
            v = v_ref[0, hh, keys, :]
            acc_scr[hh] += _dot(a.astype(v.dtype), v)
            run_scr[hh] += later[:, 0:1] + sp2[:, 0:1]

    def off_diagonal(n, carry):
        tile(i - 1 - n, False)
        return carry

    tile(i, True)
    lax.fori_loop(0, i, off_diagonal, 0)
    for hh in range(hb):
        o_ref[0, :, hh * HEAD_DIM:(hh + 1) * HEAD_DIM] = acc_scr[hh].astype(o_ref.dtype)


def _stickbreak(hm, *, q_head0, k_head0, v_head0, hb, t, out_dtype):
    b, _, s, d = hm.shape
    later = np.arange(t)[:, None] > np.arange(t)[None, :]
    u = jnp.asarray(later, jnp.bfloat16)

    def heads(h0):
        return pl.BlockSpec((1, hb, s, d), lambda bi, g, i: (bi, h0 // hb + g, 0, 0))

    return pl.pallas_call(
        functools.partial(_stickbreak_kernel, hb=hb, t=t),
        out_shape=jax.ShapeDtypeStruct((b, s, H_MIX * d), out_dtype),
        grid=(b, H_MIX // hb, s // t),
        in_specs=[pl.BlockSpec((1, hb, t, d), lambda bi, g, i: (bi, q_head0 // hb + g, i, 0)),
                  heads(k_head0), heads(v_head0),
                  pl.BlockSpec((t, t), lambda bi, g, i: (0, 0))],
        out_specs=pl.BlockSpec((1, t, hb * d), lambda bi, g, i: (bi, i, g)),
        scratch_shapes=[pltpu.VMEM((hb, t, 1), jnp.float32), pltpu.VMEM((hb, t, d), jnp.float32)],
        compiler_params=_cparams(("parallel", "parallel", "arbitrary")),
        name="stickbreak",
    )(hm, hm, hm, u)


def _gelu_tanh(x):
    return 0.5 * x * (1.0 + jnp.tanh(math.sqrt(2.0 / math.pi) * (x + 0.044715 * (x * x * x))))


def _compress_kernel(x_ref, pe_ref, w1_ref, w2_ref, o_ref):
    half = x_ref.shape[3]
    x = x_ref[0, 0]
    w1 = w1_ref[0].astype(jnp.bfloat16)
    first = _dot(x, w1[:half])
    second = _dot(x, w1[half:])
    pe = jnp.broadcast_to(pe_ref[0], (8, pe_ref.shape[2])).astype(jnp.bfloat16)
    pe_term = _dot(pe, w1)[0:1]
    n_chunks = x.shape[0]
    hid = first + pltpu.roll(second, n_chunks - 1, 0) + pe_term
    act = _gelu_tanh(hid)
    o_ref[0, 0] = _dot(act.astype(jnp.bfloat16), w2_ref[0].astype(jnp.bfloat16))


def _compress(chunks, pe, w1, w2):
    b, nh, nc, half = chunks.shape
    return pl.pallas_call(
        _compress_kernel,
        out_shape=jax.ShapeDtypeStruct((b, nh, nc, HEAD_DIM), jnp.float32),
        grid=(b, nh),
        in_specs=[pl.BlockSpec((1, 1, nc, half), lambda bi, h: (bi, h, 0, 0)),
                  pl.BlockSpec((1, 1, 2 * half), lambda bi, h: (h // NSA_KV, 0, 0)),
                  pl.BlockSpec((1, 2 * half, w1.shape[2]), lambda bi, h: (h // NSA_KV, 0, 0)),
                  pl.BlockSpec((1, w2.shape[1], HEAD_DIM), lambda bi, h: (h // NSA_KV, 0, 0))],
        out_specs=pl.BlockSpec((1, 1, nc, HEAD_DIM), lambda bi, h: (bi, h, 0, 0)),
        compiler_params=_cparams(("parallel", "parallel")),
        name="nsa_compress",
    )(chunks, pe, w1, w2)


def _cmp_select_kernel(q_ref, kc_ref, vc_ref, c2s_ref, o_ref, sel_ref, *, tq, n_cmp):
    i = pl.program_id(2)
    nc = kc_ref.shape[2]
    kc = kc_ref[0, 0].astype(jnp.bfloat16)
    vc = vc_ref[0, 0].astype(jnp.bfloat16)
    t = i * tq + lax.broadcasted_iota(jnp.int32, (tq, 1), 0)
    cidx = lax.broadcasted_iota(jnp.int32, (1, nc), 1)
    visible = (cidx * CMP_STRIDE + (CMP_LEN - 1) <= t) & (cidx < n_cmp)
    psum = jnp.zeros((tq, nc), jnp.float32)
    for hh in range(NSA_HPG):
        s = jnp.where(visible, _dot_nt(q_ref[0, hh], kc), NEG)
        m = jnp.max(s, axis=-1, keepdims=True)
        e = jnp.where(visible, jnp.exp2(s - m), 0.0)
        l = jnp.sum(e, axis=-1, keepdims=True)
        p = e / jnp.where(l > 0.0, l, 1.0)
        psum = psum + p
        o_ref[0, :, hh * HEAD_DIM:(hh + 1) * HEAD_DIM] = _dot(p.astype(jnp.bfloat16), vc)

    imp = jnp.dot(psum, c2s_ref[...], preferred_element_type=jnp.float32, precision=lax.Precision.HIGHEST)
    w = imp.shape[1]
    n_slc = w // 2
    jblk = lax.broadcasted_iota(jnp.int32, (1, w), 1)
    cur = lax.shift_right_logical(t, int(math.log2(SLC_LEN)))
    forced = (jblk == 0) | (jblk == cur) | (jblk == cur - 1)
    causal_blk = jblk * SLC_LEN <= t
    score = jnp.where(causal_blk, jnp.where(forced, FORCED_SCORE, imp), -1.0)
    rank = jnp.zeros((tq, w), jnp.float32)
    for jp in range(n_slc):
        col = score[:, jp:jp + 1]
        before = (col > score) | ((col == score) & (jblk > jp))
        rank = rank + jnp.where(before, 1.0, 0.0)
    chosen = (rank < float(min(TOPK, n_slc))) & causal_blk
    sel_ref[0, 0] = jnp.where(chosen | (jblk >= n_slc), 0.0, NEG).astype(sel_ref.dtype)


def _cmp_select(hm, cmp_kv, c2s, *, tq, n_cmp):
    b, _, s, d = hm.shape
    nc = cmp_kv.shape[2]
    w = c2s.shape[1]
    return pl.pallas_call(
        functools.partial(_cmp_select_kernel, tq=tq, n_cmp=n_cmp),
        out_shape=[jax.ShapeDtypeStruct((b, s, H_MIX * d), jnp.float32),
                   jax.ShapeDtypeStruct((b, NSA_KV, s, w), jnp.bfloat16)],
        grid=(b, NSA_KV, s // tq),
        in_specs=[pl.BlockSpec((1, NSA_HPG, tq, d), lambda bi, g, i: (bi, g, i, 0)),
                  pl.BlockSpec((1, 1, nc, d), lambda bi, g, i: (bi, g, 0, 0)),
                  pl.BlockSpec((1, 1, nc, d), lambda bi, g, i: (bi, NSA_KV + g, 0, 0)),
                  pl.BlockSpec((nc, w), lambda bi, g, i: (0, 0))],
        out_specs=[pl.BlockSpec((1, tq, NSA_HPG * d), lambda bi, g, i: (bi, i, g)),
                   pl.BlockSpec((1, 1, tq, w), lambda bi, g, i: (bi, g, i, 0))],
        compiler_params=_cparams(("parallel", "parallel", "parallel")),
        name="nsa_cmp_select",
    )(hm, cmp_kv, cmp_kv, c2s)


def _nsa_combine_kernel(oc_ref, os_ref, ow_ref, g_ref, o_ref):
    gates = jax.nn.sigmoid(g_ref[0])
    for h in range(H_MIX):
        cols = slice(h * HEAD_DIM, (h + 1) * HEAD_DIM)
        c0 = GATE_COL0 + 3 * h
        out = (gates[:, c0:c0 + 1] * oc_ref[0, :, cols]
               + gates[:, c0 + 1:c0 + 2] * os_ref[0, :, cols]
               + gates[:, c0 + 2:c0 + 3] * ow_ref[0, :, cols])
        o_ref[0, :, cols] = out.astype(o_ref.dtype)


def _nsa_combine(o_cmp, o_slc, o_win, small, tq=512):
    b, s, w = o_cmp.shape
    big = pl.BlockSpec((1, tq, w), lambda bi, i: (bi, i, 0))
    return pl.pallas_call(
        _nsa_combine_kernel,
        out_shape=jax.ShapeDtypeStruct((b, s, w), jnp.bfloat16),
        grid=(b, s // tq),
        in_specs=[big, big, big, pl.BlockSpec((1, tq, small.shape[2]), lambda bi, i: (bi, i, 0))],
        out_specs=big,
        compiler_params=_cparams(("parallel", "parallel")),
        name="nsa_combine",
    )(o_cmp, o_slc, o_win, small)


def _dil_combine_kernel(o1, o2, o3, l1, l2, l3, o_ref):
    a, bb, c = l1[0, 0], l2[0, 0], l3[0, 0]
    m = jnp.maximum(jnp.maximum(a, bb), c)
    ea, eb, ec = jnp.exp2(a - m), jnp.exp2(bb - m), jnp.exp2(c - m)
    tot = ea + eb + ec
    out = o1[0, 0] * (ea / tot) + o2[0, 0] * (eb / tot) + o3[0, 0] * (ec / tot)
    o_ref[0] = out.astype(o_ref.dtype)


def _dil_combine(outs, lses, tq=1024):
    b, h, s, d = outs[0].shape
    spec = pl.BlockSpec((1, 1, tq, d), lambda bi, hh, i: (bi, hh, i, 0))
    return pl.pallas_call(
        _dil_combine_kernel,
        out_shape=jax.ShapeDtypeStruct((b, s, h * d), jnp.bfloat16),
        grid=(b, h, s // tq),
        in_specs=[spec] * 6,
        out_specs=pl.BlockSpec((1, tq, d), lambda bi, hh, i: (bi, i, hh)),
        compiler_params=_cparams(("parallel", "parallel", "parallel")),
        name="dil_combine",
    )(*outs, *lses)


def _to_residue_major(t, dil):
    b, h, s, d = t.shape
    return t.reshape(b, h, s // dil, dil, d).transpose(0, 1, 3, 2, 4).reshape(b, h, s, d)


def _from_residue_major(t, dil):
    b, h, s, d = t.shape
    return t.reshape(b, h, dil, s // dil, d).transpose(0, 1, 3, 2, 4).reshape(b, h, s, d)


def _cmp_to_slc_matrix(n_chunks, n_cmp, n_slc):
    ratio = SLC_LEN // CMP_STRIDE
    span = CMP_LEN // CMP_STRIDE
    jj, mm, nn = np.meshgrid(np.arange(n_slc), np.arange(ratio), np.arange(span), indexing="ij")
    cc = ratio * jj + mm + nn
    keep = cc < n_cmp
    mat = np.zeros((n_chunks, 2 * n_slc), np.float32)
    np.add.at(mat, (cc[keep], jj[keep]), 1.0)
    return jnp.asarray(mat)


def _mixers(hm, small, fox_bf, cmp_pe_k, cmp_w1_k, cmp_w2_k, cmp_pe_v, cmp_w1_v, cmp_w2_v):
    b, _, s, d = hm.shape
    bf16 = jnp.bfloat16

    n_chunks = s // CMP_STRIDE
    n_cmp = (s - CMP_LEN) // CMP_STRIDE + 1
    n_slc = s // SLC_LEN
    chunks = hm[:, HD_KC:HD_KC + 2 * NSA_KV].reshape(b, 2 * NSA_KV, n_chunks, CMP_STRIDE * d)
    cmp_kv = _compress(chunks,
                       jnp.stack([cmp_pe_k, cmp_pe_v]).reshape(2, 1, CMP_LEN * d),
                       jnp.stack([cmp_w1_k, cmp_w1_v]), jnp.stack([cmp_w2_k, cmp_w2_v]))
    o_cmp, sel = _cmp_select(hm, cmp_kv, _cmp_to_slc_matrix(n_chunks, n_cmp, n_slc), tq=256, n_cmp=n_cmp)
    onehot = jnp.asarray((np.arange(s)[:, None] // SLC_LEN) == np.arange(2 * n_slc)[None, :], bf16)
    o_slc = _causal_flash(hm, mode="sel", q_head0=HD_QA, k_head0=HD_KS, v_head0=HD_VS, hb=NSA_HPG, t=512,
                          sel=sel, onehot=onehot, out_dtype=jnp.float32, name="nsa_selected")
    o_win = _band(hm, q_head0=HD_QA, k_head0=HD_KW, v_head0=HD_VW, hb=NSA_HPG, kv_heads=1, t=512, wpad=NSA_WINDOW,
                  window=NSA_WINDOW - 1, seg_len=s, has_lse=False, name="nsa_window")
    o_a = _nsa_combine(o_cmp, o_slc, o_win, small)

    t_fox = 512
    bias_row = jnp.zeros((1, small.shape[2]), jnp.float32).at[0, FORGET_COL0:FORGET_COL0 + H_MIX].set(fox_bf)
    csum = _logf_cumsum(small, bias_row)
    key_bias = (-LOG2E) * csum[:, :, FORGET_COL0:FORGET_COL0 + H_MIX].transpose(0, 2, 1)
    o_b = _causal_flash(hm, mode="fox", q_head0=HD_QB, k_head0=HD_KB, v_head0=HD_VB, hb=4, t=t_fox,
                        bias=key_bias.reshape(b, H_MIX, s // t_fox, 1, t_fox), out_dtype=bf16, name="fox")

    outs, lses = [], []
    for window, dil in DIL_CONFIGS:
        if dil == 1:
            src, heads0 = hm, (HD_QC, HD_KC2, HD_VC2)
        else:
            src, heads0 = _to_residue_major(hm[:, HD_QC:HD_QC + 3 * H_MIX], dil), (0, H_MIX, 2 * H_MIX)
        o, lse = _band(src, q_head0=heads0[0], k_head0=heads0[1], v_head0=heads0[2], hb=4, kv_heads=4, t=256,
                       wpad=HEAD_DIM, window=window // dil, seg_len=s // dil, has_lse=True, name=f"dilated_{dil}")
        if dil != 1:
            o, lse = _from_residue_major(o, dil), _from_residue_major(lse, dil)
        outs.append(o)
        lses.append(lse)
    o_c = _dil_combine(outs, lses)

    o_d = _stickbreak(hm, q_head0=HD_QD, k_head0=HD_KD, v_head0=HD_VD, hb=4, t=256, out_dtype=bf16)

    return jnp.concatenate([o_a, o_b, o_c, o_d], axis=-1)


def _split_w_in(w_in):
    a = H_MIX * HEAD_DIM + 6 * NSA_KV * HEAD_DIM
    g_end = a + 3 * H_MIX
    f0 = g_end + 3 * H_MIX * HEAD_DIM
    f_end = f0 + H_MIX
    big = jnp.concatenate([w_in[:, :a], w_in[:, g_end:f0], w_in[:, f_end:]], axis=1)
    small = jnp.concatenate([w_in[:, a:g_end], w_in[:, f0:f_end],
                             jnp.zeros((w_in.shape[0], HEAD_DIM - 4 * H_MIX), w_in.dtype)], axis=1)
    return big, small


def _layer(h, p_i, cos, sin, batch, norm_attn, w_in, fox_bf, cmp_pe_k, cmp_w1_k, cmp_w2_k,
           cmp_pe_v, cmp_w1_v, cmp_w2_v, w_o, norm_mlp, w_up, w_down, norm_ple, w_ple_gate, w_ple_proj):
    m, d = h.shape
    s = m // batch
    bf16 = jnp.bfloat16
    w_big, w_small = _split_w_in(w_in)
    xn = _rmsnorm(h, norm_attn, bf16)
    hm = _in_proj_heads(xn, w_big, cos, sin, batch)
    small = _matmul(xn, w_small, jnp.float32, name="in_proj_small")
    mix = _mixers(hm, small.reshape(batch, s, HEAD_DIM), fox_bf,
                  cmp_pe_k, cmp_w1_k, cmp_w2_k, cmp_pe_v, cmp_w1_v, cmp_w2_v)

    tm, tn = 1024, 512
    res_spec = pl.BlockSpec((tm, tn), lambda i, j: (i, j))
    h = _matmul(mix.reshape(m, -1), w_o, jnp.float32, _epi_residual, (h,), (res_spec,), tm=tm, tn=tn, name="out_proj")

    x2 = _rmsnorm(h, norm_mlp, bf16)
    mid = _matmul(x2, w_up, bf16, _epi_relu2, tm=tm, tn=tn, name="mlp_up")
    h = _matmul_residual_ksplit(mid, w_down, h, name="mlp_down")

    x3 = _rmsnorm(h, norm_ple, bf16)
    ple_dim = p_i.shape[1]
    h = _matmul(x3, w_ple_gate, jnp.float32, _epi_ple, (h, p_i, w_ple_proj),
                (res_spec, pl.BlockSpec((tm, ple_dim), lambda i, j: (i, 0)),
                 pl.BlockSpec((ple_dim, tn), lambda i, j: (0, j))),
                tm=tm, tn=tn, name="ple_gate")
    return h


def kernel(x, p, positions, norm_attn, w_in, fox_bf, cmp_pe_k, cmp_w1_k, cmp_w2_k, cmp_pe_v, cmp_w1_v, cmp_w2_v,
           w_o, norm_mlp, w_up, w_down, norm_ple, w_ple_gate, w_ple_proj, norm_final):
    batch, s, d = x.shape
    depth = p.shape[0]
    cos, sin = _rope_tables(positions)
    h = x.reshape(batch * s, d)
    for i in range(depth):
        h = _layer(h, p[i].reshape(batch * s, -1), cos, sin, batch, norm_attn[i], w_in[i], fox_bf[i],
                   cmp_pe_k[i], cmp_w1_k[i], cmp_w2_k[i], cmp_pe_v[i], cmp_w1_v[i], cmp_w2_v[i],
                   w_o[i], norm_mlp[i], w_up[i], w_down[i], norm_ple[i], w_ple_gate[i], w_ple_proj[i])
    return _rmsnorm(h, norm_final, x.dtype).reshape(batch, s, d)
```

```python
import functools
import math

import numpy as np
import jax
import jax.numpy as jnp
from jax import lax
from jax.experimental import pallas as pl
from jax.experimental.pallas import tpu as pltpu

HEAD_DIM = 128
H_MIX = 8
NSA_KV = 2
NSA_HPG = H_MIX // NSA_KV
CMP_LEN = 32
CMP_STRIDE = 16
SLC_LEN = 64
TOPK = 16
NSA_WINDOW = 512
DIL_CONFIGS = ((128, 1), (512, 4), (2048, 16))
ROPE_THETA = 10000.0
RMS_EPS = 1e-6
NEG = -1e30
M_INIT = -5e29
FORCED_SCORE = 1e9
LOG2E = math.log2(math.e)
Q_SCALE = HEAD_DIM ** -0.5 * LOG2E

VMEM_LIMIT_BYTES = 52 * 1024 * 1024

HD_QA, HD_KC, HD_VC, HD_KS, HD_VS, HD_KW, HD_VW = 0, 8, 10, 12, 14, 16, 18
HD_QB, HD_KB, HD_VB = 20, 28, 36
HD_QC, HD_KC2, HD_VC2 = 44, 52, 60
HD_QD, HD_KD, HD_VD = 68, 76, 84
N_HEADS_ALL = 92
N_BIG = N_HEADS_ALL * HEAD_DIM
GATE_COL0 = 0
FORGET_COL0 = 24
_ROPED_HEADS = (tuple(range(HD_QA, HD_QA + H_MIX)) + tuple(range(HD_KS, HD_KS + NSA_KV))
                + tuple(range(HD_KW, HD_KW + NSA_KV)) + tuple(range(HD_QC, HD_QC + 2 * H_MIX)))
_QUERY_HEADS = tuple(h for q0 in (HD_QA, HD_QB, HD_QC, HD_QD) for h in range(q0, q0 + H_MIX))


def _cparams(sem):
    return pltpu.CompilerParams(dimension_semantics=sem, vmem_limit_bytes=VMEM_LIMIT_BYTES)


def _dot_nt(a, b):
    return lax.dot_general(a, b, (((1,), (1,)), ((), ())), preferred_element_type=jnp.float32)


def _dot(a, b):
    return jnp.dot(a, b, preferred_element_type=jnp.float32)


def _rmsnorm_kernel(x_ref, g_ref, o_ref):
    x = x_ref[...]
    ms = jnp.mean(x * x, axis=-1, keepdims=True)
    o_ref[...] = (x * lax.rsqrt(ms + RMS_EPS) * g_ref[...]).astype(o_ref.dtype)


def _rmsnorm(x, g, out_dtype, tm=256):
    m, d = x.shape
    return pl.pallas_call(
        _rmsnorm_kernel,
        out_shape=jax.ShapeDtypeStruct((m, d), out_dtype),
        grid=(m // tm,),
        in_specs=[pl.BlockSpec((tm, d), lambda i: (i, 0)),
                  pl.BlockSpec((1, d), lambda i: (0, 0))],
        out_specs=pl.BlockSpec((tm, d), lambda i: (i, 0)),
        compiler_params=_cparams(("parallel",)),
        name="rmsnorm",
    )(x, g.reshape(1, d))


def _mm_kernel(*refs, n_extra, epilogue):
    x_ref, w_ref = refs[0], refs[1]
    extras = refs[2:2 + n_extra]
    o_ref = refs[2 + n_extra]
    acc = _dot(x_ref[...].astype(jnp.bfloat16), w_ref[...].astype(jnp.bfloat16))
    o_ref[...] = epilogue(acc, *extras).astype(o_ref.dtype)


def _epi_none(acc):
    return acc


def _epi_relu2(acc):
    r = jnp.maximum(acc, 0.0)
    return r * r


def _epi_residual(acc, h_ref):
    return h_ref[...] + acc


def _epi_ple(acc, h_ref, p_ref, wp_ref):
    pp = _dot(p_ref[...].astype(jnp.bfloat16), wp_ref[...].astype(jnp.bfloat16))
    return h_ref[...] + jax.nn.sigmoid(acc) * pp


def _matmul(x, w, out_dtype, epilogue=_epi_none, extras=(), extra_specs=(), tm=1024, tn=512, name="matmul"):
    m, kdim = x.shape
    n = w.shape[1]
    tn = min(tn, n)
    return pl.pallas_call(
        functools.partial(_mm_kernel, n_extra=len(extras), epilogue=epilogue),
        out_shape=jax.ShapeDtypeStruct((m, n), out_dtype),
        grid=(m // tm, n // tn),
        in_specs=[pl.BlockSpec((tm, kdim), lambda i, j: (i, 0)),
                  pl.BlockSpec((kdim, tn), lambda i, j: (0, j)),
                  *extra_specs],
        out_specs=pl.BlockSpec((tm, tn), lambda i, j: (i, j)),
        compiler_params=_cparams(("parallel", "arbitrary")),
        name=name,
    )(x, w, *extras)


def _mm_residual_ksplit_kernel(x_ref, w_ref, h_ref, o_ref):
    @pl.when(pl.program_id(2) == 0)
    def _():
        o_ref[...] = h_ref[...]

    o_ref[...] += _dot(x_ref[...], w_ref[...].astype(jnp.bfloat16))


def _matmul_residual_ksplit(x, w, h, tm=2048, tn=1024, tk=1024, name="matmul_ksplit"):
    m, kdim = x.shape
    n = w.shape[1]
    return pl.pallas_call(
        _mm_residual_ksplit_kernel,
        out_shape=jax.ShapeDtypeStruct((m, n), jnp.float32),
        grid=(m // tm, n // tn, kdim // tk),
        in_specs=[pl.BlockSpec((tm, tk), lambda i, j, k: (i, k)),
                  pl.BlockSpec((tk, tn), lambda i, j, k: (k, j)),
                  pl.BlockSpec((tm, tn), lambda i, j, k: (i, j), pipeline_mode=pl.Buffered(1))],
        out_specs=pl.BlockSpec((tm, tn), lambda i, j, k: (i, j)),
        compiler_params=_cparams(("parallel", "parallel", "arbitrary")),
        name=name,
    )(x, w, h)


def _rope_table_kernel(pos_ref, freq_ref, sign_ref, cos_ref, sin_ref):
    ang = pos_ref[...] * freq_ref[...]
    cos_ref[...] = jnp.cos(ang)
    sin_ref[...] = jnp.sin(ang) * sign_ref[...]


def _rope_tables(positions, ts=512):
    n = positions.size
    half = HEAD_DIM // 2
    inv_freq = ROPE_THETA ** (-jnp.arange(half, dtype=jnp.float32) / half)
    freq = jnp.concatenate([inv_freq, inv_freq]).reshape(1, HEAD_DIM)
    sign = jnp.concatenate([-jnp.ones((half,), jnp.float32), jnp.ones((half,), jnp.float32)]).reshape(1, HEAD_DIM)
    pos = positions.astype(jnp.float32).reshape(n, 1)
    row = pl.BlockSpec((1, HEAD_DIM), lambda i: (0, 0))
    return pl.pallas_call(
        _rope_table_kernel,
        out_shape=[jax.ShapeDtypeStruct((n, HEAD_DIM), jnp.float32)] * 2,
        grid=(n // ts,),
        in_specs=[pl.BlockSpec((ts, 1), lambda i: (i, 0)), row, row],
        out_specs=[pl.BlockSpec((ts, HEAD_DIM), lambda i: (i, 0))] * 2,
        compiler_params=_cparams(("parallel",)),
        name="rope_tables",
    )(pos, freq, sign)


_FLAG_ROPED, _FLAG_QUERY = 1, 2


def _in_proj_kernel(flags_ref, x_ref, w_ref, cos_ref, sin_ref, o_ref):
    j = pl.program_id(1)
    heads = o_ref.shape[1]
    acc = _dot(x_ref[...], w_ref[...].astype(jnp.bfloat16))
    for hh in range(heads):
        t = acc[:, hh * HEAD_DIM:(hh + 1) * HEAD_DIM]
        flags = flags_ref[j * heads + hh]
        roped = (flags & _FLAG_ROPED) > 0
        scale = jnp.where((flags & _FLAG_QUERY) > 0, Q_SCALE, 1.0)
        cos = jnp.where(roped, cos_ref[...], 1.0)
        sin = jnp.where(roped, sin_ref[...], 0.0)
        o_ref[0, hh] = ((t * cos + pltpu.roll(t, HEAD_DIM // 2, 1) * sin) * scale).astype(o_ref.dtype)


def _in_proj_heads(xn, w_big, cos, sin, batch, tm=1024, tn=512):
    m, d = xn.shape
    s = m // batch
    heads = tn // HEAD_DIM
    q_tiles = s // tm
    flags = np.zeros((N_HEADS_ALL,), np.int32)
    flags[list(_ROPED_HEADS)] |= _FLAG_ROPED
    flags[list(_QUERY_HEADS)] |= _FLAG_QUERY
    table = pl.BlockSpec((tm, HEAD_DIM), lambda i, j, f: (i, 0))
    return pl.pallas_call(
        _in_proj_kernel,
        out_shape=jax.ShapeDtypeStruct((batch, N_HEADS_ALL, s, HEAD_DIM), jnp.bfloat16),
        grid_spec=pltpu.PrefetchScalarGridSpec(
            num_scalar_prefetch=1,
            grid=(m // tm, N_BIG // tn),
            in_specs=[pl.BlockSpec((tm, d), lambda i, j, f: (i, 0)),
                      pl.BlockSpec((d, tn), lambda i, j, f: (0, j)),
                      table, table],
            out_specs=pl.BlockSpec((1, heads, tm, HEAD_DIM), lambda i, j, f: (i // q_tiles, j, i % q_tiles, 0)),
        ),
        compiler_params=_cparams(("parallel", "arbitrary")),
        name="in_proj",
    )(jnp.asarray(flags), xn, w_big, cos, sin)


def _softplus(z):
    return jnp.maximum(z, 0.0) + jnp.log1p(jnp.exp(-jnp.abs(z)))


def _logf_cumsum_kernel(x_ref, bias_ref, o_ref, *, blk):
    s = x_ref.shape[1]
    r = lax.broadcasted_iota(jnp.int32, (blk, blk), 0)
    c = lax.broadcasted_iota(jnp.int32, (blk, blk), 1)
    tri = jnp.where(c <= r, 1.0, 0.0).astype(jnp.float32)

    def body(i, carry):
        x = x_ref[0, pl.ds(i * blk, blk), :]
        logf = -_softplus(-(x + bias_ref[...]))
        cs = jnp.dot(tri, logf, preferred_element_type=jnp.float32,
                     precision=lax.Precision.HIGHEST) + carry
        o_ref[0, pl.ds(i * blk, blk), :] = cs
        return cs[blk - 1:blk, :]

    lax.fori_loop(0, s // blk, body, jnp.zeros((1, x_ref.shape[2]), jnp.float32))


def _logf_cumsum(small, bias_row, blk=128):
    b, s, w = small.shape
    return pl.pallas_call(
        functools.partial(_logf_cumsum_kernel, blk=blk),
        out_shape=jax.ShapeDtypeStruct((b, s, w), jnp.float32),
        grid=(b,),
        in_specs=[pl.BlockSpec((1, s, w), lambda bi: (bi, 0, 0)),
                  pl.BlockSpec((1, w), lambda bi: (0, 0))],
        out_specs=pl.BlockSpec((1, s, w), lambda bi: (bi, 0, 0)),
        compiler_params=_cparams(("parallel",)),
        name="logf_cumsum",
    )(small, bias_row)


def _rowmax_lanes(s):
    m = s[:, :HEAD_DIM]
    for c in range(1, s.shape[1] // HEAD_DIM):
        m = jnp.maximum(m, s[:, c * HEAD_DIM:(c + 1) * HEAD_DIM])
    return jnp.broadcast_to(jnp.max(m, axis=-1, keepdims=True), m.shape)


def _fold_lanes(p):
    a = p[:, :HEAD_DIM]
    for c in range(1, p.shape[1] // HEAD_DIM):
        a = a + p[:, c * HEAD_DIM:(c + 1) * HEAD_DIM]
    return a


def _tile_lanes(x, width):
    return jnp.concatenate([x] * (width // HEAD_DIM), axis=1)


def _causal_flash_kernel(*refs, mode, hb, t):
    it = iter(refs)
    q_ref, k_ref, v_ref = next(it), next(it), next(it)
    bias_ref = next(it) if mode == "fox" else None
    sel_ref, onehot_ref = (next(it), next(it)) if mode == "sel" else (None, None)
    o_ref = next(it)
    m_scr, l_scr, acc_scr = next(it), next(it), next(it)
    i = pl.program_id(2)

    m_scr[...] = jnp.full(m_scr.shape, M_INIT, jnp.float32)
    l_scr[...] = jnp.zeros(l_scr.shape, jnp.float32)
    acc_scr[...] = jnp.zeros(acc_scr.shape, jnp.float32)

    def tile(j, diagonal):
        keys = pl.ds(pl.multiple_of(j * t, t), t)
        if mode == "sel":
            k_shared = jnp.concatenate([k_ref[0, 0, keys, :], onehot_ref[keys, :]], axis=1)
            scores = [_dot_nt(jnp.concatenate([q_ref[0, hh], sel_ref[0, 0]], axis=1), k_shared)
                      for hh in range(hb)]
        else:
            scores = [_dot_nt(q_ref[0, hh], k_ref[0, hh, keys, :]) + bias_ref[0, hh, j] for hh in range(hb)]
        if diagonal:
            causal = (lax.broadcasted_iota(jnp.int32, (t, t), 1) <= lax.broadcasted_iota(jnp.int32, (t, t), 0))
            scores = [jnp.where(causal, s, NEG) for s in scores]
        probs = []
        for hh, s in enumerate(scores):
            m_prev = m_scr[hh]
            m_new = jnp.maximum(m_prev, _rowmax_lanes(s))
            alpha = jnp.exp2(m_prev - m_new)
            p = jnp.exp2(s - _tile_lanes(m_new, t))
            l_scr[hh] = alpha * l_scr[hh] + _fold_lanes(p)
            m_scr[hh] = m_new
            probs.append((alpha, p.astype(v_ref.dtype)))
        for hh, (alpha, p) in enumerate(probs):
            v = v_ref[0, 0 if mode == "sel" else hh, keys, :]
            acc_scr[hh] = alpha * acc_scr[hh] + _dot(p, v)

    def off_diagonal(j, carry):
        tile(j, False)
        return carry

    lax.fori_loop(0, i, off_diagonal, 0)
    tile(i, True)
    for hh in range(hb):
        l = jnp.sum(l_scr[hh], axis=-1, keepdims=True)
        o_ref[0, :, hh * HEAD_DIM:(hh + 1) * HEAD_DIM] = (acc_scr[hh] / l).astype(o_ref.dtype)


def _causal_flash(hm, *, mode, q_head0, k_head0, v_head0, hb, t, out_dtype, bias=None, sel=None, onehot=None,
                  name="causal_flash"):
    b, _, s, d = hm.shape
    groups = H_MIX // hb
    kv_heads = hb if mode == "fox" else 1
    in_specs = [pl.BlockSpec((1, hb, t, d), lambda bi, g, i: (bi, q_head0 // hb + g, i, 0)),
                pl.BlockSpec((1, kv_heads, s, d), lambda bi, g, i: (bi, k_head0 // kv_heads + g, 0, 0)),
                pl.BlockSpec((1, kv_heads, s, d), lambda bi, g, i: (bi, v_head0 // kv_heads + g, 0, 0))]
    args = [hm, hm, hm]
    if mode == "fox":
        in_specs.append(pl.BlockSpec((1, hb, s // t, 1, t), lambda bi, g, i: (bi, g, 0, 0, 0)))
        args.append(bias)
    else:
        in_specs.append(pl.BlockSpec((1, 1, t, sel.shape[-1]), lambda bi, g, i: (bi, g, i, 0)))
        in_specs.append(pl.BlockSpec(onehot.shape, lambda bi, g, i: (0, 0)))
        args += [sel, onehot]
    return pl.pallas_call(
        functools.partial(_causal_flash_kernel, mode=mode, hb=hb, t=t),
        out_shape=jax.ShapeDtypeStruct((b, s, H_MIX * d), out_dtype),
        grid=(b, groups, s // t),
        in_specs=in_specs,
        out_specs=pl.BlockSpec((1, t, hb * d), lambda bi, g, i: (bi, i, g)),
        scratch_shapes=[pltpu.VMEM((hb, t, d), jnp.float32)] * 3,
        compiler_params=_cparams(("parallel", "parallel", "arbitrary")),
        name=name,
    )(*args)


def _band_kernel(*refs, hb, kv_heads, t, wpad, window, seg_len, has_lse):
    q_ref, k_ref, v_ref, o_ref = refs[:4]
    lse_ref = refs[4] if has_lse else None
    s_len = k_ref.shape[2]
    span = t + wpad
    t0 = pl.program_id(2) * t
    seg0 = (t0 // seg_len) * seg_len
    start = jnp.minimum(jnp.maximum(t0 - wpad, seg0), s_len - span)
    start = pl.multiple_of(start, HEAD_DIM)
    keys = pl.ds(start, span)
    qpos = t0 + lax.broadcasted_iota(jnp.int32, (t, span), 0)
    kpos = start + lax.broadcasted_iota(jnp.int32, (t, span), 1)
    dist = qpos - kpos
    mask = (dist >= 0) & (dist <= window) & (kpos >= seg0)
    kv_of = [hh * kv_heads // hb for hh in range(hb)]
    scores = [jnp.where(mask, _dot_nt(q_ref[0, hh], k_ref[0, kv_of[hh], keys, :]), NEG) for hh in range(hb)]
    probs = []
    for s in scores:
        m = _rowmax_lanes(s)
        p = jnp.exp2(s - _tile_lanes(m, span))
        l = jnp.broadcast_to(jnp.sum(_fold_lanes(p), axis=-1, keepdims=True), m.shape)
        probs.append((m, l, p.astype(v_ref.dtype)))
    for hh, (m, l, p) in enumerate(probs):
        out = _dot(p, v_ref[0, kv_of[hh], keys, :]) / l
        if has_lse:
            o_ref[0, hh] = out.astype(o_ref.dtype)
            lse_ref[0, hh] = m + jnp.log2(l)
        else:
            o_ref[0, :, hh * HEAD_DIM:(hh + 1) * HEAD_DIM] = out.astype(o_ref.dtype)


def _band(src, *, q_head0, k_head0, v_head0, hb, kv_heads, t, wpad, window, seg_len, has_lse, name):
    b, _, s, d = src.shape
    groups = H_MIX // hb
    in_specs = [pl.BlockSpec((1, hb, t, d), lambda bi, g, i: (bi, q_head0 // hb + g, i, 0)),
                pl.BlockSpec((1, kv_heads, s, d), lambda bi, g, i: (bi, k_head0 // kv_heads + g, 0, 0)),
                pl.BlockSpec((1, kv_heads, s, d), lambda bi, g, i: (bi, v_head0 // kv_heads + g, 0, 0))]
    if has_lse:
        hm_spec = pl.BlockSpec((1, hb, t, d), lambda bi, g, i: (bi, g, i, 0))
        out_shape = [jax.ShapeDtypeStruct((b, H_MIX, s, d), jnp.float32)] * 2
        out_specs = [hm_spec, hm_spec]
    else:
        out_shape = jax.ShapeDtypeStruct((b, s, H_MIX * d), jnp.float32)
        out_specs = pl.BlockSpec((1, t, hb * d), lambda bi, g, i: (bi, i, g))
    return pl.pallas_call(
        functools.partial(_band_kernel, hb=hb, kv_heads=kv_heads, t=t, wpad=wpad, window=window,
                          seg_len=seg_len, has_lse=has_lse),
        out_shape=out_shape,
        grid=(b, groups, s // t),
        in_specs=in_specs,
        out_specs=out_specs,
        compiler_params=_cparams(("parallel", "parallel", "arbitrary")),
        name=name,
    )(src, src, src)


def _stickbreak_kernel(q_ref, k_ref, v_ref, u2_ref, o_ref, run_scr, acc_scr, *, hb, t):
    i = pl.program_id(2)
    run_scr[...] = jnp.zeros(run_scr.shape, jnp.float32)
    acc_scr[...] = jnp.zeros(acc_scr.shape, jnp.float32)
    u2 = u2_ref[...]
    sign_bit = jnp.uint32(0x80000000)

    def tile(j, diagonal):
        keys = pl.ds(pl.multiple_of(j * t, t), t)
        if diagonal:
            strict = (lax.broadcasted_iota(jnp.int32, (t, t), 1) < lax.broadcasted_iota(jnp.int32, (t, t), 0))
        zs = [_dot_nt(q_ref[0, hh], k_ref[0, hh, keys, :]) for hh in range(hb)]
        stage = []
        for z2 in zs:
            neg_abs = lax.bitcast_convert_type(lax.bitcast_convert_type(z2, jnp.uint32) | sign_bit, jnp.float32)
            lg = jnp.log2(1.0 + jnp.exp2(neg_abs))
            sp2 = jnp.maximum(z2, 0.0) + lg
            if diagonal:
                sp2 = jnp.where(strict, sp2, 0.0)
            hi = sp2.astype(jnp.bfloat16)
            lo = (sp2 - hi.astype(jnp.float32)).astype(jnp.bfloat16)
            stage.append((jnp.minimum(z2, 0.0) - lg, sp2[:, 0:1], jnp.concatenate([hi, lo], axis=1)))
        laters = [_dot(hl, u2) for (_, _, hl) in stage]
        weights = []
        for hh, later in enumerate(laters):
            log_beta, first, _ = stage[hh]
            run = run_scr[hh]
            a = jnp.exp2(log_beta - (later + _tile_lanes(run, t)))
            if diagonal:
                a = jnp.where(strict, a, 0.0)
            run_scr[hh] = run + jnp.broadcast_to(later[:, 0:1] + first, run.shape)
            weights.append(a.astype(v_ref.dtype))
        for hh, a in enumerate(weights):
            acc_scr[hh] += _dot(a, v_ref[0, hh, keys, :])

    def off_diagonal(n, carry):
        tile(i - 1 - n, False)
        return carry

    tile(i, True)
    lax.fori_loop(0, i, off_diagonal, 0)
    for hh in range(hb):
        o_ref[0, :, hh * HEAD_DIM:(hh + 1) * HEAD_DIM] = acc_scr[hh].astype(o_ref.dtype)


def _stickbreak(hm, *, q_head0, k_head0, v_head0, hb, t, out_dtype):
    b, _, s, d = hm.shape
    later = np.arange(t)[:, None] > np.arange(t)[None, :]
    u2 = jnp.asarray(np.concatenate([later, later], axis=0), jnp.bfloat16)

    def heads(h0):
        return pl.BlockSpec((1, hb, s, d), lambda bi, g, i: (bi, h0 // hb + g, 0, 0))

    return pl.pallas_call(
        functools.partial(_stickbreak_kernel, hb=hb, t=t),
        out_shape=jax.ShapeDtypeStruct((b, s, H_MIX * d), out_dtype),
        grid=(b, H_MIX // hb, s // t),
        in_specs=[pl.BlockSpec((1, hb, t, d), lambda bi, g, i: (bi, q_head0 // hb + g, i, 0)),
                  heads(k_head0), heads(v_head0),
                  pl.BlockSpec((2 * t, t), lambda bi, g, i: (0, 0))],
        out_specs=pl.BlockSpec((1, t, hb * d), lambda bi, g, i: (bi, i, g)),
        scratch_shapes=[pltpu.VMEM((hb, t, d), jnp.float32), pltpu.VMEM((hb, t, d), jnp.float32)],
        compiler_params=_cparams(("parallel", "parallel", "arbitrary")),
        name="stickbreak",
    )(hm, hm, hm, u2)


def _gelu_tanh(x):
    return 0.5 * x * (1.0 + jnp.tanh(math.sqrt(2.0 / math.pi) * (x + 0.044715 * (x * x * x))))


def _compress_kernel(x_ref, pe_ref, w1_ref, w2_ref, o_ref):
    half = x_ref.shape[3]
    x = x_ref[0, 0]
    w1 = w1_ref[0].astype(jnp.bfloat16)
    first = _dot(x, w1[:half])
    second = _dot(x, w1[half:])
    pe = jnp.broadcast_to(pe_ref[0], (8, pe_ref.shape[2])).astype(jnp.bfloat16)
    pe_term = _dot(pe, w1)[0:1]
    n_chunks = x.shape[0]
    hid = first + pltpu.roll(second, n_chunks - 1, 0) + pe_term
    act = _gelu_tanh(hid)
    o_ref[0, 0] = _dot(act.astype(jnp.bfloat16), w2_ref[0].astype(jnp.bfloat16))


def _compress(chunks, pe, w1, w2):
    b, nh, nc, half = chunks.shape
    return pl.pallas_call(
        _compress_kernel,
        out_shape=jax.ShapeDtypeStruct((b, nh, nc, HEAD_DIM), jnp.float32),
        grid=(b, nh),
        in_specs=[pl.BlockSpec((1, 1, nc, half), lambda bi, h: (bi, h, 0, 0)),
                  pl.BlockSpec((1, 1, 2 * half), lambda bi, h: (h // NSA_KV, 0, 0)),
                  pl.BlockSpec((1, 2 * half, w1.shape[2]), lambda bi, h: (h // NSA_KV, 0, 0)),
                  pl.BlockSpec((1, w2.shape[1], HEAD_DIM), lambda bi, h: (h // NSA_KV, 0, 0))],
        out_specs=pl.BlockSpec((1, 1, nc, HEAD_DIM), lambda bi, h: (bi, h, 0, 0)),
        compiler_params=_cparams(("parallel", "parallel")),
        name="nsa_compress",
    )(chunks, pe, w1, w2)


def _cmp_select_kernel(q_ref, kc_ref, vc_ref, c2s_ref, o_ref, sel_ref, *, tq, n_cmp):
    i = pl.program_id(2)
    nc = kc_ref.shape[2]
    kc = kc_ref[0, 0].astype(jnp.bfloat16)
    vc = vc_ref[0, 0].astype(jnp.bfloat16)
    t = i * tq + lax.broadcasted_iota(jnp.int32, (tq, 1), 0)
    cidx = lax.broadcasted_iota(jnp.int32, (1, nc), 1)
    visible = (cidx * CMP_STRIDE + (CMP_LEN - 1) <= t) & (cidx < n_cmp)
    psum = jnp.zeros((tq, nc), jnp.float32)
    for hh in range(NSA_HPG):
        s = jnp.where(visible, _dot_nt(q_ref[0, hh], kc), NEG)
        m = jnp.max(s, axis=-1, keepdims=True)
        e = jnp.where(visible, jnp.exp2(s - m), 0.0)
        l = jnp.sum(e, axis=-1, keepdims=True)
        p = e / jnp.where(l > 0.0, l, 1.0)
        psum = psum + p
        o_ref[0, :, hh * HEAD_DIM:(hh + 1) * HEAD_DIM] = _dot(p.astype(jnp.bfloat16), vc)

    imp = jnp.dot(psum, c2s_ref[...], preferred_element_type=jnp.float32, precision=lax.Precision.HIGHEST)
    w = imp.shape[1]
    n_slc = w // 2
    jblk = lax.broadcasted_iota(jnp.int32, (1, w), 1)
    cur = lax.shift_right_logical(t, int(math.log2(SLC_LEN)))
    forced = (jblk == 0) | (jblk == cur) | (jblk == cur - 1)
    causal_blk = jblk * SLC_LEN <= t
    score = jnp.where(causal_blk, jnp.where(forced, FORCED_SCORE, imp), -1.0)
    rank = jnp.zeros((tq, w), jnp.float32)
    for jp in range(n_slc):
        col = score[:, jp:jp + 1]
        before = (col > score) | ((col == score) & (jblk > jp))
        rank = rank + jnp.where(before, 1.0, 0.0)
    chosen = (rank < float(min(TOPK, n_slc))) & causal_blk
    sel_ref[0, 0] = jnp.where(chosen | (jblk >= n_slc), 0.0, NEG).astype(sel_ref.dtype)


def _cmp_select(hm, cmp_kv, c2s, *, tq, n_cmp):
    b, _, s, d = hm.shape
    nc = cmp_kv.shape[2]
    w = c2s.shape[1]
    return pl.pallas_call(
        functools.partial(_cmp_select_kernel, tq=tq, n_cmp=n_cmp),
        out_shape=[jax.ShapeDtypeStruct((b, s, H_MIX * d), jnp.float32),
                   jax.ShapeDtypeStruct((b, NSA_KV, s, w), jnp.bfloat16)],
        grid=(b, NSA_KV, s // tq),
        in_specs=[pl.BlockSpec((1, NSA_HPG, tq, d), lambda bi, g, i: (bi, g, i, 0)),
                  pl.BlockSpec((1, 1, nc, d), lambda bi, g, i: (bi, g, 0, 0)),
                  pl.BlockSpec((1, 1, nc, d), lambda bi, g, i: (bi, NSA_KV + g, 0, 0)),
                  pl.BlockSpec((nc, w), lambda bi, g, i: (0, 0))],
        out_specs=[pl.BlockSpec((1, tq, NSA_HPG * d), lambda bi, g, i: (bi, i, g)),
                   pl.BlockSpec((1, 1, tq, w), lambda bi, g, i: (bi, g, i, 0))],
        compiler_params=_cparams(("parallel", "parallel", "parallel")),
        name="nsa_cmp_select",
    )(hm, cmp_kv, cmp_kv, c2s)


def _nsa_combine_kernel(oc_ref, os_ref, ow_ref, g_ref, o_ref):
    gates = jax.nn.sigmoid(g_ref[0])
    for h in range(H_MIX):
        cols = slice(h * HEAD_DIM, (h + 1) * HEAD_DIM)
        c0 = GATE_COL0 + 3 * h
        out = (gates[:, c0:c0 + 1] * oc_ref[0, :, cols]
               + gates[:, c0 + 1:c0 + 2] * os_ref[0, :, cols]
               + gates[:, c0 + 2:c0 + 3] * ow_ref[0, :, cols])
        o_ref[0, :, cols] = out.astype(o_ref.dtype)


def _nsa_combine(o_cmp, o_slc, o_win, small, tq=512):
    b, s, w = o_cmp.shape
    big = pl.BlockSpec((1, tq, w), lambda bi, i: (bi, i, 0))
    return pl.pallas_call(
        _nsa_combine_kernel,
        out_shape=jax.ShapeDtypeStruct((b, s, w), jnp.bfloat16),
        grid=(b, s // tq),
        in_specs=[big, big, big, pl.BlockSpec((1, tq, small.shape[2]), lambda bi, i: (bi, i, 0))],
        out_specs=big,
        compiler_params=_cparams(("parallel", "parallel")),
        name="nsa_combine",
    )(o_cmp, o_slc, o_win, small)


def _dil_combine_kernel(o1, o2, o3, l1, l2, l3, o_ref):
    a, bb, c = l1[0, 0], l2[0, 0], l3[0, 0]
    m = jnp.maximum(jnp.maximum(a, bb), c)
    ea, eb, ec = jnp.exp2(a - m), jnp.exp2(bb - m), jnp.exp2(c - m)
    tot = ea + eb + ec
    out = o1[0, 0] * (ea / tot) + o2[0, 0] * (eb / tot) + o3[0, 0] * (ec / tot)
    o_ref[0] = out.astype(o_ref.dtype)


def _dil_combine(outs, lses, tq=1024):
    b, h, s, d = outs[0].shape
    spec = pl.BlockSpec((1, 1, tq, d), lambda bi, hh, i: (bi, hh, i, 0))
    return pl.pallas_call(
        _dil_combine_kernel,
        out_shape=jax.ShapeDtypeStruct((b, s, h * d), jnp.bfloat16),
        grid=(b, h, s // tq),
        in_specs=[spec] * 6,
        out_specs=pl.BlockSpec((1, tq, d), lambda bi, hh, i: (bi, i, hh)),
        compiler_params=_cparams(("parallel", "parallel", "parallel")),
        name="dil_combine",
    )(*outs, *lses)


def _to_residue_major(t, dil):
    b, h, s, d = t.shape
    return t.reshape(b, h, s // dil, dil, d).transpose(0, 1, 3, 2, 4).reshape(b, h, s, d)


def _from_residue_major(t, dil):
    b, h, s, d = t.shape
    return t.reshape(b, h, dil, s // dil, d).transpose(0, 1, 3, 2, 4).reshape(b, h, s, d)


def _cmp_to_slc_matrix(n_chunks, n_cmp, n_slc):
    ratio = SLC_LEN // CMP_STRIDE
    span = CMP_LEN // CMP_STRIDE
    jj, mm, nn = np.meshgrid(np.arange(n_slc), np.arange(ratio), np.arange(span), indexing="ij")
    cc = ratio * jj + mm + nn
    keep = cc < n_cmp
    mat = np.zeros((n_chunks, 2 * n_slc), np.float32)
    np.add.at(mat, (cc[keep], jj[keep]), 1.0)
    return jnp.asarray(mat)


def _mixers(hm, small, fox_bf, cmp_pe_k, cmp_w1_k, cmp_w2_k, cmp_pe_v, cmp_w1_v, cmp_w2_v):
    b, _, s, d = hm.shape
    bf16 = jnp.bfloat16

    n_chunks = s // CMP_STRIDE
    n_cmp = (s - CMP_LEN) // CMP_STRIDE + 1
    n_slc = s // SLC_LEN
    chunks = hm[:, HD_KC:HD_KC + 2 * NSA_KV].reshape(b, 2 * NSA_KV, n_chunks, CMP_STRIDE * d)
    cmp_kv = _compress(chunks,
                       jnp.stack([cmp_pe_k, cmp_pe_v]).reshape(2, 1, CMP_LEN * d),
                       jnp.stack([cmp_w1_k, cmp_w1_v]), jnp.stack([cmp_w2_k, cmp_w2_v]))
    o_cmp, sel = _cmp_select(hm, cmp_kv, _cmp_to_slc_matrix(n_chunks, n_cmp, n_slc), tq=256, n_cmp=n_cmp)
    onehot = jnp.asarray((np.arange(s)[:, None] // SLC_LEN) == np.arange(2 * n_slc)[None, :], bf16)
    o_slc = _causal_flash(hm, mode="sel", q_head0=HD_QA, k_head0=HD_KS, v_head0=HD_VS, hb=NSA_HPG, t=512,
                          sel=sel, onehot=onehot, out_dtype=jnp.float32, name="nsa_selected")
    o_win = _band(hm, q_head0=HD_QA, k_head0=HD_KW, v_head0=HD_VW, hb=NSA_HPG, kv_heads=1, t=512, wpad=NSA_WINDOW,
                  window=NSA_WINDOW - 1, seg_len=s, has_lse=False, name="nsa_window")
    o_a = _nsa_combine(o_cmp, o_slc, o_win, small)

    t_fox = 512
    bias_row = jnp.zeros((1, small.shape[2]), jnp.float32).at[0, FORGET_COL0:FORGET_COL0 + H_MIX].set(fox_bf)
    csum = _logf_cumsum(small, bias_row)
    key_bias = (-LOG2E) * csum[:, :, FORGET_COL0:FORGET_COL0 + H_MIX].transpose(0, 2, 1)
    o_b = _causal_flash(hm, mode="fox", q_head0=HD_QB, k_head0=HD_KB, v_head0=HD_VB, hb=4, t=t_fox,
                        bias=key_bias.reshape(b, H_MIX, s // t_fox, 1, t_fox), out_dtype=bf16, name="fox")

    outs, lses = [], []
    for window, dil in DIL_CONFIGS:
        if dil == 1:
            src, heads0 = hm, (HD_QC, HD_KC2, HD_VC2)
        else:
            src, heads0 = _to_residue_major(hm[:, HD_QC:HD_QC + 3 * H_MIX], dil), (0, H_MIX, 2 * H_MIX)
        o, lse = _band(src, q_head0=heads0[0], k_head0=heads0[1], v_head0=heads0[2], hb=4, kv_heads=4, t=256,
                       wpad=HEAD_DIM, window=window // dil, seg_len=s // dil, has_lse=True, name=f"dilated_{dil}")
        if dil != 1:
            o, lse = _from_residue_major(o, dil), _from_residue_major(lse, dil)
        outs.append(o)
        lses.append(lse)
    o_c = _dil_combine(outs, lses)

    o_d = _stickbreak(hm, q_head0=HD_QD, k_head0=HD_KD, v_head0=HD_VD, hb=4, t=256, out_dtype=bf16)

    return jnp.concatenate([o_a, o_b, o_c, o_d], axis=-1)


def _split_w_in(w_in):
    a = H_MIX * HEAD_DIM + 6 * NSA_KV * HEAD_DIM
    g_end = a + 3 * H_MIX
    f0 = g_end + 3 * H_MIX * HEAD_DIM
    f_end = f0 + H_MIX
    big = jnp.concatenate([w_in[:, :a], w_in[:, g_end:f0], w_in[:, f_end:]], axis=1)
    small = jnp.concatenate([w_in[:, a:g_end], w_in[:, f0:f_end],
                             jnp.zeros((w_in.shape[0], HEAD_DIM - 4 * H_MIX), w_in.dtype)], axis=1)
    return big, small


def _layer(h, p_i, cos, sin, batch, norm_attn, w_in, fox_bf, cmp_pe_k, cmp_w1_k, cmp_w2_k,
           cmp_pe_v, cmp_w1_v, cmp_w2_v, w_o, norm_mlp, w_up, w_down, norm_ple, w_ple_gate, w_ple_proj):
    m, d = h.shape
    s = m // batch
    bf16 = jnp.bfloat16
    w_big, w_small = _split_w_in(w_in)
    xn = _rmsnorm(h, norm_attn, bf16)
    hm = _in_proj_heads(xn, w_big, cos, sin, batch)
    small = _matmul(xn, w_small, jnp.float32, name="in_proj_small")
    mix = _mixers(hm, small.reshape(batch, s, HEAD_DIM), fox_bf,
                  cmp_pe_k, cmp_w1_k, cmp_w2_k, cmp_pe_v, cmp_w1_v, cmp_w2_v)

    tm, tn = 1024, 512
    res_spec = pl.BlockSpec((tm, tn), lambda i, j: (i, j))
    h = _matmul(mix.reshape(m, -1), w_o, jnp.float32, _epi_residual, (h,), (res_spec,), tm=tm, tn=tn, name="out_proj")

    x2 = _rmsnorm(h, norm_mlp, bf16)
    mid = _matmul(x2, w_up, bf16, _epi_relu2, tm=tm, tn=tn, name="mlp_up")
    h = _matmul_residual_ksplit(mid, w_down, h, name="mlp_down")

    x3 = _rmsnorm(h, norm_ple, bf16)
    ple_dim = p_i.shape[1]
    h = _matmul(x3, w_ple_gate, jnp.float32, _epi_ple, (h, p_i, w_ple_proj),
                (res_spec, pl.BlockSpec((tm, ple_dim), lambda i, j: (i, 0)),
                 pl.BlockSpec((ple_dim, tn), lambda i, j: (0, j))),
                tm=tm, tn=tn, name="ple_gate")
    return h


def kernel(x, p, positions, norm_attn, w_in, fox_bf, cmp_pe_k, cmp_w1_k, cmp_w2_k, cmp_pe_v, cmp_w1_v, cmp_w2_v,
           w_o, norm_mlp, w_up, w_down, norm_ple, w_ple_gate, w_ple_proj, norm_final):
    batch, s, d = x.shape
    depth = p.shape[0]
    cos, sin = _rope_tables(positions)
    h = x.reshape(batch * s, d)
    for i in range(depth):
        h = _layer(h, p[i].reshape(batch * s, -1), cos, sin, batch, norm_attn[i], w_in[i], fox_bf[i],
                   cmp_pe_k[i], cmp_w1_k[i], cmp_w2_k[i], cmp_pe_v[i], cmp_w1_v[i], cmp_w2_v[i],
                   w_o[i], norm_mlp[i], w_up[i], w_down[i], norm_ple[i], w_ple_gate[i], w_ple_proj[i])
    return _rmsnorm(h, norm_final, x.dtype).reshape(batch, s, d)
```

```python
import functools
import math

import numpy as np
import jax
import jax.numpy as jnp
from jax import lax
from jax.experimental import pallas as pl
from jax.experimental.pallas import tpu as pltpu

HEAD_DIM = 128
H_MIX = 8
NSA_KV = 2
NSA_HPG = H_MIX // NSA_KV
CMP_LEN = 32
CMP_STRIDE = 16
SLC_LEN = 64
TOPK = 16
NSA_WINDOW = 512
DIL_CONFIGS = ((128, 1), (512, 4), (2048, 16))
ROPE_THETA = 10000.0
RMS_EPS = 1e-6
NEG = -1e30
M_INIT = -5e29
FORCED_SCORE = 1e9
LOG2E = math.log2(math.e)
Q_SCALE = HEAD_DIM ** -0.5 * LOG2E

VMEM_LIMIT_BYTES = 52 * 1024 * 1024

HD_QA, HD_KC, HD_VC, HD_KS, HD_VS, HD_KW, HD_VW = 0, 8, 10, 12, 14, 16, 18
HD_QB, HD_KB, HD_VB = 20, 28, 36
HD_QC, HD_KC2, HD_VC2 = 44, 52, 60
HD_QD, HD_KD, HD_VD = 68, 76, 84
N_HEADS_ALL = 92
N_BIG = N_HEADS_ALL * HEAD_DIM
GATE_COL0 = 0
FORGET_COL0 = 24
_ROPED_HEADS = (tuple(range(HD_QA, HD_QA + H_MIX)) + tuple(range(HD_KS, HD_KS + NSA_KV))
                + tuple(range(HD_KW, HD_KW + NSA_KV)) + tuple(range(HD_QC, HD_QC + 2 * H_MIX)))
_QUERY_HEADS = tuple(h for q0 in (HD_QA, HD_QB, HD_QC, HD_QD) for h in range(q0, q0 + H_MIX))


def _cparams(sem):
    return pltpu.CompilerParams(dimension_semantics=sem, vmem_limit_bytes=VMEM_LIMIT_BYTES)


def _dot_nt(a, b):
    return lax.dot_general(a, b, (((1,), (1,)), ((), ())), preferred_element_type=jnp.float32)


def _dot(a, b):
    return jnp.dot(a, b, preferred_element_type=jnp.float32)


def _rmsnorm_kernel(x_ref, g_ref, o_ref):
    x = x_ref[...]
    ms = jnp.mean(x * x, axis=-1, keepdims=True)
    o_ref[...] = (x * lax.rsqrt(ms + RMS_EPS) * g_ref[...]).astype(o_ref.dtype)


def _rmsnorm(x, g, out_dtype, tm=256):
    m, d = x.shape
    return pl.pallas_call(
        _rmsnorm_kernel,
        out_shape=jax.ShapeDtypeStruct((m, d), out_dtype),
        grid=(m // tm,),
        in_specs=[pl.BlockSpec((tm, d), lambda i: (i, 0)),
                  pl.BlockSpec((1, d), lambda i: (0, 0))],
        out_specs=pl.BlockSpec((tm, d), lambda i: (i, 0)),
        compiler_params=_cparams(("parallel",)),
        name="rmsnorm",
    )(x, g.reshape(1, d))


def _mm_kernel(*refs, n_extra, epilogue):
    x_ref, w_ref = refs[0], refs[1]
    extras = refs[2:2 + n_extra]
    o_ref = refs[2 + n_extra]
    acc = _dot(x_ref[...].astype(jnp.bfloat16), w_ref[...].astype(jnp.bfloat16))
    o_ref[...] = epilogue(acc, *extras).astype(o_ref.dtype)


def _epi_none(acc):
    return acc


def _epi_relu2(acc):
    r = jnp.maximum(acc, 0.0)
    return r * r


def _epi_residual(acc, h_ref):
    return h_ref[...] + acc


def _epi_ple(acc, h_ref, p_ref, wp_ref):
    pp = _dot(p_ref[...].astype(jnp.bfloat16), wp_ref[...].astype(jnp.bfloat16))
    return h_ref[...] + jax.nn.sigmoid(acc) * pp


def _matmul(x, w, layer, out_dtype, epilogue=_epi_none, extras=(), extra_specs=(), tm=1024, tn=512, name="matmul"):
    m, kdim = x.shape
    n = w.shape[2]
    tn = min(tn, n)
    return pl.pallas_call(
        functools.partial(_mm_kernel, n_extra=len(extras), epilogue=epilogue),
        out_shape=jax.ShapeDtypeStruct((m, n), out_dtype),
        grid=(m // tm, n // tn),
        in_specs=[pl.BlockSpec((tm, kdim), lambda i, j: (i, 0)),
                  pl.BlockSpec((None, kdim, tn), lambda i, j: (layer, 0, j)),
                  *extra_specs],
        out_specs=pl.BlockSpec((tm, tn), lambda i, j: (i, j)),
        compiler_params=_cparams(("parallel", "arbitrary")),
        name=name,
    )(x, w, *extras)


def _mm_residual_ksplit_kernel(x_ref, w_ref, h_ref, o_ref):
    @pl.when(pl.program_id(2) == 0)
    def _():
        o_ref[...] = h_ref[...]

    o_ref[...] += _dot(x_ref[...], w_ref[...].astype(jnp.bfloat16))


def _matmul_residual_ksplit(x, w, layer, h, tm=2048, tn=1024, tk=1024, name="matmul_ksplit"):
    m, kdim = x.shape
    n = w.shape[2]
    return pl.pallas_call(
        _mm_residual_ksplit_kernel,
        out_shape=jax.ShapeDtypeStruct((m, n), jnp.float32),
        grid=(m // tm, n // tn, kdim // tk),
        in_specs=[pl.BlockSpec((tm, tk), lambda i, j, k: (i, k)),
                  pl.BlockSpec((None, tk, tn), lambda i, j, k: (layer, k, j)),
                  pl.BlockSpec((tm, tn), lambda i, j, k: (i, j), pipeline_mode=pl.Buffered(1))],
        out_specs=pl.BlockSpec((tm, tn), lambda i, j, k: (i, j)),
        compiler_params=_cparams(("parallel", "parallel", "arbitrary")),
        name=name,
    )(x, w, h)


def _rope_table_kernel(pos_ref, freq_ref, sign_ref, cos_ref, sin_ref):
    ang = pos_ref[...] * freq_ref[...]
    cos_ref[...] = jnp.cos(ang)
    sin_ref[...] = jnp.sin(ang) * sign_ref[...]


def _rope_tables(positions, ts=512):
    n = positions.size
    half = HEAD_DIM // 2
    inv_freq = ROPE_THETA ** (-jnp.arange(half, dtype=jnp.float32) / half)
    freq = jnp.concatenate([inv_freq, inv_freq]).reshape(1, HEAD_DIM)
    sign = jnp.concatenate([-jnp.ones((half,), jnp.float32), jnp.ones((half,), jnp.float32)]).reshape(1, HEAD_DIM)
    pos = positions.astype(jnp.float32).reshape(n, 1)
    row = pl.BlockSpec((1, HEAD_DIM), lambda i: (0, 0))
    return pl.pallas_call(
        _rope_table_kernel,
        out_shape=[jax.ShapeDtypeStruct((n, HEAD_DIM), jnp.float32)] * 2,
        grid=(n // ts,),
        in_specs=[pl.BlockSpec((ts, 1), lambda i: (i, 0)), row, row],
        out_specs=[pl.BlockSpec((ts, HEAD_DIM), lambda i: (i, 0))] * 2,
        compiler_params=_cparams(("parallel",)),
        name="rope_tables",
    )(pos, freq, sign)


_FLAG_ROPED, _FLAG_QUERY = 1, 2


def _in_proj_kernel(flags_ref, x_ref, w_ref, cos_ref, sin_ref, o_ref):
    j = pl.program_id(1)
    heads = o_ref.shape[1]
    acc = _dot(x_ref[...], w_ref[...])
    for hh in range(heads):
        t = acc[:, hh * HEAD_DIM:(hh + 1) * HEAD_DIM]
        flags = flags_ref[j * heads + hh]
        roped = (flags & _FLAG_ROPED) > 0
        scale = jnp.where((flags & _FLAG_QUERY) > 0, Q_SCALE, 1.0)
        cos = jnp.where(roped, cos_ref[...], 1.0)
        sin = jnp.where(roped, sin_ref[...], 0.0)
        o_ref[0, hh] = ((t * cos + pltpu.roll(t, HEAD_DIM // 2, 1) * sin) * scale).astype(o_ref.dtype)


def _in_proj_heads(xn, w_big, layer, cos, sin, batch, tm=1024, tn=512):
    m, d = xn.shape
    s = m // batch
    heads = tn // HEAD_DIM
    q_tiles = s // tm
    flags = np.zeros((N_HEADS_ALL,), np.int32)
    flags[list(_ROPED_HEADS)] |= _FLAG_ROPED
    flags[list(_QUERY_HEADS)] |= _FLAG_QUERY
    table = pl.BlockSpec((tm, HEAD_DIM), lambda i, j, f: (i, 0))
    return pl.pallas_call(
        _in_proj_kernel,
        out_shape=jax.ShapeDtypeStruct((batch, N_HEADS_ALL, s, HEAD_DIM), jnp.bfloat16),
        grid_spec=pltpu.PrefetchScalarGridSpec(
            num_scalar_prefetch=1,
            grid=(m // tm, N_BIG // tn),
            in_specs=[pl.BlockSpec((tm, d), lambda i, j, f: (i, 0)),
                      pl.BlockSpec((None, d, tn), lambda i, j, f: (layer, 0, j)),
                      table, table],
            out_specs=pl.BlockSpec((1, heads, tm, HEAD_DIM), lambda i, j, f: (i // q_tiles, j, i % q_tiles, 0)),
        ),
        compiler_params=_cparams(("parallel", "arbitrary")),
        name="in_proj",
    )(jnp.asarray(flags), xn, w_big, cos, sin)


def _softplus(z):
    return jnp.maximum(z, 0.0) + jnp.log1p(jnp.exp(-jnp.abs(z)))


def _logf_cumsum_kernel(x_ref, bias_ref, o_ref, *, blk):
    s = x_ref.shape[1]
    r = lax.broadcasted_iota(jnp.int32, (blk, blk), 0)
    c = lax.broadcasted_iota(jnp.int32, (blk, blk), 1)
    tri = jnp.where(c <= r, 1.0, 0.0).astype(jnp.float32)

    def body(i, carry):
        x = x_ref[0, pl.ds(i * blk, blk), :]
        logf = -_softplus(-(x + bias_ref[...]))
        cs = jnp.dot(tri, logf, preferred_element_type=jnp.float32,
                     precision=lax.Precision.HIGHEST) + carry
        o_ref[0, pl.ds(i * blk, blk), :] = cs
        return cs[blk - 1:blk, :]

    lax.fori_loop(0, s // blk, body, jnp.zeros((1, x_ref.shape[2]), jnp.float32))


def _logf_cumsum(small, bias_row, blk=128):
    b, s, w = small.shape
    return pl.pallas_call(
        functools.partial(_logf_cumsum_kernel, blk=blk),
        out_shape=jax.ShapeDtypeStruct((b, s, w), jnp.float32),
        grid=(b,),
        in_specs=[pl.BlockSpec((1, s, w), lambda bi: (bi, 0, 0)),
                  pl.BlockSpec((1, w), lambda bi: (0, 0))],
        out_specs=pl.BlockSpec((1, s, w), lambda bi: (bi, 0, 0)),
        compiler_params=_cparams(("parallel",)),
        name="logf_cumsum",
    )(small, bias_row)


def _rowmax_lanes(s):
    m = s[:, :HEAD_DIM]
    for c in range(1, s.shape[1] // HEAD_DIM):
        m = jnp.maximum(m, s[:, c * HEAD_DIM:(c + 1) * HEAD_DIM])
    return jnp.broadcast_to(jnp.max(m, axis=-1, keepdims=True), m.shape)


def _fold_lanes(p):
    a = p[:, :HEAD_DIM]
    for c in range(1, p.shape[1] // HEAD_DIM):
        a = a + p[:, c * HEAD_DIM:(c + 1) * HEAD_DIM]
    return a


def _tile_lanes(x, width):
    return jnp.concatenate([x] * (width // HEAD_DIM), axis=1)


def _causal_flash_kernel(*refs, mode, hb, t):
    it = iter(refs)
    q_ref, k_ref, v_ref = next(it), next(it), next(it)
    bias_ref = next(it) if mode == "fox" else None
    sel_ref, onehot_ref = (next(it), next(it)) if mode == "sel" else (None, None)
    o_ref = next(it)
    m_scr, l_scr, acc_scr = next(it), next(it), next(it)
    i = pl.program_id(2)

    m_scr[...] = jnp.full(m_scr.shape, M_INIT, jnp.float32)
    l_scr[...] = jnp.zeros(l_scr.shape, jnp.float32)
    acc_scr[...] = jnp.zeros(acc_scr.shape, jnp.float32)

    def tile(j, diagonal):
        keys = pl.ds(pl.multiple_of(j * t, t), t)
        if mode == "sel":
            k_shared = jnp.concatenate([k_ref[0, 0, keys, :], onehot_ref[keys, :]], axis=1)
            scores = [_dot_nt(jnp.concatenate([q_ref[0, hh], sel_ref[0, 0]], axis=1), k_shared)
                      for hh in range(hb)]
        else:
            scores = [_dot_nt(q_ref[0, hh], k_ref[0, hh, keys, :]) + bias_ref[0, hh, j] for hh in range(hb)]
        if diagonal:
            causal = (lax.broadcasted_iota(jnp.int32, (t, t), 1) <= lax.broadcasted_iota(jnp.int32, (t, t), 0))
            scores = [jnp.where(causal, s, NEG) for s in scores]
        probs = []
        for hh, s in enumerate(scores):
            m_prev = m_scr[hh]
            m_new = jnp.maximum(m_prev, _rowmax_lanes(s))
            alpha = jnp.exp2(m_prev - m_new)
            p = jnp.exp2(s - _tile_lanes(m_new, t))
            l_scr[hh] = alpha * l_scr[hh] + _fold_lanes(p)
            m_scr[hh] = m_new
            probs.append((alpha, p.astype(v_ref.dtype)))
        for hh, (alpha, p) in enumerate(probs):
            v = v_ref[0, 0 if mode == "sel" else hh, keys, :]
            acc_scr[hh] = alpha * acc_scr[hh] + _dot(p, v)

    def off_diagonal(j, carry):
        tile(j, False)
        return carry

    lax.fori_loop(0, i, off_diagonal, 0)
    tile(i, True)
    for hh in range(hb):
        l = jnp.sum(l_scr[hh], axis=-1, keepdims=True)
        o_ref[0, :, hh * HEAD_DIM:(hh + 1) * HEAD_DIM] = (acc_scr[hh] / l).astype(o_ref.dtype)


def _causal_flash(hm, *, mode, q_head0, k_head0, v_head0, hb, t, out_dtype, bias=None, sel=None, onehot=None,
                  name="causal_flash"):
    b, _, s, d = hm.shape
    groups = H_MIX // hb
    kv_heads = hb if mode == "fox" else 1
    in_specs = [pl.BlockSpec((1, hb, t, d), lambda bi, g, i: (bi, q_head0 // hb + g, i, 0)),
                pl.BlockSpec((1, kv_heads, s, d), lambda bi, g, i: (bi, k_head0 // kv_heads + g, 0, 0)),
                pl.BlockSpec((1, kv_heads, s, d), lambda bi, g, i: (bi, v_head0 // kv_heads + g, 0, 0))]
    args = [hm, hm, hm]
    if mode == "fox":
        in_specs.append(pl.BlockSpec((1, hb, s // t, 1, t), lambda bi, g, i: (bi, g, 0, 0, 0)))
        args.append(bias)
    else:
        in_specs.append(pl.BlockSpec((1, 1, t, sel.shape[-1]), lambda bi, g, i: (bi, g, i, 0)))
        in_specs.append(pl.BlockSpec(onehot.shape, lambda bi, g, i: (0, 0)))
        args += [sel, onehot]
    return pl.pallas_call(
        functools.partial(_causal_flash_kernel, mode=mode, hb=hb, t=t),
        out_shape=jax.ShapeDtypeStruct((b, s, H_MIX * d), out_dtype),
        grid=(b, groups, s // t),
        in_specs=in_specs,
        out_specs=pl.BlockSpec((1, t, hb * d), lambda bi, g, i: (bi, i, g)),
        scratch_shapes=[pltpu.VMEM((hb, t, d), jnp.float32)] * 3,
        compiler_params=_cparams(("parallel", "parallel", "arbitrary")),
        name=name,
    )(*args)


def _band_kernel(*refs, hb, kv_heads, t, wpad, window, seg_len, has_lse):
    q_ref, k_ref, v_ref, o_ref = refs[:4]
    lse_ref = refs[4] if has_lse else None
    s_len = k_ref.shape[2]
    span = t + wpad
    t0 = pl.program_id(2) * t
    seg0 = (t0 // seg_len) * seg_len
    start = jnp.minimum(jnp.maximum(t0 - wpad, seg0), s_len - span)
    start = pl.multiple_of(start, HEAD_DIM)
    keys = pl.ds(start, span)
    qpos = t0 + lax.broadcasted_iota(jnp.int32, (t, span), 0)
    kpos = start + lax.broadcasted_iota(jnp.int32, (t, span), 1)
    dist = qpos - kpos
    mask = (dist >= 0) & (dist <= window) & (kpos >= seg0)
    kv_of = [hh * kv_heads // hb for hh in range(hb)]
    scores = [jnp.where(mask, _dot_nt(q_ref[0, hh], k_ref[0, kv_of[hh], keys, :]), NEG) for hh in range(hb)]
    probs = []
    for s in scores:
        m = _rowmax_lanes(s)
        p = jnp.exp2(s - _tile_lanes(m, span))
        l = jnp.broadcast_to(jnp.sum(_fold_lanes(p), axis=-1, keepdims=True), m.shape)
        probs.append((m, l, p.astype(v_ref.dtype)))
    for hh, (m, l, p) in enumerate(probs):
        out = _dot(p, v_ref[0, kv_of[hh], keys, :]) / l
        if has_lse:
            o_ref[0, hh] = out.astype(o_ref.dtype)
            lse_ref[0, hh] = m + jnp.log2(l)
        else:
            o_ref[0, :, hh * HEAD_DIM:(hh + 1) * HEAD_DIM] = out.astype(o_ref.dtype)


def _band(src, *, q_head0, k_head0, v_head0, hb, kv_heads, t, wpad, window, seg_len, has_lse, name):
    b, _, s, d = src.shape
    groups = H_MIX // hb
    in_specs = [pl.BlockSpec((1, hb, t, d), lambda bi, g, i: (bi, q_head0 // hb + g, i, 0)),
                pl.BlockSpec((1, kv_heads, s, d), lambda bi, g, i: (bi, k_head0 // kv_heads + g, 0, 0)),
                pl.BlockSpec((1, kv_heads, s, d), lambda bi, g, i: (bi, v_head0 // kv_heads + g, 0, 0))]
    if has_lse:
        hm_spec = pl.BlockSpec((1, hb, t, d), lambda bi, g, i: (bi, g, i, 0))
        out_shape = [jax.ShapeDtypeStruct((b, H_MIX, s, d), jnp.float32)] * 2
        out_specs = [hm_spec, hm_spec]
    else:
        out_shape = jax.ShapeDtypeStruct((b, s, H_MIX * d), jnp.float32)
        out_specs = pl.BlockSpec((1, t, hb * d), lambda bi, g, i: (bi, i, g))
    return pl.pallas_call(
        functools.partial(_band_kernel, hb=hb, kv_heads=kv_heads, t=t, wpad=wpad, window=window,
                          seg_len=seg_len, has_lse=has_lse),
        out_shape=out_shape,
        grid=(b, groups, s // t),
        in_specs=in_specs,
        out_specs=out_specs,
        compiler_params=_cparams(("parallel", "parallel", "arbitrary")),
        name=name,
    )(src, src, src)


def _stickbreak_kernel(q_ref, k_ref, v_ref, u2_ref, o_ref, run_scr, acc_scr, *, hb, t):
    i = pl.program_id(2)
    run_scr[...] = jnp.zeros(run_scr.shape, jnp.float32)
    acc_scr[...] = jnp.zeros(acc_scr.shape, jnp.float32)
    u2 = u2_ref[...]
    sign_bit = jnp.uint32(0x80000000)

    def tile(j, diagonal):
        keys = pl.ds(pl.multiple_of(j * t, t), t)
        if diagonal:
            strict = (lax.broadcasted_iota(jnp.int32, (t, t), 1) < lax.broadcasted_iota(jnp.int32, (t, t), 0))
        zs = [_dot_nt(q_ref[0, hh], k_ref[0, hh, keys, :]) for hh in range(hb)]
        stage = []
        for z2 in zs:
            neg_abs = lax.bitcast_convert_type(lax.bitcast_convert_type(z2, jnp.uint32) | sign_bit, jnp.float32)
            lg = jnp.log2(1.0 + jnp.exp2(neg_abs))
            sp2 = jnp.maximum(z2, 0.0) + lg
            if diagonal:
                sp2 = jnp.where(strict, sp2, 0.0)
            hi = sp2.astype(jnp.bfloat16)
            lo = (sp2 - hi.astype(jnp.float32)).astype(jnp.bfloat16)
            stage.append((jnp.minimum(z2, 0.0) - lg, sp2[:, 0:1], jnp.concatenate([hi, lo], axis=1)))
        laters = [_dot(hl, u2) for (_, _, hl) in stage]
        weights = []
        for hh, later in enumerate(laters):
            log_beta, first, _ = stage[hh]
            run = run_scr[hh]
            a = jnp.exp2(log_beta - (later + _tile_lanes(run, t)))
            if diagonal:
                a = jnp.where(strict, a, 0.0)
            run_scr[hh] = run + jnp.broadcast_to(later[:, 0:1] + first, run.shape)
            weights.append(a.astype(v_ref.dtype))
        for hh, a in enumerate(weights):
            acc_scr[hh] += _dot(a, v_ref[0, hh, keys, :])

    def off_diagonal(n, carry):
        tile(i - 1 - n, False)
        return carry

    tile(i, True)
    lax.fori_loop(0, i, off_diagonal, 0)
    for hh in range(hb):
        o_ref[0, :, hh * HEAD_DIM:(hh + 1) * HEAD_DIM] = acc_scr[hh].astype(o_ref.dtype)


def _stickbreak(hm, *, q_head0, k_head0, v_head0, hb, t, out_dtype):
    b, _, s, d = hm.shape
    later = np.arange(t)[:, None] > np.arange(t)[None, :]
    u2 = jnp.asarray(np.concatenate([later, later], axis=0), jnp.bfloat16)

    def heads(h0):
        return pl.BlockSpec((1, hb, s, d), lambda bi, g, i: (bi, h0 // hb + g, 0, 0))

    return pl.pallas_call(
        functools.partial(_stickbreak_kernel, hb=hb, t=t),
        out_shape=jax.ShapeDtypeStruct((b, s, H_MIX * d), out_dtype),
        grid=(b, H_MIX // hb, s // t),
        in_specs=[pl.BlockSpec((1, hb, t, d), lambda bi, g, i: (bi, q_head0 // hb + g, i, 0)),
                  heads(k_head0), heads(v_head0),
                  pl.BlockSpec((2 * t, t), lambda bi, g, i: (0, 0))],
        out_specs=pl.BlockSpec((1, t, hb * d), lambda bi, g, i: (bi, i, g)),
        scratch_shapes=[pltpu.VMEM((hb, t, d), jnp.float32), pltpu.VMEM((hb, t, d), jnp.float32)],
        compiler_params=_cparams(("parallel", "parallel", "arbitrary")),
        name="stickbreak",
    )(hm, hm, hm, u2)


def _gelu_tanh(x):
    return 0.5 * x * (1.0 + jnp.tanh(math.sqrt(2.0 / math.pi) * (x + 0.044715 * (x * x * x))))


def _compress_kernel(x_ref, pe_ref, w1_ref, w2_ref, o_ref):
    half = x_ref.shape[3]
    x = x_ref[0, 0]
    w1 = w1_ref[0].astype(jnp.bfloat16)
    first = _dot(x, w1[:half])
    second = _dot(x, w1[half:])
    pe = jnp.broadcast_to(pe_ref[0], (8, pe_ref.shape[2])).astype(jnp.bfloat16)
    pe_term = _dot(pe, w1)[0:1]
    n_chunks = x.shape[0]
    hid = first + pltpu.roll(second, n_chunks - 1, 0) + pe_term
    act = _gelu_tanh(hid)
    o_ref[0, 0] = _dot(act.astype(jnp.bfloat16), w2_ref[0].astype(jnp.bfloat16))


def _compress(chunks, pe, w1, w2):
    b, nh, nc, half = chunks.shape
    return pl.pallas_call(
        _compress_kernel,
        out_shape=jax.ShapeDtypeStruct((b, nh, nc, HEAD_DIM), jnp.float32),
        grid=(b, nh),
        in_specs=[pl.BlockSpec((1, 1, nc, half), lambda bi, h: (bi, h, 0, 0)),
                  pl.BlockSpec((1, 1, 2 * half), lambda bi, h: (h // NSA_KV, 0, 0)),
                  pl.BlockSpec((1, 2 * half, w1.shape[2]), lambda bi, h: (h // NSA_KV, 0, 0)),
                  pl.BlockSpec((1, w2.shape[1], HEAD_DIM), lambda bi, h: (h // NSA_KV, 0, 0))],
        out_specs=pl.BlockSpec((1, 1, nc, HEAD_DIM), lambda bi, h: (bi, h, 0, 0)),
        compiler_params=_cparams(("parallel", "parallel")),
        name="nsa_compress",
    )(chunks, pe, w1, w2)


def _cmp_select_kernel(q_ref, kc_ref, vc_ref, c2s_ref, o_ref, sel_ref, *, tq, n_cmp):
    i = pl.program_id(2)
    nc = kc_ref.shape[2]
    kc = kc_ref[0, 0].astype(jnp.bfloat16)
    vc = vc_ref[0, 0].astype(jnp.bfloat16)
    t = i * tq + lax.broadcasted_iota(jnp.int32, (tq, 1), 0)
    cidx = lax.broadcasted_iota(jnp.int32, (1, nc), 1)
    visible = (cidx * CMP_STRIDE + (CMP_LEN - 1) <= t) & (cidx < n_cmp)
    psum = jnp.zeros((tq, nc), jnp.float32)
    for hh in range(NSA_HPG):
        s = jnp.where(visible, _dot_nt(q_ref[0, hh], kc), NEG)
        m = jnp.max(s, axis=-1, keepdims=True)
        e = jnp.where(visible, jnp.exp2(s - m), 0.0)
        l = jnp.sum(e, axis=-1, keepdims=True)
        p = e / jnp.where(l > 0.0, l, 1.0)
        psum = psum + p
        o_ref[0, :, hh * HEAD_DIM:(hh + 1) * HEAD_DIM] = _dot(p.astype(jnp.bfloat16), vc)

    imp = jnp.dot(psum, c2s_ref[...], preferred_element_type=jnp.float32, precision=lax.Precision.HIGHEST)
    w = imp.shape[1]
    n_slc = w // 2
    jblk = lax.broadcasted_iota(jnp.int32, (1, w), 1)
    cur = lax.shift_right_logical(t, int(math.log2(SLC_LEN)))
    forced = (jblk == 0) | (jblk == cur) | (jblk == cur - 1)
    causal_blk = jblk * SLC_LEN <= t
    score = jnp.where(causal_blk, jnp.where(forced, FORCED_SCORE, imp), -1.0)
    rank = jnp.zeros((tq, w), jnp.float32)
    for jp in range(n_slc):
        col = score[:, jp:jp + 1]
        before = (col > score) | ((col == score) & (jblk > jp))
        rank = rank + jnp.where(before, 1.0, 0.0)
    chosen = (rank < float(min(TOPK, n_slc))) & causal_blk
    sel_ref[0, 0] = jnp.where(chosen | (jblk >= n_slc), 0.0, NEG).astype(sel_ref.dtype)


def _cmp_select(hm, cmp_kv, c2s, *, tq, n_cmp):
    b, _, s, d = hm.shape
    nc = cmp_kv.shape[2]
    w = c2s.shape[1]
    return pl.pallas_call(
        functools.partial(_cmp_select_kernel, tq=tq, n_cmp=n_cmp),
        out_shape=[jax.ShapeDtypeStruct((b, s, H_MIX * d), jnp.float32),
                   jax.ShapeDtypeStruct((b, NSA_KV, s, w), jnp.bfloat16)],
        grid=(b, NSA_KV, s // tq),
        in_specs=[pl.BlockSpec((1, NSA_HPG, tq, d), lambda bi, g, i: (bi, g, i, 0)),
                  pl.BlockSpec((1, 1, nc, d), lambda bi, g, i: (bi, g, 0, 0)),
                  pl.BlockSpec((1, 1, nc, d), lambda bi, g, i: (bi, NSA_KV + g, 0, 0)),
                  pl.BlockSpec((nc, w), lambda bi, g, i: (0, 0))],
        out_specs=[pl.BlockSpec((1, tq, NSA_HPG * d), lambda bi, g, i: (bi, i, g)),
                   pl.BlockSpec((1, 1, tq, w), lambda bi, g, i: (bi, g, i, 0))],
        compiler_params=_cparams(("parallel", "parallel", "parallel")),
        name="nsa_cmp_select",
    )(hm, cmp_kv, cmp_kv, c2s)


def _nsa_combine_kernel(oc_ref, os_ref, ow_ref, g_ref, o_ref):
    gates = jax.nn.sigmoid(g_ref[0])
    for h in range(H_MIX):
        cols = slice(h * HEAD_DIM, (h + 1) * HEAD_DIM)
        c0 = GATE_COL0 + 3 * h
        out = (gates[:, c0:c0 + 1] * oc_ref[0, :, cols]
               + gates[:, c0 + 1:c0 + 2] * os_ref[0, :, cols]
               + gates[:, c0 + 2:c0 + 3] * ow_ref[0, :, cols])
        o_ref[0, :, cols] = out.astype(o_ref.dtype)


def _nsa_combine(o_cmp, o_slc, o_win, small, tq=512):
    b, s, w = o_cmp.shape
    big = pl.BlockSpec((1, tq, w), lambda bi, i: (bi, i, 0))
    return pl.pallas_call(
        _nsa_combine_kernel,
        out_shape=jax.ShapeDtypeStruct((b, s, w), jnp.bfloat16),
        grid=(b, s // tq),
        in_specs=[big, big, big, pl.BlockSpec((1, tq, small.shape[2]), lambda bi, i: (bi, i, 0))],
        out_specs=big,
        compiler_params=_cparams(("parallel", "parallel")),
        name="nsa_combine",
    )(o_cmp, o_slc, o_win, small)


def _dil_combine_kernel(o1, o2, o3, l1, l2, l3, o_ref):
    a, bb, c = l1[0, 0], l2[0, 0], l3[0, 0]
    m = jnp.maximum(jnp.maximum(a, bb), c)
    ea, eb, ec = jnp.exp2(a - m), jnp.exp2(bb - m), jnp.exp2(c - m)
    tot = ea + eb + ec
    out = o1[0, 0] * (ea / tot) + o2[0, 0] * (eb / tot) + o3[0, 0] * (ec / tot)
    o_ref[0] = out.astype(o_ref.dtype)


def _dil_combine(outs, lses, tq=1024):
    b, h, s, d = outs[0].shape
    spec = pl.BlockSpec((1, 1, tq, d), lambda bi, hh, i: (bi, hh, i, 0))
    return pl.pallas_call(
        _dil_combine_kernel,
        out_shape=jax.ShapeDtypeStruct((b, s, h * d), jnp.bfloat16),
        grid=(b, h, s // tq),
        in_specs=[spec] * 6,
        out_specs=pl.BlockSpec((1, tq, d), lambda bi, hh, i: (bi, i, hh)),
        compiler_params=_cparams(("parallel", "parallel", "parallel")),
        name="dil_combine",
    )(*outs, *lses)


def _to_residue_major(t, dil):
    b, h, s, d = t.shape
    return t.reshape(b, h, s // dil, dil, d).transpose(0, 1, 3, 2, 4).reshape(b, h, s, d)


def _from_residue_major(t, dil):
    b, h, s, d = t.shape
    return t.reshape(b, h, dil, s // dil, d).transpose(0, 1, 3, 2, 4).reshape(b, h, s, d)


def _cmp_to_slc_matrix(n_chunks, n_cmp, n_slc):
    ratio = SLC_LEN // CMP_STRIDE
    span = CMP_LEN // CMP_STRIDE
    jj, mm, nn = np.meshgrid(np.arange(n_slc), np.arange(ratio), np.arange(span), indexing="ij")
    cc = ratio * jj + mm + nn
    keep = cc < n_cmp
    mat = np.zeros((n_chunks, 2 * n_slc), np.float32)
    np.add.at(mat, (cc[keep], jj[keep]), 1.0)
    return jnp.asarray(mat)


def _mixers(hm, small, fox_bf, cmp_pe_k, cmp_w1_k, cmp_w2_k, cmp_pe_v, cmp_w1_v, cmp_w2_v):
    b, _, s, d = hm.shape
    bf16 = jnp.bfloat16

    n_chunks = s // CMP_STRIDE
    n_cmp = (s - CMP_LEN) // CMP_STRIDE + 1
    n_slc = s // SLC_LEN
    chunks = hm[:, HD_KC:HD_KC + 2 * NSA_KV].reshape(b, 2 * NSA_KV, n_chunks, CMP_STRIDE * d)
    cmp_kv = _compress(chunks,
                       jnp.stack([cmp_pe_k, cmp_pe_v]).reshape(2, 1, CMP_LEN * d),
                       jnp.stack([cmp_w1_k, cmp_w1_v]), jnp.stack([cmp_w2_k, cmp_w2_v]))
    o_cmp, sel = _cmp_select(hm, cmp_kv, _cmp_to_slc_matrix(n_chunks, n_cmp, n_slc), tq=256, n_cmp=n_cmp)
    onehot = jnp.asarray((np.arange(s)[:, None] // SLC_LEN) == np.arange(2 * n_slc)[None, :], bf16)
    o_slc = _causal_flash(hm, mode="sel", q_head0=HD_QA, k_head0=HD_KS, v_head0=HD_VS, hb=NSA_HPG, t=512,
                          sel=sel, onehot=onehot, out_dtype=jnp.float32, name="nsa_selected")
    o_win = _band(hm, q_head0=HD_QA, k_head0=HD_KW, v_head0=HD_VW, hb=NSA_HPG, kv_heads=1, t=512, wpad=NSA_WINDOW,
                  window=NSA_WINDOW - 1, seg_len=s, has_lse=False, name="nsa_window")
    o_a = _nsa_combine(o_cmp, o_slc, o_win, small)

    t_fox = 512
    bias_row = jnp.zeros((1, small.shape[2]), jnp.float32).at[0, FORGET_COL0:FORGET_COL0 + H_MIX].set(fox_bf)
    csum = _logf_cumsum(small, bias_row)
    key_bias = (-LOG2E) * csum[:, :, FORGET_COL0:FORGET_COL0 + H_MIX].transpose(0, 2, 1)
    o_b = _causal_flash(hm, mode="fox", q_head0=HD_QB, k_head0=HD_KB, v_head0=HD_VB, hb=4, t=t_fox,
                        bias=key_bias.reshape(b, H_MIX, s // t_fox, 1, t_fox), out_dtype=bf16, name="fox")

    outs, lses = [], []
    for window, dil in DIL_CONFIGS:
        if dil == 1:
            src, heads0 = hm, (HD_QC, HD_KC2, HD_VC2)
        else:
            src, heads0 = _to_residue_major(hm[:, HD_QC:HD_QC + 3 * H_MIX], dil), (0, H_MIX, 2 * H_MIX)
        o, lse = _band(src, q_head0=heads0[0], k_head0=heads0[1], v_head0=heads0[2], hb=4, kv_heads=4, t=256,
                       wpad=HEAD_DIM, window=window // dil, seg_len=s // dil, has_lse=True, name=f"dilated_{dil}")
        if dil != 1:
            o, lse = _from_residue_major(o, dil), _from_residue_major(lse, dil)
        outs.append(o)
        lses.append(lse)
    o_c = _dil_combine(outs, lses)

    o_d = _stickbreak(hm, q_head0=HD_QD, k_head0=HD_KD, v_head0=HD_VD, hb=4, t=256, out_dtype=bf16)

    return jnp.concatenate([o_a, o_b, o_c, o_d], axis=-1)


def _split_w_in(w_in):
    a = H_MIX * HEAD_DIM + 6 * NSA_KV * HEAD_DIM
    g_end = a + 3 * H_MIX
    f0 = g_end + 3 * H_MIX * HEAD_DIM
    f_end = f0 + H_MIX
    big = jnp.concatenate([w_in[..., :a], w_in[..., g_end:f0], w_in[..., f_end:]], axis=-1)
    small = jnp.concatenate([w_in[..., a:g_end], w_in[..., f0:f_end],
                             jnp.zeros(w_in.shape[:-1] + (HEAD_DIM - 4 * H_MIX,), w_in.dtype)], axis=-1)
    return big.astype(jnp.bfloat16), small


def _layer(h, layer, p, cos, sin, batch, norm_attn, w_big, w_small, fox_bf, cmp_pe_k, cmp_w1_k, cmp_w2_k,
           cmp_pe_v, cmp_w1_v, cmp_w2_v, w_o, norm_mlp, w_up, w_down, norm_ple, w_ple_gate, w_ple_proj):
    m, d = h.shape
    s = m // batch
    bf16 = jnp.bfloat16
    xn = _rmsnorm(h, norm_attn, bf16)
    hm = _in_proj_heads(xn, w_big, layer, cos, sin, batch)
    small = _matmul(xn, w_small, layer, jnp.float32, name="in_proj_small")
    mix = _mixers(hm, small.reshape(batch, s, HEAD_DIM), fox_bf,
                  cmp_pe_k, cmp_w1_k, cmp_w2_k, cmp_pe_v, cmp_w1_v, cmp_w2_v)

    tm, tn = 1024, 512
    res_spec = pl.BlockSpec((tm, tn), lambda i, j: (i, j))
    h = _matmul(mix.reshape(m, -1), w_o, layer, jnp.float32, _epi_residual, (h,), (res_spec,), tm=tm, tn=tn,
                name="out_proj")

    x2 = _rmsnorm(h, norm_mlp, bf16)
    mid = _matmul(x2, w_up, layer, bf16, _epi_relu2, tm=tm, tn=tn, name="mlp_up")
    h = _matmul_residual_ksplit(mid, w_down, layer, h, name="mlp_down")

    x3 = _rmsnorm(h, norm_ple, bf16)
    ple_dim = p.shape[2]
    h = _matmul(x3, w_ple_gate, layer, jnp.float32, _epi_ple, (h, p, w_ple_proj),
                (res_spec, pl.BlockSpec((None, tm, ple_dim), lambda i, j: (layer, i, 0)),
                 pl.BlockSpec((None, ple_dim, tn), lambda i, j: (layer, 0, j))),
                tm=tm, tn=tn, name="ple_gate")
    return h


def kernel(x, p, positions, norm_attn, w_in, fox_bf, cmp_pe_k, cmp_w1_k, cmp_w2_k, cmp_pe_v, cmp_w1_v, cmp_w2_v,
           w_o, norm_mlp, w_up, w_down, norm_ple, w_ple_gate, w_ple_proj, norm_final):
    batch, s, d = x.shape
    depth = p.shape[0]
    cos, sin = _rope_tables(positions)
    w_big, w_small = _split_w_in(w_in)
    p = p.reshape(depth, batch * s, -1)
    h = x.reshape(batch * s, d)
    for i in range(depth):
        h = _layer(h, i, p, cos, sin, batch, norm_attn[i], w_big, w_small, fox_bf[i],
                   cmp_pe_k[i], cmp_w1_k[i], cmp_w2_k[i], cmp_pe_v[i], cmp_w1_v[i], cmp_w2_v[i],
                   w_o, norm_mlp[i], w_up, w_down, norm_ple[i], w_ple_gate, w_ple_proj)
    return _rmsnorm(h, norm_final, x.dtype).reshape(batch, s, d)
```

```python
import functools
import math

import numpy as np
import jax
import jax.numpy as jnp
from jax import lax
from jax.experimental import pallas as pl
from jax.experimental.pallas import tpu as pltpu

HEAD_DIM = 128
H_MIX = 8
NSA_KV = 2
NSA_HPG = H_MIX // NSA_KV
CMP_LEN = 32
CMP_STRIDE = 16
SLC_LEN = 64
TOPK = 16
NSA_WINDOW = 512
DIL_CONFIGS = ((128, 1), (512, 4), (2048, 16))
ROPE_THETA = 10000.0
RMS_EPS = 1e-6
NEG = -1e30
M_INIT = -5e29
FORCED_SCORE = 1e9
LOG2E = math.log2(math.e)
Q_SCALE = HEAD_DIM ** -0.5 * LOG2E

VMEM_LIMIT_BYTES = 52 * 1024 * 1024

HD_QA, HD_KC, HD_VC, HD_KS, HD_VS, HD_KW, HD_VW = 0, 8, 10, 12, 14, 16, 18
HD_QB, HD_KB, HD_VB = 20, 28, 36
HD_QC, HD_KC2, HD_VC2 = 44, 52, 60
HD_QD, HD_KD, HD_VD = 68, 76, 84
N_HEADS_ALL = 92
N_BIG = N_HEADS_ALL * HEAD_DIM
GATE_COL0 = 0
FORGET_COL0 = 24
N_GATES = 3 * H_MIX
GATES_AT = (H_MIX + 6 * NSA_KV) * HEAD_DIM
FORGET_AT = GATES_AT + N_GATES + 3 * H_MIX * HEAD_DIM
_ROPED_HEADS = (tuple(range(HD_QA, HD_QA + H_MIX)) + tuple(range(HD_KS, HD_KS + NSA_KV))
                + tuple(range(HD_KW, HD_KW + NSA_KV)) + tuple(range(HD_QC, HD_QC + 2 * H_MIX)))
_QUERY_HEADS = tuple(h for q0 in (HD_QA, HD_QB, HD_QC, HD_QD) for h in range(q0, q0 + H_MIX))


def _cparams(sem):
    return pltpu.CompilerParams(dimension_semantics=sem, vmem_limit_bytes=VMEM_LIMIT_BYTES)


def _dot_nt(a, b):
    return lax.dot_general(a, b, (((1,), (1,)), ((), ())), preferred_element_type=jnp.float32)


def _dot(a, b):
    return jnp.dot(a, b, preferred_element_type=jnp.float32)


def _rmsnorm_kernel(x_ref, g_ref, o_ref):
    x = x_ref[...]
    ms = jnp.mean(x * x, axis=-1, keepdims=True)
    o_ref[...] = (x * lax.rsqrt(ms + RMS_EPS) * g_ref[...]).astype(o_ref.dtype)


def _rmsnorm(x, g, out_dtype, tm=256):
    m, d = x.shape
    return pl.pallas_call(
        _rmsnorm_kernel,
        out_shape=jax.ShapeDtypeStruct((m, d), out_dtype),
        grid=(m // tm,),
        in_specs=[pl.BlockSpec((tm, d), lambda i: (i, 0)),
                  pl.BlockSpec((1, d), lambda i: (0, 0))],
        out_specs=pl.BlockSpec((tm, d), lambda i: (i, 0)),
        compiler_params=_cparams(("parallel",)),
        name="rmsnorm",
    )(x, g.reshape(1, d))


def _mm_kernel(*refs, n_extra, epilogue):
    x_ref, w_ref = refs[0], refs[1]
    extras = refs[2:2 + n_extra]
    o_ref = refs[2 + n_extra]
    acc = _dot(x_ref[...].astype(jnp.bfloat16), w_ref[...].astype(jnp.bfloat16))
    o_ref[...] = epilogue(acc, *extras).astype(o_ref.dtype)


def _epi_none(acc):
    return acc


def _epi_relu2(acc):
    r = jnp.maximum(acc, 0.0)
    return r * r


def _epi_ple(acc, h_ref, p_ref, wp_ref):
    pp = _dot(p_ref[...].astype(jnp.bfloat16), wp_ref[...].astype(jnp.bfloat16))
    return h_ref[...] + jax.nn.sigmoid(acc) * pp


def _matmul(x, w, layer, out_dtype, epilogue=_epi_none, extras=(), extra_specs=(), tm=1024, tn=512, name="matmul"):
    m, kdim = x.shape
    n = w.shape[2]
    tn = min(tn, n)
    return pl.pallas_call(
        functools.partial(_mm_kernel, n_extra=len(extras), epilogue=epilogue),
        out_shape=jax.ShapeDtypeStruct((m, n), out_dtype),
        grid=(m // tm, n // tn),
        in_specs=[pl.BlockSpec((tm, kdim), lambda i, j: (i, 0)),
                  pl.BlockSpec((None, kdim, tn), lambda i, j: (layer, 0, j)),
                  *extra_specs],
        out_specs=pl.BlockSpec((tm, tn), lambda i, j: (i, j)),
        compiler_params=_cparams(("parallel", "arbitrary")),
        name=name,
    )(x, w, *extras)


def _out_proj_kernel(*refs):
    *x_refs, w_ref, h_ref, o_ref = refs
    acc = h_ref[...]
    row = 0
    for x_ref in x_refs:
        width = x_ref.shape[1]
        acc = acc + _dot(x_ref[...], w_ref[row:row + width, :].astype(jnp.bfloat16))
        row += width
    o_ref[...] = acc


def _out_proj(xs, w, layer, h, tm, tn):
    m, n = h.shape
    return pl.pallas_call(
        _out_proj_kernel,
        out_shape=jax.ShapeDtypeStruct((m, n), jnp.float32),
        grid=(m // tm, n // tn),
        in_specs=[*[pl.BlockSpec((tm, x.shape[1]), lambda i, j: (i, 0)) for x in xs],
                  pl.BlockSpec((None, w.shape[1], tn), lambda i, j: (layer, 0, j)),
                  pl.BlockSpec((tm, tn), lambda i, j: (i, j))],
        out_specs=pl.BlockSpec((tm, tn), lambda i, j: (i, j)),
        compiler_params=_cparams(("parallel", "arbitrary")),
        name="out_proj",
    )(*xs, w, h)


def _mm_residual_ksplit_kernel(x_ref, w_ref, h_ref, o_ref):
    @pl.when(pl.program_id(2) == 0)
    def _():
        o_ref[...] = h_ref[...]

    o_ref[...] += _dot(x_ref[...], w_ref[...].astype(jnp.bfloat16))


def _matmul_residual_ksplit(x, w, layer, h, tm=2048, tn=1024, tk=1024, name="matmul_ksplit"):
    m, kdim = x.shape
    n = w.shape[2]
    return pl.pallas_call(
        _mm_residual_ksplit_kernel,
        out_shape=jax.ShapeDtypeStruct((m, n), jnp.float32),
        grid=(m // tm, n // tn, kdim // tk),
        in_specs=[pl.BlockSpec((tm, tk), lambda i, j, k: (i, k)),
                  pl.BlockSpec((None, tk, tn), lambda i, j, k: (layer, k, j)),
                  pl.BlockSpec((tm, tn), lambda i, j, k: (i, j), pipeline_mode=pl.Buffered(1))],
        out_specs=pl.BlockSpec((tm, tn), lambda i, j, k: (i, j)),
        compiler_params=_cparams(("parallel", "parallel", "arbitrary")),
        name=name,
    )(x, w, h)


def _rope_table_kernel(pos_ref, freq_ref, sign_ref, cos_ref, sin_ref):
    ang = pos_ref[...] * freq_ref[...]
    cos_ref[...] = jnp.cos(ang)
    sin_ref[...] = jnp.sin(ang) * sign_ref[...]


def _rope_tables(positions, ts=512):
    n = positions.size
    half = HEAD_DIM // 2
    inv_freq = ROPE_THETA ** (-jnp.arange(half, dtype=jnp.float32) / half)
    freq = jnp.concatenate([inv_freq, inv_freq]).reshape(1, HEAD_DIM)
    sign = jnp.concatenate([-jnp.ones((half,), jnp.float32), jnp.ones((half,), jnp.float32)]).reshape(1, HEAD_DIM)
    pos = positions.astype(jnp.float32).reshape(n, 1)
    row = pl.BlockSpec((1, HEAD_DIM), lambda i: (0, 0))
    return pl.pallas_call(
        _rope_table_kernel,
        out_shape=[jax.ShapeDtypeStruct((n, HEAD_DIM), jnp.float32)] * 2,
        grid=(n // ts,),
        in_specs=[pl.BlockSpec((ts, 1), lambda i: (i, 0)), row, row],
        out_specs=[pl.BlockSpec((ts, HEAD_DIM), lambda i: (i, 0))] * 2,
        compiler_params=_cparams(("parallel",)),
        name="rope_tables",
    )(pos, freq, sign)


_FLAG_ROPED, _FLAG_QUERY = 1, 2


def _in_proj_kernel(flags_ref, row0_ref, x_ref, wt_ref, cos_ref, sin_ref, o_ref):
    j = pl.program_id(1)
    heads = o_ref.shape[1]
    acc = _dot_nt(x_ref[...], wt_ref[0].astype(jnp.bfloat16))
    for hh in range(heads):
        t = acc[:, hh * HEAD_DIM:(hh + 1) * HEAD_DIM]
        flags = flags_ref[j * heads + hh]
        roped = (flags & _FLAG_ROPED) > 0
        scale = jnp.where((flags & _FLAG_QUERY) > 0, Q_SCALE, 1.0)
        cos = jnp.where(roped, cos_ref[...], 1.0)
        sin = jnp.where(roped, sin_ref[...], 0.0)
        o_ref[0, hh] = ((t * cos + pltpu.roll(t, HEAD_DIM // 2, 1) * sin) * scale).astype(o_ref.dtype)


def _in_proj_heads(xn, w_in_t, layer, cos, sin, batch, tm=1024, tn=512):
    m, d = xn.shape
    s = m // batch
    heads = tn // HEAD_DIM
    q_tiles = s // tm
    flags = np.zeros((N_HEADS_ALL,), np.int32)
    flags[list(_ROPED_HEADS)] |= _FLAG_ROPED
    flags[list(_QUERY_HEADS)] |= _FLAG_QUERY
    col = np.arange(0, N_BIG, tn)
    row0 = col + np.where(col >= GATES_AT, N_GATES, 0) + np.where(col >= FORGET_AT - N_GATES, H_MIX, 0)
    table = pl.BlockSpec((tm, HEAD_DIM), lambda i, j, f, r: (i, 0))
    return pl.pallas_call(
        _in_proj_kernel,
        out_shape=jax.ShapeDtypeStruct((batch, N_HEADS_ALL, s, HEAD_DIM), jnp.bfloat16),
        grid_spec=pltpu.PrefetchScalarGridSpec(
            num_scalar_prefetch=2,
            grid=(m // tm, N_BIG // tn),
            in_specs=[pl.BlockSpec((tm, d), lambda i, j, f, r: (i, 0)),
                      pl.BlockSpec((pl.Element(1), pl.Element(tn), pl.Element(d)),
                                   lambda i, j, f, r: (layer, pl.multiple_of(r[j], 8), 0)),
                      table, table],
            out_specs=pl.BlockSpec((1, heads, tm, HEAD_DIM),
                                   lambda i, j, f, r: (i // q_tiles, j, i % q_tiles, 0)),
        ),
        compiler_params=_cparams(("parallel", "arbitrary")),
        name="in_proj",
    )(jnp.asarray(flags), jnp.asarray(row0.astype(np.int32)), xn, w_in_t, cos, sin)


def _in_proj_small_kernel(x_ref, wt_ref, o_ref):
    o_ref[...] = _dot_nt(x_ref[...], wt_ref[...].astype(jnp.bfloat16))


def _in_proj_small(xn, w_small_t, layer, tm=1024):
    m, d = xn.shape
    n = w_small_t.shape[1]
    return pl.pallas_call(
        _in_proj_small_kernel,
        out_shape=jax.ShapeDtypeStruct((m, n), jnp.float32),
        grid=(m // tm,),
        in_specs=[pl.BlockSpec((tm, d), lambda i: (i, 0)),
                  pl.BlockSpec((None, n, d), lambda i: (layer, 0, 0))],
        out_specs=pl.BlockSpec((tm, n), lambda i: (i, 0)),
        compiler_params=_cparams(("parallel",)),
        name="in_proj_small",
    )(xn, w_small_t)


def _softplus(z):
    return jnp.maximum(z, 0.0) + jnp.log1p(jnp.exp(-jnp.abs(z)))


def _logf_cumsum_kernel(x_ref, bias_ref, o_ref, *, blk):
    s = x_ref.shape[1]
    r = lax.broadcasted_iota(jnp.int32, (blk, blk), 0)
    c = lax.broadcasted_iota(jnp.int32, (blk, blk), 1)
    tri = jnp.where(c <= r, 1.0, 0.0).astype(jnp.float32)

    def body(i, carry):
        x = x_ref[0, pl.ds(i * blk, blk), :]
        logf = -_softplus(-(x + bias_ref[...]))
        cs = jnp.dot(tri, logf, preferred_element_type=jnp.float32,
                     precision=lax.Precision.HIGHEST) + carry
        o_ref[0, pl.ds(i * blk, blk), :] = cs
        return cs[blk - 1:blk, :]

    lax.fori_loop(0, s // blk, body, jnp.zeros((1, x_ref.shape[2]), jnp.float32))


def _logf_cumsum(small, bias_row, blk=128):
    b, s, w = small.shape
    return pl.pallas_call(
        functools.partial(_logf_cumsum_kernel, blk=blk),
        out_shape=jax.ShapeDtypeStruct((b, s, w), jnp.float32),
        grid=(b,),
        in_specs=[pl.BlockSpec((1, s, w), lambda bi: (bi, 0, 0)),
                  pl.BlockSpec((1, w), lambda bi: (0, 0))],
        out_specs=pl.BlockSpec((1, s, w), lambda bi: (bi, 0, 0)),
        compiler_params=_cparams(("parallel",)),
        name="logf_cumsum",
    )(small, bias_row)


def _rowmax_lanes(s):
    m = s[:, :HEAD_DIM]
    for c in range(1, s.shape[1] // HEAD_DIM):
        m = jnp.maximum(m, s[:, c * HEAD_DIM:(c + 1) * HEAD_DIM])
    return jnp.broadcast_to(jnp.max(m, axis=-1, keepdims=True), m.shape)


def _fold_lanes(p):
    a = p[:, :HEAD_DIM]
    for c in range(1, p.shape[1] // HEAD_DIM):
        a = a + p[:, c * HEAD_DIM:(c + 1) * HEAD_DIM]
    return a


def _tile_lanes(x, width):
    return jnp.concatenate([x] * (width // HEAD_DIM), axis=1)


def _causal_flash_kernel(*refs, mode, hb, t):
    it = iter(refs)
    q_ref, k_ref, v_ref = next(it), next(it), next(it)
    bias_ref = next(it) if mode == "fox" else None
    sel_ref, onehot_ref = (next(it), next(it)) if mode == "sel" else (None, None)
    o_ref = next(it)
    m_scr, l_scr, acc_scr = next(it), next(it), next(it)
    i = pl.program_id(2)

    m_scr[...] = jnp.full(m_scr.shape, M_INIT, jnp.float32)
    l_scr[...] = jnp.zeros(l_scr.shape, jnp.float32)
    acc_scr[...] = jnp.zeros(acc_scr.shape, jnp.float32)

    def tile(j, diagonal):
        keys = pl.ds(pl.multiple_of(j * t, t), t)
        if mode == "sel":
            k_shared = jnp.concatenate([k_ref[0, 0, keys, :], onehot_ref[keys, :]], axis=1)
            scores = [_dot_nt(jnp.concatenate([q_ref[0, hh], sel_ref[0, 0]], axis=1), k_shared)
                      for hh in range(hb)]
        else:
            scores = [_dot_nt(q_ref[0, hh], k_ref[0, hh, keys, :]) + bias_ref[0, hh, j] for hh in range(hb)]
        if diagonal:
            causal = (lax.broadcasted_iota(jnp.int32, (t, t), 1) <= lax.broadcasted_iota(jnp.int32, (t, t), 0))
            scores = [jnp.where(causal, s, NEG) for s in scores]
        probs = []
        for hh, s in enumerate(scores):
            m_prev = m_scr[hh]
            m_new = jnp.maximum(m_prev, _rowmax_lanes(s))
            alpha = jnp.exp2(m_prev - m_new)
            p = jnp.exp2(s - _tile_lanes(m_new, t))
            l_scr[hh] = alpha * l_scr[hh] + _fold_lanes(p)
            m_scr[hh] = m_new
            probs.append((alpha, p.astype(v_ref.dtype)))
        for hh, (alpha, p) in enumerate(probs):
            v = v_ref[0, 0 if mode == "sel" else hh, keys, :]
            acc_scr[hh] = alpha * acc_scr[hh] + _dot(p, v)

    def off_diagonal(j, carry):
        tile(j, False)
        return carry

    lax.fori_loop(0, i, off_diagonal, 0)
    tile(i, True)
    for hh in range(hb):
        l = jnp.sum(l_scr[hh], axis=-1, keepdims=True)
        o_ref[0, :, hh * HEAD_DIM:(hh + 1) * HEAD_DIM] = (acc_scr[hh] / l).astype(o_ref.dtype)


def _causal_flash(hm, *, mode, q_head0, k_head0, v_head0, hb, t, out_dtype, bias=None, sel=None, onehot=None,
                  name="causal_flash"):
    b, _, s, d = hm.shape
    groups = H_MIX // hb
    kv_heads = hb if mode == "fox" else 1
    in_specs = [pl.BlockSpec((1, hb, t, d), lambda bi, g, i: (bi, q_head0 // hb + g, i, 0)),
                pl.BlockSpec((1, kv_heads, s, d), lambda bi, g, i: (bi, k_head0 // kv_heads + g, 0, 0)),
                pl.BlockSpec((1, kv_heads, s, d), lambda bi, g, i: (bi, v_head0 // kv_heads + g, 0, 0))]
    args = [hm, hm, hm]
    if mode == "fox":
        in_specs.append(pl.BlockSpec((1, hb, s // t, 1, t), lambda bi, g, i: (bi, g, 0, 0, 0)))
        args.append(bias)
    else:
        in_specs.append(pl.BlockSpec((1, 1, t, sel.shape[-1]), lambda bi, g, i: (bi, g, i, 0)))
        in_specs.append(pl.BlockSpec(onehot.shape, lambda bi, g, i: (0, 0)))
        args += [sel, onehot]
    return pl.pallas_call(
        functools.partial(_causal_flash_kernel, mode=mode, hb=hb, t=t),
        out_shape=jax.ShapeDtypeStruct((b, s, H_MIX * d), out_dtype),
        grid=(b, groups, s // t),
        in_specs=in_specs,
        out_specs=pl.BlockSpec((1, t, hb * d), lambda bi, g, i: (bi, i, g)),
        scratch_shapes=[pltpu.VMEM((hb, t, d), jnp.float32)] * 3,
        compiler_params=_cparams(("parallel", "parallel", "arbitrary")),
        name=name,
    )(*args)


def _band_kernel(*refs, hb, kv_heads, t, wpad, window, seg_len, has_lse):
    q_ref, k_ref, v_ref, o_ref = refs[:4]
    lse_ref = refs[4] if has_lse else None
    s_len = k_ref.shape[2]
    span = t + wpad
    t0 = pl.program_id(2) * t
    seg0 = (t0 // seg_len) * seg_len
    start = jnp.minimum(jnp.maximum(t0 - wpad, seg0), s_len - span)
    start = pl.multiple_of(start, HEAD_DIM)
    keys = pl.ds(start, span)
    qpos = t0 + lax.broadcasted_iota(jnp.int32, (t, span), 0)
    kpos = start + lax.broadcasted_iota(jnp.int32, (t, span), 1)
    dist = qpos - kpos
    mask = (dist >= 0) & (dist <= window) & (kpos >= seg0)
    kv_of = [hh * kv_heads // hb for hh in range(hb)]
    scores = [jnp.where(mask, _dot_nt(q_ref[0, hh], k_ref[0, kv_of[hh], keys, :]), NEG) for hh in range(hb)]
    probs = []
    for s in scores:
        m = _rowmax_lanes(s)
        p = jnp.exp2(s - _tile_lanes(m, span))
        l = jnp.broadcast_to(jnp.sum(_fold_lanes(p), axis=-1, keepdims=True), m.shape)
        probs.append((m, l, p.astype(v_ref.dtype)))
    for hh, (m, l, p) in enumerate(probs):
        out = _dot(p, v_ref[0, kv_of[hh], keys, :]) / l
        if has_lse:
            o_ref[0, hh] = out.astype(o_ref.dtype)
            lse_ref[0, hh] = m + jnp.log2(l)
        else:
            o_ref[0, :, hh * HEAD_DIM:(hh + 1) * HEAD_DIM] = out.astype(o_ref.dtype)


def _band(src, *, q_head0, k_head0, v_head0, hb, kv_heads, t, wpad, window, seg_len, has_lse, name):
    b, _, s, d = src.shape
    groups = H_MIX // hb
    in_specs = [pl.BlockSpec((1, hb, t, d), lambda bi, g, i: (bi, q_head0 // hb + g, i, 0)),
                pl.BlockSpec((1, kv_heads, s, d), lambda bi, g, i: (bi, k_head0 // kv_heads + g, 0, 0)),
                pl.BlockSpec((1, kv_heads, s, d), lambda bi, g, i: (bi, v_head0 // kv_heads + g, 0, 0))]
    if has_lse:
        hm_spec = pl.BlockSpec((1, hb, t, d), lambda bi, g, i: (bi, g, i, 0))
        out_shape = [jax.ShapeDtypeStruct((b, H_MIX, s, d), jnp.float32)] * 2
        out_specs = [hm_spec, hm_spec]
    else:
        out_shape = jax.ShapeDtypeStruct((b, s, H_MIX * d), jnp.float32)
        out_specs = pl.BlockSpec((1, t, hb * d), lambda bi, g, i: (bi, i, g))
    return pl.pallas_call(
        functools.partial(_band_kernel, hb=hb, kv_heads=kv_heads, t=t, wpad=wpad, window=window,
                          seg_len=seg_len, has_lse=has_lse),
        out_shape=out_shape,
        grid=(b, groups, s // t),
        in_specs=in_specs,
        out_specs=out_specs,
        compiler_params=_cparams(("parallel", "parallel", "arbitrary")),
        name=name,
    )(src, src, src)


def _stickbreak_kernel(q_ref, k_ref, v_ref, u2_ref, o_ref, run_scr, acc_scr, *, hb, t):
    i = pl.program_id(2)
    run_scr[...] = jnp.zeros(run_scr.shape, jnp.float32)
    acc_scr[...] = jnp.zeros(acc_scr.shape, jnp.float32)
    u2 = u2_ref[...]
    sign_bit = jnp.uint32(0x80000000)

    def tile(j, diagonal):
        keys = pl.ds(pl.multiple_of(j * t, t), t)
        if diagonal:
            strict = (lax.broadcasted_iota(jnp.int32, (t, t), 1) < lax.broadcasted_iota(jnp.int32, (t, t), 0))
        zs = [_dot_nt(q_ref[0, hh], k_ref[0, hh, keys, :]) for hh in range(hb)]
        stage = []
        for z2 in zs:
            neg_abs = lax.bitcast_convert_type(lax.bitcast_convert_type(z2, jnp.uint32) | sign_bit, jnp.float32)
            lg = jnp.log2(1.0 + jnp.exp2(neg_abs))
            sp2 = jnp.maximum(z2, 0.0) + lg
            if diagonal:
                sp2 = jnp.where(strict, sp2, 0.0)
            hi = sp2.astype(jnp.bfloat16)
            lo = (sp2 - hi.astype(jnp.float32)).astype(jnp.bfloat16)
            stage.append((jnp.minimum(z2, 0.0) - lg, sp2[:, 0:1], jnp.concatenate([hi, lo], axis=1)))
        laters = [_dot(hl, u2) for (_, _, hl) in stage]
        weights = []
        for hh, later in enumerate(laters):
            log_beta, first, _ = stage[hh]
            run = run_scr[hh]
            a = jnp.exp2(log_beta - (later + _tile_lanes(run, t)))
            if diagonal:
                a = jnp.where(strict, a, 0.0)
            run_scr[hh] = run + jnp.broadcast_to(later[:, 0:1] + first, run.shape)
            weights.append(a.astype(v_ref.dtype))
        for hh, a in enumerate(weights):
            acc_scr[hh] += _dot(a, v_ref[0, hh, keys, :])

    def off_diagonal(n, carry):
        tile(i - 1 - n, False)
        return carry

    tile(i, True)
    lax.fori_loop(0, i, off_diagonal, 0)
    for hh in range(hb):
        o_ref[0, :, hh * HEAD_DIM:(hh + 1) * HEAD_DIM] = acc_scr[hh].astype(o_ref.dtype)


def _stickbreak(hm, *, q_head0, k_head0, v_head0, hb, t, out_dtype):
    b, _, s, d = hm.shape
    later = np.arange(t)[:, None] > np.arange(t)[None, :]
    u2 = jnp.asarray(np.concatenate([later, later], axis=0), jnp.bfloat16)

    def heads(h0):
        return pl.BlockSpec((1, hb, s, d), lambda bi, g, i: (bi, h0 // hb + g, 0, 0))

    return pl.pallas_call(
        functools.partial(_stickbreak_kernel, hb=hb, t=t),
        out_shape=jax.ShapeDtypeStruct((b, s, H_MIX * d), out_dtype),
        grid=(b, H_MIX // hb, s // t),
        in_specs=[pl.BlockSpec((1, hb, t, d), lambda bi, g, i: (bi, q_head0 // hb + g, i, 0)),
                  heads(k_head0), heads(v_head0),
                  pl.BlockSpec((2 * t, t), lambda bi, g, i: (0, 0))],
        out_specs=pl.BlockSpec((1, t, hb * d), lambda bi, g, i: (bi, i, g)),
        scratch_shapes=[pltpu.VMEM((hb, t, d), jnp.float32), pltpu.VMEM((hb, t, d), jnp.float32)],
        compiler_params=_cparams(("parallel", "parallel", "arbitrary")),
        name="stickbreak",
    )(hm, hm, hm, u2)


def _gelu_tanh(x):
    return 0.5 * x * (1.0 + jnp.tanh(math.sqrt(2.0 / math.pi) * (x + 0.044715 * (x * x * x))))


def _compress_kernel(x_ref, pe_ref, w1_ref, w2_ref, o_ref):
    half = x_ref.shape[3]
    x = x_ref[0, 0]
    w1 = w1_ref[0].astype(jnp.bfloat16)
    first = _dot(x, w1[:half])
    second = _dot(x, w1[half:])
    pe = jnp.broadcast_to(pe_ref[0], (8, pe_ref.shape[2])).astype(jnp.bfloat16)
    pe_term = _dot(pe, w1)[0:1]
    n_chunks = x.shape[0]
    hid = first + pltpu.roll(second, n_chunks - 1, 0) + pe_term
    act = _gelu_tanh(hid)
    o_ref[0, 0] = _dot(act.astype(jnp.bfloat16), w2_ref[0].astype(jnp.bfloat16))


def _compress(chunks, pe, w1, w2):
    b, nh, nc, half = chunks.shape
    return pl.pallas_call(
        _compress_kernel,
        out_shape=jax.ShapeDtypeStruct((b, nh, nc, HEAD_DIM), jnp.float32),
        grid=(b, nh),
        in_specs=[pl.BlockSpec((1, 1, nc, half), lambda bi, h: (bi, h, 0, 0)),
                  pl.BlockSpec((1, 1, 2 * half), lambda bi, h: (h // NSA_KV, 0, 0)),
                  pl.BlockSpec((1, 2 * half, w1.shape[2]), lambda bi, h: (h // NSA_KV, 0, 0)),
                  pl.BlockSpec((1, w2.shape[1], HEAD_DIM), lambda bi, h: (h // NSA_KV, 0, 0))],
        out_specs=pl.BlockSpec((1, 1, nc, HEAD_DIM), lambda bi, h: (bi, h, 0, 0)),
        compiler_params=_cparams(("parallel", "parallel")),
        name="nsa_compress",
    )(chunks, pe, w1, w2)


def _cmp_select_kernel(q_ref, kc_ref, vc_ref, c2s_ref, o_ref, sel_ref, *, tq, n_cmp):
    i = pl.program_id(2)
    nc = kc_ref.shape[2]
    kc = kc_ref[0, 0].astype(jnp.bfloat16)
    vc = vc_ref[0, 0].astype(jnp.bfloat16)
    t = i * tq + lax.broadcasted_iota(jnp.int32, (tq, 1), 0)
    cidx = lax.broadcasted_iota(jnp.int32, (1, nc), 1)
    visible = (cidx * CMP_STRIDE + (CMP_LEN - 1) <= t) & (cidx < n_cmp)
    psum = jnp.zeros((tq, nc), jnp.float32)
    for hh in range(NSA_HPG):
        s = jnp.where(visible, _dot_nt(q_ref[0, hh], kc), NEG)
        m = jnp.max(s, axis=-1, keepdims=True)
        e = jnp.where(visible, jnp.exp2(s - m), 0.0)
        l = jnp.sum(e, axis=-1, keepdims=True)
        p = e / jnp.where(l > 0.0, l, 1.0)
        psum = psum + p
        o_ref[0, :, hh * HEAD_DIM:(hh + 1) * HEAD_DIM] = _dot(p.astype(jnp.bfloat16), vc)

    imp = jnp.dot(psum, c2s_ref[...], preferred_element_type=jnp.float32, precision=lax.Precision.HIGHEST)
    w = imp.shape[1]
    n_slc = w // 2
    jblk = lax.broadcasted_iota(jnp.int32, (1, w), 1)
    cur = lax.shift_right_logical(t, int(math.log2(SLC_LEN)))
    forced = (jblk == 0) | (jblk == cur) | (jblk == cur - 1)
    causal_blk = jblk * SLC_LEN <= t
    score = jnp.where(causal_blk, jnp.where(forced, FORCED_SCORE, imp), -1.0)
    rank = jnp.zeros((tq, w), jnp.float32)
    for jp in range(n_slc):
        col = score[:, jp:jp + 1]
        before = (col > score) | ((col == score) & (jblk > jp))
        rank = rank + jnp.where(before, 1.0, 0.0)
    chosen = (rank < float(min(TOPK, n_slc))) & causal_blk
    sel_ref[0, 0] = jnp.where(chosen | (jblk >= n_slc), 0.0, NEG).astype(sel_ref.dtype)


def _cmp_select(hm, cmp_kv, c2s, *, tq, n_cmp):
    b, _, s, d = hm.shape
    nc = cmp_kv.shape[2]
    w = c2s.shape[1]
    return pl.pallas_call(
        functools.partial(_cmp_select_kernel, tq=tq, n_cmp=n_cmp),
        out_shape=[jax.ShapeDtypeStruct((b, s, H_MIX * d), jnp.float32),
                   jax.ShapeDtypeStruct((b, NSA_KV, s, w), jnp.bfloat16)],
        grid=(b, NSA_KV, s // tq),
        in_specs=[pl.BlockSpec((1, NSA_HPG, tq, d), lambda bi, g, i: (bi, g, i, 0)),
                  pl.BlockSpec((1, 1, nc, d), lambda bi, g, i: (bi, g, 0, 0)),
                  pl.BlockSpec((1, 1, nc, d), lambda bi, g, i: (bi, NSA_KV + g, 0, 0)),
                  pl.BlockSpec((nc, w), lambda bi, g, i: (0, 0))],
        out_specs=[pl.BlockSpec((1, tq, NSA_HPG * d), lambda bi, g, i: (bi, i, g)),
                   pl.BlockSpec((1, 1, tq, w), lambda bi, g, i: (bi, g, i, 0))],
        compiler_params=_cparams(("parallel", "parallel", "parallel")),
        name="nsa_cmp_select",
    )(hm, cmp_kv, cmp_kv, c2s)


def _nsa_combine_kernel(oc_ref, os_ref, ow_ref, g_ref, o_ref):
    gates = jax.nn.sigmoid(g_ref[0])
    for h in range(H_MIX):
        cols = slice(h * HEAD_DIM, (h + 1) * HEAD_DIM)
        c0 = GATE_COL0 + 3 * h
        out = (gates[:, c0:c0 + 1] * oc_ref[0, :, cols]
               + gates[:, c0 + 1:c0 + 2] * os_ref[0, :, cols]
               + gates[:, c0 + 2:c0 + 3] * ow_ref[0, :, cols])
        o_ref[0, :, cols] = out.astype(o_ref.dtype)


def _nsa_combine(o_cmp, o_slc, o_win, small, tq=512):
    b, s, w = o_cmp.shape
    big = pl.BlockSpec((1, tq, w), lambda bi, i: (bi, i, 0))
    return pl.pallas_call(
        _nsa_combine_kernel,
        out_shape=jax.ShapeDtypeStruct((b, s, w), jnp.bfloat16),
        grid=(b, s // tq),
        in_specs=[big, big, big, pl.BlockSpec((1, tq, small.shape[2]), lambda bi, i: (bi, i, 0))],
        out_specs=big,
        compiler_params=_cparams(("parallel", "parallel")),
        name="nsa_combine",
    )(o_cmp, o_slc, o_win, small)


def _dil_combine_kernel(o1, o2, o3, l1, l2, l3, o_ref):
    a, bb, c = l1[0, 0], l2[0, 0], l3[0, 0]
    m = jnp.maximum(jnp.maximum(a, bb), c)
    ea, eb, ec = jnp.exp2(a - m), jnp.exp2(bb - m), jnp.exp2(c - m)
    tot = ea + eb + ec
    out = o1[0, 0] * (ea / tot) + o2[0, 0] * (eb / tot) + o3[0, 0] * (ec / tot)
    o_ref[0] = out.astype(o_ref.dtype)


def _dil_combine(outs, lses, tq=1024):
    b, h, s, d = outs[0].shape
    spec = pl.BlockSpec((1, 1, tq, d), lambda bi, hh, i: (bi, hh, i, 0))
    return pl.pallas_call(
        _dil_combine_kernel,
        out_shape=jax.ShapeDtypeStruct((b, s, h * d), jnp.bfloat16),
        grid=(b, h, s // tq),
        in_specs=[spec] * 6,
        out_specs=pl.BlockSpec((1, tq, d), lambda bi, hh, i: (bi, i, hh)),
        compiler_params=_cparams(("parallel", "parallel", "parallel")),
        name="dil_combine",
    )(*outs, *lses)


def _to_residue_major(t, dil):
    b, h, s, d = t.shape
    return t.reshape(b, h, s // dil, dil, d).transpose(0, 1, 3, 2, 4).reshape(b, h, s, d)


def _from_residue_major(t, dil):
    b, h, s, d = t.shape
    return t.reshape(b, h, dil, s // dil, d).transpose(0, 1, 3, 2, 4).reshape(b, h, s, d)


def _cmp_to_slc_matrix(n_chunks, n_cmp, n_slc):
    ratio = SLC_LEN // CMP_STRIDE
    span = CMP_LEN // CMP_STRIDE
    jj, mm, nn = np.meshgrid(np.arange(n_slc), np.arange(ratio), np.arange(span), indexing="ij")
    cc = ratio * jj + mm + nn
    keep = cc < n_cmp
    mat = np.zeros((n_chunks, 2 * n_slc), np.float32)
    np.add.at(mat, (cc[keep], jj[keep]), 1.0)
    return jnp.asarray(mat)


def _mixers(hm, small, fox_bf, cmp_pe_k, cmp_w1_k, cmp_w2_k, cmp_pe_v, cmp_w1_v, cmp_w2_v):
    b, _, s, d = hm.shape
    bf16 = jnp.bfloat16

    n_chunks = s // CMP_STRIDE
    n_cmp = (s - CMP_LEN) // CMP_STRIDE + 1
    n_slc = s // SLC_LEN
    chunks = hm[:, HD_KC:HD_KC + 2 * NSA_KV].reshape(b, 2 * NSA_KV, n_chunks, CMP_STRIDE * d)
    cmp_kv = _compress(chunks,
                       jnp.stack([cmp_pe_k, cmp_pe_v]).reshape(2, 1, CMP_LEN * d),
                       jnp.stack([cmp_w1_k, cmp_w1_v]), jnp.stack([cmp_w2_k, cmp_w2_v]))
    o_cmp, sel = _cmp_select(hm, cmp_kv, _cmp_to_slc_matrix(n_chunks, n_cmp, n_slc), tq=256, n_cmp=n_cmp)
    onehot = jnp.asarray((np.arange(s)[:, None] // SLC_LEN) == np.arange(2 * n_slc)[None, :], bf16)
    o_slc = _causal_flash(hm, mode="sel", q_head0=HD_QA, k_head0=HD_KS, v_head0=HD_VS, hb=NSA_HPG, t=512,
                          sel=sel, onehot=onehot, out_dtype=jnp.float32, name="nsa_selected")
    o_win = _band(hm, q_head0=HD_QA, k_head0=HD_KW, v_head0=HD_VW, hb=NSA_HPG, kv_heads=1, t=512, wpad=NSA_WINDOW,
                  window=NSA_WINDOW - 1, seg_len=s, has_lse=False, name="nsa_window")
    o_a = _nsa_combine(o_cmp, o_slc, o_win, small)

    t_fox = 512
    bias_row = jnp.zeros((1, small.shape[2]), jnp.float32).at[0, FORGET_COL0:FORGET_COL0 + H_MIX].set(fox_bf)
    csum = _logf_cumsum(small, bias_row)
    key_bias = (-LOG2E) * csum[:, :, FORGET_COL0:FORGET_COL0 + H_MIX].transpose(0, 2, 1)
    o_b = _causal_flash(hm, mode="fox", q_head0=HD_QB, k_head0=HD_KB, v_head0=HD_VB, hb=4, t=t_fox,
                        bias=key_bias.reshape(b, H_MIX, s // t_fox, 1, t_fox), out_dtype=bf16, name="fox")

    outs, lses = [], []
    for window, dil in DIL_CONFIGS:
        if dil == 1:
            src, heads0 = hm, (HD_QC, HD_KC2, HD_VC2)
        else:
            src, heads0 = _to_residue_major(hm[:, HD_QC:HD_QC + 3 * H_MIX], dil), (0, H_MIX, 2 * H_MIX)
        o, lse = _band(src, q_head0=heads0[0], k_head0=heads0[1], v_head0=heads0[2], hb=4, kv_heads=4, t=256,
                       wpad=HEAD_DIM, window=window // dil, seg_len=s // dil, has_lse=True, name=f"dilated_{dil}")
        if dil != 1:
            o, lse = _from_residue_major(o, dil), _from_residue_major(lse, dil)
        outs.append(o)
        lses.append(lse)
    o_c = _dil_combine(outs, lses)

    o_d = _stickbreak(hm, q_head0=HD_QD, k_head0=HD_KD, v_head0=HD_VD, hb=4, t=256, out_dtype=bf16)

    return o_a, o_b, o_c, o_d


def _small_w_in_t(w_in_t):
    pad = jnp.zeros((w_in_t.shape[0], HEAD_DIM - N_GATES - H_MIX, w_in_t.shape[2]), w_in_t.dtype)
    return jnp.concatenate([w_in_t[:, GATES_AT:GATES_AT + N_GATES], w_in_t[:, FORGET_AT:FORGET_AT + H_MIX], pad],
                           axis=1)


def _layer(h, layer, p, cos, sin, batch, norm_attn, w_in_t, w_small_t, fox_bf, cmp_pe_k, cmp_w1_k, cmp_w2_k,
           cmp_pe_v, cmp_w1_v, cmp_w2_v, w_o, norm_mlp, w_up, w_down, norm_ple, w_ple_gate, w_ple_proj):
    m, d = h.shape
    s = m // batch
    bf16 = jnp.bfloat16
    xn = _rmsnorm(h, norm_attn, bf16)
    hm = _in_proj_heads(xn, w_in_t, layer, cos, sin, batch)
    small = _in_proj_small(xn, w_small_t, layer)
    mix = _mixers(hm, small.reshape(batch, s, HEAD_DIM), fox_bf,
                  cmp_pe_k, cmp_w1_k, cmp_w2_k, cmp_pe_v, cmp_w1_v, cmp_w2_v)

    tm, tn = 1024, 512
    res_spec = pl.BlockSpec((tm, tn), lambda i, j: (i, j))
    h = _out_proj([o.reshape(m, -1) for o in mix], w_o, layer, h, tm=tm, tn=tn)

    x2 = _rmsnorm(h, norm_mlp, bf16)
    mid = _matmul(x2, w_up, layer, bf16, _epi_relu2, tm=tm, tn=tn, name="mlp_up")
    h = _matmul_residual_ksplit(mid, w_down, layer, h, name="mlp_down")

    x3 = _rmsnorm(h, norm_ple, bf16)
    ple_dim = p.shape[2]
    h = _matmul(x3, w_ple_gate, layer, jnp.float32, _epi_ple, (h, p, w_ple_proj),
                (res_spec, pl.BlockSpec((None, tm, ple_dim), lambda i, j: (layer, i, 0)),
                 pl.BlockSpec((None, ple_dim, tn), lambda i, j: (layer, 0, j))),
                tm=tm, tn=tn, name="ple_gate")
    return h


def kernel(x, p, positions, norm_attn, w_in, fox_bf, cmp_pe_k, cmp_w1_k, cmp_w2_k, cmp_pe_v, cmp_w1_v, cmp_w2_v,
           w_o, norm_mlp, w_up, w_down, norm_ple, w_ple_gate, w_ple_proj, norm_final):
    batch, s, d = x.shape
    depth = p.shape[0]
    cos, sin = _rope_tables(positions)
    w_in_t = jnp.swapaxes(w_in, 1, 2)
    w_small_t = _small_w_in_t(w_in_t)
    p = p.reshape(depth, batch * s, -1)
    h = x.reshape(batch * s, d)
    for i in range(depth):
        h = _layer(h, i, p, cos, sin, batch, norm_attn[i], w_in_t, w_small_t, fox_bf[i],
                   cmp_pe_k[i], cmp_w1_k[i], cmp_w2_k[i], cmp_pe_v[i], cmp_w1_v[i], cmp_w2_v[i],
                   w_o, norm_mlp[i], w_up, w_down, norm_ple[i], w_ple_gate, w_ple_proj)
    return _rmsnorm(h, norm_final, x.dtype).reshape(batch, s, d)
```

```python
import functools
import math

import numpy as np
import jax
import jax.numpy as jnp
from jax import lax
from jax.experimental import pallas as pl
from jax.experimental.pallas import tpu as pltpu

HEAD_DIM = 128
H_MIX = 8
NSA_KV = 2
NSA_HPG = H_MIX // NSA_KV
CMP_LEN = 32
CMP_STRIDE = 16
SLC_LEN = 64
TOPK = 16
NSA_WINDOW = 512
DIL_CONFIGS = ((128, 1), (512, 4), (2048, 16))
ROPE_THETA = 10000.0
RMS_EPS = 1e-6
NEG = -1e30
M_INIT = -5e29
FORCED_SCORE = 1e9
LOG2E = math.log2(math.e)
Q_SCALE = HEAD_DIM ** -0.5 * LOG2E

VMEM_LIMIT_BYTES = 52 * 1024 * 1024

HD_QA, HD_KC, HD_VC, HD_KS, HD_VS, HD_KW, HD_VW = 0, 8, 10, 12, 14, 16, 18
HD_QB, HD_KB, HD_VB = 20, 28, 36
HD_QC, HD_KC2, HD_VC2 = 44, 52, 60
HD_QD, HD_KD, HD_VD = 68, 76, 84
N_HEADS_ALL = 92
N_BIG = N_HEADS_ALL * HEAD_DIM
GATE_COL0 = 0
FORGET_COL0 = 24
N_GATES = 3 * H_MIX
GATES_AT = (H_MIX + 6 * NSA_KV) * HEAD_DIM
FORGET_AT = GATES_AT + N_GATES + 3 * H_MIX * HEAD_DIM
_ROPED_HEADS = (tuple(range(HD_QA, HD_QA + H_MIX)) + tuple(range(HD_KS, HD_KS + NSA_KV))
                + tuple(range(HD_KW, HD_KW + NSA_KV)) + tuple(range(HD_QC, HD_QC + 2 * H_MIX)))
_QUERY_HEADS = tuple(h for q0 in (HD_QA, HD_QB, HD_QC, HD_QD) for h in range(q0, q0 + H_MIX))


def _cparams(sem):
    return pltpu.CompilerParams(dimension_semantics=sem, vmem_limit_bytes=VMEM_LIMIT_BYTES)


def _dot_nt(a, b):
    return lax.dot_general(a, b, (((1,), (1,)), ((), ())), preferred_element_type=jnp.float32)


def _dot(a, b):
    return jnp.dot(a, b, preferred_element_type=jnp.float32)


def _rmsnorm_kernel(x_ref, g_ref, o_ref):
    x = x_ref[...]
    ms = jnp.mean(x * x, axis=-1, keepdims=True)
    o_ref[...] = (x * lax.rsqrt(ms + RMS_EPS) * g_ref[...]).astype(o_ref.dtype)


def _rmsnorm(x, g, out_dtype, tm=256):
    m, d = x.shape
    return pl.pallas_call(
        _rmsnorm_kernel,
        out_shape=jax.ShapeDtypeStruct((m, d), out_dtype),
        grid=(m // tm,),
        in_specs=[pl.BlockSpec((tm, d), lambda i: (i, 0)),
                  pl.BlockSpec((1, d), lambda i: (0, 0))],
        out_specs=pl.BlockSpec((tm, d), lambda i: (i, 0)),
        compiler_params=_cparams(("parallel",)),
        name="rmsnorm",
    )(x, g.reshape(1, d))


def _mm_kernel(*refs, n_extra, epilogue):
    x_ref, w_ref = refs[0], refs[1]
    extras = refs[2:2 + n_extra]
    o_ref = refs[2 + n_extra]
    acc = _dot(x_ref[...].astype(jnp.bfloat16), w_ref[...].astype(jnp.bfloat16))
    o_ref[...] = epilogue(acc, *extras).astype(o_ref.dtype)


def _epi_none(acc):
    return acc


def _epi_relu2(acc):
    r = jnp.maximum(acc, 0.0)
    return r * r


def _epi_ple(acc, h_ref, p_ref, wp_ref):
    pp = _dot(p_ref[...].astype(jnp.bfloat16), wp_ref[...].astype(jnp.bfloat16))
    return h_ref[...] + jax.nn.sigmoid(acc) * pp


def _matmul(x, w, layer, out_dtype, epilogue=_epi_none, extras=(), extra_specs=(), tm=1024, tn=512, name="matmul"):
    m, kdim = x.shape
    n = w.shape[2]
    tn = min(tn, n)
    return pl.pallas_call(
        functools.partial(_mm_kernel, n_extra=len(extras), epilogue=epilogue),
        out_shape=jax.ShapeDtypeStruct((m, n), out_dtype),
        grid=(m // tm, n // tn),
        in_specs=[pl.BlockSpec((tm, kdim), lambda i, j: (i, 0), pipeline_mode=pl.Buffered(1)),
                  pl.BlockSpec((None, kdim, tn), lambda i, j: (layer, 0, j)),
                  *extra_specs],
        out_specs=pl.BlockSpec((tm, tn), lambda i, j: (i, j)),
        compiler_params=_cparams(("parallel", "arbitrary")),
        name=name,
    )(x, w, *extras)


def _out_proj_kernel(*refs):
    *x_refs, w_ref, h_ref, o_ref = refs
    acc = h_ref[...]
    row = 0
    for x_ref in x_refs:
        width = x_ref.shape[1]
        acc = acc + _dot(x_ref[...], w_ref[row:row + width, :].astype(jnp.bfloat16))
        row += width
    o_ref[...] = acc


def _out_proj(xs, w, layer, h, tm, tn):
    m, n = h.shape
    return pl.pallas_call(
        _out_proj_kernel,
        out_shape=jax.ShapeDtypeStruct((m, n), jnp.float32),
        grid=(m // tm, n // tn),
        in_specs=[*[pl.BlockSpec((tm, x.shape[1]), lambda i, j: (i, 0), pipeline_mode=pl.Buffered(1)) for x in xs],
                  pl.BlockSpec((None, w.shape[1], tn), lambda i, j: (layer, 0, j)),
                  pl.BlockSpec((tm, tn), lambda i, j: (i, j))],
        out_specs=pl.BlockSpec((tm, tn), lambda i, j: (i, j)),
        compiler_params=_cparams(("parallel", "arbitrary")),
        name="out_proj",
    )(*xs, w, h)


def _mm_residual_ksplit_kernel(x_ref, w_ref, h_ref, o_ref):
    @pl.when(pl.program_id(2) == 0)
    def _():
        o_ref[...] = h_ref[...]

    o_ref[...] += _dot(x_ref[...], w_ref[...].astype(jnp.bfloat16))


def _matmul_residual_ksplit(x, w, layer, h, tm=2048, tn=1024, tk=1024, name="matmul_ksplit"):
    m, kdim = x.shape
    n = w.shape[2]
    return pl.pallas_call(
        _mm_residual_ksplit_kernel,
        out_shape=jax.ShapeDtypeStruct((m, n), jnp.float32),
        grid=(m // tm, n // tn, kdim // tk),
        in_specs=[pl.BlockSpec((tm, tk), lambda i, j, k: (i, k)),
                  pl.BlockSpec((None, tk, tn), lambda i, j, k: (layer, k, j)),
                  pl.BlockSpec((tm, tn), lambda i, j, k: (i, j), pipeline_mode=pl.Buffered(1))],
        out_specs=pl.BlockSpec((tm, tn), lambda i, j, k: (i, j)),
        compiler_params=_cparams(("parallel", "parallel", "arbitrary")),
        name=name,
    )(x, w, h)


def _rope_table_kernel(pos_ref, freq_ref, sign_ref, cos_ref, sin_ref):
    ang = pos_ref[...] * freq_ref[...]
    cos_ref[...] = jnp.cos(ang)
    sin_ref[...] = jnp.sin(ang) * sign_ref[...]


def _rope_tables(positions, ts=512):
    n = positions.size
    half = HEAD_DIM // 2
    inv_freq = ROPE_THETA ** (-jnp.arange(half, dtype=jnp.float32) / half)
    freq = jnp.concatenate([inv_freq, inv_freq]).reshape(1, HEAD_DIM)
    sign = jnp.concatenate([-jnp.ones((half,), jnp.float32), jnp.ones((half,), jnp.float32)]).reshape(1, HEAD_DIM)
    pos = positions.astype(jnp.float32).reshape(n, 1)
    row = pl.BlockSpec((1, HEAD_DIM), lambda i: (0, 0))
    return pl.pallas_call(
        _rope_table_kernel,
        out_shape=[jax.ShapeDtypeStruct((n, HEAD_DIM), jnp.float32)] * 2,
        grid=(n // ts,),
        in_specs=[pl.BlockSpec((ts, 1), lambda i: (i, 0)), row, row],
        out_specs=[pl.BlockSpec((ts, HEAD_DIM), lambda i: (i, 0))] * 2,
        compiler_params=_cparams(("parallel",)),
        name="rope_tables",
    )(pos, freq, sign)


_FLAG_ROPED, _FLAG_QUERY = 1, 2


def _in_proj_kernel(flags_ref, row0_ref, x_ref, wt_ref, cos_ref, sin_ref, o_ref):
    j = pl.program_id(1)
    heads = o_ref.shape[1]
    acc = _dot_nt(x_ref[...], wt_ref[0].astype(jnp.bfloat16))
    for hh in range(heads):
        t = acc[:, hh * HEAD_DIM:(hh + 1) * HEAD_DIM]
        flags = flags_ref[j * heads + hh]
        roped = (flags & _FLAG_ROPED) > 0
        scale = jnp.where((flags & _FLAG_QUERY) > 0, Q_SCALE, 1.0)
        cos = jnp.where(roped, cos_ref[...], 1.0)
        sin = jnp.where(roped, sin_ref[...], 0.0)
        o_ref[0, hh] = ((t * cos + pltpu.roll(t, HEAD_DIM // 2, 1) * sin) * scale).astype(o_ref.dtype)


def _in_proj_heads(xn, w_in_t, layer, cos, sin, batch, tm=1024, tn=512):
    m, d = xn.shape
    s = m // batch
    heads = tn // HEAD_DIM
    q_tiles = s // tm
    flags = np.zeros((N_HEADS_ALL,), np.int32)
    flags[list(_ROPED_HEADS)] |= _FLAG_ROPED
    flags[list(_QUERY_HEADS)] |= _FLAG_QUERY
    col = np.arange(0, N_BIG, tn)
    row0 = col + np.where(col >= GATES_AT, N_GATES, 0) + np.where(col >= FORGET_AT - N_GATES, H_MIX, 0)
    table = pl.BlockSpec((tm, HEAD_DIM), lambda i, j, f, r: (i, 0))
    return pl.pallas_call(
        _in_proj_kernel,
        out_shape=jax.ShapeDtypeStruct((batch, N_HEADS_ALL, s, HEAD_DIM), jnp.bfloat16),
        grid_spec=pltpu.PrefetchScalarGridSpec(
            num_scalar_prefetch=2,
            grid=(m // tm, N_BIG // tn),
            in_specs=[pl.BlockSpec((tm, d), lambda i, j, f, r: (i, 0), pipeline_mode=pl.Buffered(1)),
                      pl.BlockSpec((pl.Element(1), pl.Element(tn), pl.Element(d)),
                                   lambda i, j, f, r: (layer, pl.multiple_of(r[j], 8), 0)),
                      table, table],
            out_specs=pl.BlockSpec((1, heads, tm, HEAD_DIM),
                                   lambda i, j, f, r: (i // q_tiles, j, i % q_tiles, 0)),
        ),
        compiler_params=_cparams(("parallel", "arbitrary")),
        name="in_proj",
    )(jnp.asarray(flags), jnp.asarray(row0.astype(np.int32)), xn, w_in_t, cos, sin)


def _in_proj_small_kernel(x_ref, wt_ref, o_ref):
    o_ref[...] = _dot_nt(x_ref[...], wt_ref[...].astype(jnp.bfloat16))


def _in_proj_small(xn, w_small_t, layer, tm=1024):
    m, d = xn.shape
    n = w_small_t.shape[1]
    return pl.pallas_call(
        _in_proj_small_kernel,
        out_shape=jax.ShapeDtypeStruct((m, n), jnp.float32),
        grid=(m // tm,),
        in_specs=[pl.BlockSpec((tm, d), lambda i: (i, 0)),
                  pl.BlockSpec((None, n, d), lambda i: (layer, 0, 0))],
        out_specs=pl.BlockSpec((tm, n), lambda i: (i, 0)),
        compiler_params=_cparams(("parallel",)),
        name="in_proj_small",
    )(xn, w_small_t)


def _softplus(z):
    return jnp.maximum(z, 0.0) + jnp.log1p(jnp.exp(-jnp.abs(z)))


def _logf_cumsum_kernel(x_ref, bias_ref, o_ref, *, blk):
    s = x_ref.shape[1]
    r = lax.broadcasted_iota(jnp.int32, (blk, blk), 0)
    c = lax.broadcasted_iota(jnp.int32, (blk, blk), 1)
    tri = jnp.where(c <= r, 1.0, 0.0).astype(jnp.float32)

    def body(i, carry):
        x = x_ref[0, pl.ds(i * blk, blk), :]
        logf = -_softplus(-(x + bias_ref[...]))
        cs = jnp.dot(tri, logf, preferred_element_type=jnp.float32,
                     precision=lax.Precision.HIGHEST) + carry
        o_ref[0, pl.ds(i * blk, blk), :] = cs
        return cs[blk - 1:blk, :]

    lax.fori_loop(0, s // blk, body, jnp.zeros((1, x_ref.shape[2]), jnp.float32))


def _logf_cumsum(small, bias_row, blk=128):
    b, s, w = small.shape
    return pl.pallas_call(
        functools.partial(_logf_cumsum_kernel, blk=blk),
        out_shape=jax.ShapeDtypeStruct((b, s, w), jnp.float32),
        grid=(b,),
        in_specs=[pl.BlockSpec((1, s, w), lambda bi: (bi, 0, 0)),
                  pl.BlockSpec((1, w), lambda bi: (0, 0))],
        out_specs=pl.BlockSpec((1, s, w), lambda bi: (bi, 0, 0)),
        compiler_params=_cparams(("parallel",)),
        name="logf_cumsum",
    )(small, bias_row)


def _rowmax_lanes(s):
    m = s[:, :HEAD_DIM]
    for c in range(1, s.shape[1] // HEAD_DIM):
        m = jnp.maximum(m, s[:, c * HEAD_DIM:(c + 1) * HEAD_DIM])
    return jnp.broadcast_to(jnp.max(m, axis=-1, keepdims=True), m.shape)


def _fold_lanes(p):
    a = p[:, :HEAD_DIM]
    for c in range(1, p.shape[1] // HEAD_DIM):
        a = a + p[:, c * HEAD_DIM:(c + 1) * HEAD_DIM]
    return a


def _tile_lanes(x, width):
    return jnp.concatenate([x] * (width // HEAD_DIM), axis=1)


def _causal_flash_kernel(*refs, mode, hb, t):
    it = iter(refs)
    q_ref, k_ref, v_ref = next(it), next(it), next(it)
    bias_ref = next(it) if mode == "fox" else None
    sel_ref, onehot_ref = (next(it), next(it)) if mode == "sel" else (None, None)
    o_ref = next(it)
    m_scr, l_scr, acc_scr = next(it), next(it), next(it)
    i = pl.program_id(2)

    m_scr[...] = jnp.full(m_scr.shape, M_INIT, jnp.float32)
    l_scr[...] = jnp.zeros(l_scr.shape, jnp.float32)
    acc_scr[...] = jnp.zeros(acc_scr.shape, jnp.float32)

    def tile(j, diagonal):
        keys = pl.ds(pl.multiple_of(j * t, t), t)
        if mode == "sel":
            k_shared = jnp.concatenate([k_ref[0, 0, keys, :], onehot_ref[keys, :]], axis=1)
            scores = [_dot_nt(jnp.concatenate([q_ref[0, hh], sel_ref[0, 0]], axis=1), k_shared)
                      for hh in range(hb)]
        else:
            scores = [_dot_nt(q_ref[0, hh], k_ref[0, hh, keys, :]) + bias_ref[0, hh, j] for hh in range(hb)]
        if diagonal:
            causal = (lax.broadcasted_iota(jnp.int32, (t, t), 1) <= lax.broadcasted_iota(jnp.int32, (t, t), 0))
            scores = [jnp.where(causal, s, NEG) for s in scores]
        probs = []
        for hh, s in enumerate(scores):
            m_prev = m_scr[hh]
            m_new = jnp.maximum(m_prev, _rowmax_lanes(s))
            alpha = jnp.exp2(m_prev - m_new)
            p = jnp.exp2(s - _tile_lanes(m_new, t))
            l_scr[hh] = alpha * l_scr[hh] + _fold_lanes(p)
            m_scr[hh] = m_new
            probs.append((alpha, p.astype(v_ref.dtype)))
        for hh, (alpha, p) in enumerate(probs):
            v = v_ref[0, 0 if mode == "sel" else hh, keys, :]
            acc_scr[hh] = alpha * acc_scr[hh] + _dot(p, v)

    def off_diagonal(j, carry):
        tile(j, False)
        return carry

    lax.fori_loop(0, i, off_diagonal, 0)
    tile(i, True)
    for hh in range(hb):
        l = jnp.sum(l_scr[hh], axis=-1, keepdims=True)
        o_ref[0, :, hh * HEAD_DIM:(hh + 1) * HEAD_DIM] = (acc_scr[hh] / l).astype(o_ref.dtype)


def _causal_flash(hm, *, mode, q_head0, k_head0, v_head0, hb, t, out_dtype, bias=None, sel=None, onehot=None,
                  name="causal_flash"):
    b, _, s, d = hm.shape
    groups = H_MIX // hb
    kv_heads = hb if mode == "fox" else 1
    in_specs = [pl.BlockSpec((1, hb, t, d), lambda bi, g, i: (bi, q_head0 // hb + g, i, 0)),
                pl.BlockSpec((1, kv_heads, s, d), lambda bi, g, i: (bi, k_head0 // kv_heads + g, 0, 0)),
                pl.BlockSpec((1, kv_heads, s, d), lambda bi, g, i: (bi, v_head0 // kv_heads + g, 0, 0))]
    args = [hm, hm, hm]
    if mode == "fox":
        in_specs.append(pl.BlockSpec((1, hb, s // t, 1, t), lambda bi, g, i: (bi, g, 0, 0, 0)))
        args.append(bias)
    else:
        in_specs.append(pl.BlockSpec((1, 1, t, sel.shape[-1]), lambda bi, g, i: (bi, g, i, 0)))
        in_specs.append(pl.BlockSpec(onehot.shape, lambda bi, g, i: (0, 0)))
        args += [sel, onehot]
    return pl.pallas_call(
        functools.partial(_causal_flash_kernel, mode=mode, hb=hb, t=t),
        out_shape=jax.ShapeDtypeStruct((b, s, H_MIX * d), out_dtype),
        grid=(b, groups, s // t),
        in_specs=in_specs,
        out_specs=pl.BlockSpec((1, t, hb * d), lambda bi, g, i: (bi, i, g)),
        scratch_shapes=[pltpu.VMEM((hb, t, d), jnp.float32)] * 3,
        compiler_params=_cparams(("parallel", "parallel", "arbitrary")),
        name=name,
    )(*args)


def _band_kernel(*refs, hb, kv_heads, t, wpad, window, seg_len, has_lse):
    q_ref, k_ref, v_ref, o_ref = refs[:4]
    lse_ref = refs[4] if has_lse else None
    s_len = k_ref.shape[2]
    span = t + wpad
    t0 = pl.program_id(2) * t
    seg0 = (t0 // seg_len) * seg_len
    start = jnp.minimum(jnp.maximum(t0 - wpad, seg0), s_len - span)
    start = pl.multiple_of(start, HEAD_DIM)
    keys = pl.ds(start, span)
    qpos = t0 + lax.broadcasted_iota(jnp.int32, (t, span), 0)
    kpos = start + lax.broadcasted_iota(jnp.int32, (t, span), 1)
    dist = qpos - kpos
    mask = (dist >= 0) & (dist <= window) & (kpos >= seg0)
    kv_of = [hh * kv_heads // hb for hh in range(hb)]
    scores = [jnp.where(mask, _dot_nt(q_ref[0, hh], k_ref[0, kv_of[hh], keys, :]), NEG) for hh in range(hb)]
    probs = []
    for s in scores:
        m = _rowmax_lanes(s)
        p = jnp.exp2(s - _tile_lanes(m, span))
        l = jnp.broadcast_to(jnp.sum(_fold_lanes(p), axis=-1, keepdims=True), m.shape)
        probs.append((m, l, p.astype(v_ref.dtype)))
    for hh, (m, l, p) in enumerate(probs):
        out = _dot(p, v_ref[0, kv_of[hh], keys, :]) / l
        if has_lse:
            o_ref[0, hh] = out.astype(o_ref.dtype)
            lse_ref[0, hh] = m + jnp.log2(l)
        else:
            o_ref[0, :, hh * HEAD_DIM:(hh + 1) * HEAD_DIM] = out.astype(o_ref.dtype)


def _band(src, *, q_head0, k_head0, v_head0, hb, kv_heads, t, wpad, window, seg_len, has_lse, name):
    b, _, s, d = src.shape
    groups = H_MIX // hb
    in_specs = [pl.BlockSpec((1, hb, t, d), lambda bi, g, i: (bi, q_head0 // hb + g, i, 0)),
                pl.BlockSpec((1, kv_heads, s, d), lambda bi, g, i: (bi, k_head0 // kv_heads + g, 0, 0)),
                pl.BlockSpec((1, kv_heads, s, d), lambda bi, g, i: (bi, v_head0 // kv_heads + g, 0, 0))]
    if has_lse:
        hm_spec = pl.BlockSpec((1, hb, t, d), lambda bi, g, i: (bi, g, i, 0))
        out_shape = [jax.ShapeDtypeStruct((b, H_MIX, s, d), jnp.float32)] * 2
        out_specs = [hm_spec, hm_spec]
    else:
        out_shape = jax.ShapeDtypeStruct((b, s, H_MIX * d), jnp.float32)
        out_specs = pl.BlockSpec((1, t, hb * d), lambda bi, g, i: (bi, i, g))
    return pl.pallas_call(
        functools.partial(_band_kernel, hb=hb, kv_heads=kv_heads, t=t, wpad=wpad, window=window,
                          seg_len=seg_len, has_lse=has_lse),
        out_shape=out_shape,
        grid=(b, groups, s // t),
        in_specs=in_specs,
        out_specs=out_specs,
        compiler_params=_cparams(("parallel", "parallel", "arbitrary")),
        name=name,
    )(src, src, src)


def _stickbreak_kernel(q_ref, k_ref, v_ref, u2_ref, o_ref, run_scr, acc_scr, *, hb, t):
    i = pl.program_id(2)
    run_scr[...] = jnp.zeros(run_scr.shape, jnp.float32)
    acc_scr[...] = jnp.zeros(acc_scr.shape, jnp.float32)
    u2 = u2_ref[...]
    sign_bit = jnp.uint32(0x80000000)

    def tile(j, diagonal):
        keys = pl.ds(pl.multiple_of(j * t, t), t)
        if diagonal:
            strict = (lax.broadcasted_iota(jnp.int32, (t, t), 1) < lax.broadcasted_iota(jnp.int32, (t, t), 0))
        zs = [_dot_nt(q_ref[0, hh], k_ref[0, hh, keys, :]) for hh in range(hb)]
        stage = []
        for z2 in zs:
            neg_abs = lax.bitcast_convert_type(lax.bitcast_convert_type(z2, jnp.uint32) | sign_bit, jnp.float32)
            lg = jnp.log2(1.0 + jnp.exp2(neg_abs))
            sp2 = jnp.maximum(z2, 0.0) + lg
            if diagonal:
                sp2 = jnp.where(strict, sp2, 0.0)
            hi = sp2.astype(jnp.bfloat16)
            lo = (sp2 - hi.astype(jnp.float32)).astype(jnp.bfloat16)
            stage.append((jnp.minimum(z2, 0.0) - lg, sp2[:, 0:1], jnp.concatenate([hi, lo], axis=1)))
        laters = [_dot(hl, u2) for (_, _, hl) in stage]
        weights = []
        for hh, later in enumerate(laters):
            log_beta, first, _ = stage[hh]
            run = run_scr[hh]
            a = jnp.exp2(log_beta - (later + _tile_lanes(run, t)))
            if diagonal:
                a = jnp.where(strict, a, 0.0)
            run_scr[hh] = run + jnp.broadcast_to(later[:, 0:1] + first, run.shape)
            weights.append(a.astype(v_ref.dtype))
        for hh, a in enumerate(weights):
            acc_scr[hh] += _dot(a, v_ref[0, hh, keys, :])

    def off_diagonal(n, carry):
        tile(i - 1 - n, False)
        return carry

    tile(i, True)
    lax.fori_loop(0, i, off_diagonal, 0)
    for hh in range(hb):
        o_ref[0, :, hh * HEAD_DIM:(hh + 1) * HEAD_DIM] = acc_scr[hh].astype(o_ref.dtype)


def _stickbreak(hm, *, q_head0, k_head0, v_head0, hb, t, out_dtype):
    b, _, s, d = hm.shape
    later = np.arange(t)[:, None] > np.arange(t)[None, :]
    u2 = jnp.asarray(np.concatenate([later, later], axis=0), jnp.bfloat16)

    def heads(h0):
        return pl.BlockSpec((1, hb, s, d), lambda bi, g, i: (bi, h0 // hb + g, 0, 0))

    return pl.pallas_call(
        functools.partial(_stickbreak_kernel, hb=hb, t=t),
        out_shape=jax.ShapeDtypeStruct((b, s, H_MIX * d), out_dtype),
        grid=(b, H_MIX // hb, s // t),
        in_specs=[pl.BlockSpec((1, hb, t, d), lambda bi, g, i: (bi, q_head0 // hb + g, i, 0)),
                  heads(k_head0), heads(v_head0),
                  pl.BlockSpec((2 * t, t), lambda bi, g, i: (0, 0))],
        out_specs=pl.BlockSpec((1, t, hb * d), lambda bi, g, i: (bi, i, g)),
        scratch_shapes=[pltpu.VMEM((hb, t, d), jnp.float32), pltpu.VMEM((hb, t, d), jnp.float32)],
        compiler_params=_cparams(("parallel", "parallel", "arbitrary")),
        name="stickbreak",
    )(hm, hm, hm, u2)


def _gelu_tanh(x):
    return 0.5 * x * (1.0 + jnp.tanh(math.sqrt(2.0 / math.pi) * (x + 0.044715 * (x * x * x))))


def _compress_kernel(x_ref, pe_ref, w1_ref, w2_ref, o_ref):
    half = x_ref.shape[3]
    x = x_ref[0, 0]
    w1 = w1_ref[0].astype(jnp.bfloat16)
    first = _dot(x, w1[:half])
    second = _dot(x, w1[half:])
    pe = jnp.broadcast_to(pe_ref[0], (8, pe_ref.shape[2])).astype(jnp.bfloat16)
    pe_term = _dot(pe, w1)[0:1]
    n_chunks = x.shape[0]
    hid = first + pltpu.roll(second, n_chunks - 1, 0) + pe_term
    act = _gelu_tanh(hid)
    o_ref[0, 0] = _dot(act.astype(jnp.bfloat16), w2_ref[0].astype(jnp.bfloat16))


def _compress(chunks, pe, w1, w2):
    b, nh, nc, half = chunks.shape
    return pl.pallas_call(
        _compress_kernel,
        out_shape=jax.ShapeDtypeStruct((b, nh, nc, HEAD_DIM), jnp.float32),
        grid=(b, nh),
        in_specs=[pl.BlockSpec((1, 1, nc, half), lambda bi, h: (bi, h, 0, 0)),
                  pl.BlockSpec((1, 1, 2 * half), lambda bi, h: (h // NSA_KV, 0, 0)),
                  pl.BlockSpec((1, 2 * half, w1.shape[2]), lambda bi, h: (h // NSA_KV, 0, 0)),
                  pl.BlockSpec((1, w2.shape[1], HEAD_DIM), lambda bi, h: (h // NSA_KV, 0, 0))],
        out_specs=pl.BlockSpec((1, 1, nc, HEAD_DIM), lambda bi, h: (bi, h, 0, 0)),
        compiler_params=_cparams(("parallel", "parallel")),
        name="nsa_compress",
    )(chunks, pe, w1, w2)


def _cmp_select_kernel(q_ref, kc_ref, vc_ref, c2s_t_ref, o_ref, sel_ref, *, tq, n_cmp):
    i = pl.program_id(2)
    nc = kc_ref.shape[2]
    kc = kc_ref[0, 0].astype(jnp.bfloat16)
    vc = vc_ref[0, 0].astype(jnp.bfloat16)
    t = i * tq + lax.broadcasted_iota(jnp.int32, (1, tq), 1)
    cidx = lax.broadcasted_iota(jnp.int32, (nc, 1), 0)
    visible = (cidx * CMP_STRIDE + (CMP_LEN - 1) <= t) & (cidx < n_cmp)
    psum = jnp.zeros((nc, tq), jnp.float32)
    for hh in range(NSA_HPG):
        s = jnp.where(visible, _dot_nt(kc, q_ref[0, hh]), NEG)
        m = jnp.max(s, axis=0, keepdims=True)
        e = jnp.where(visible, jnp.exp2(s - m), 0.0)
        l = jnp.sum(e, axis=0, keepdims=True)
        p = e / jnp.where(l > 0.0, l, 1.0)
        psum = psum + p
        o_ref[0, :, hh * HEAD_DIM:(hh + 1) * HEAD_DIM] = _dot(p.T.astype(jnp.bfloat16), vc)

    imp = jnp.dot(c2s_t_ref[...], psum, preferred_element_type=jnp.float32, precision=lax.Precision.HIGHEST)
    w = imp.shape[0]
    n_slc = w // 2
    score_rows = slice(0, n_slc)
    jblk = lax.broadcasted_iota(jnp.int32, (n_slc, 1), 0)
    cur = lax.shift_right_logical(t, int(math.log2(SLC_LEN)))
    forced = (jblk == 0) | (jblk == cur) | (jblk == cur - 1)
    causal_blk = jblk * SLC_LEN <= t
    score = jnp.where(causal_blk, jnp.where(forced, FORCED_SCORE, imp[score_rows]), -1.0)
    rank = jnp.zeros((n_slc, tq), jnp.float32)
    for jp in range(n_slc):
        row = score[jp:jp + 1, :]
        before = (row > score) | ((row == score) & (jblk > jp))
        rank = rank + jnp.where(before, 1.0, 0.0)
    chosen = (rank < float(min(TOPK, n_slc))) & causal_blk
    add_mask = jnp.concatenate([jnp.where(chosen, 0.0, NEG), jnp.zeros((w - n_slc, tq), jnp.float32)], axis=0)
    sel_ref[0, 0] = add_mask.T.astype(sel_ref.dtype)


def _cmp_select(hm, cmp_kv, c2s, *, tq, n_cmp):
    b, _, s, d = hm.shape
    nc = cmp_kv.shape[2]
    w = c2s.shape[1]
    return pl.pallas_call(
        functools.partial(_cmp_select_kernel, tq=tq, n_cmp=n_cmp),
        out_shape=[jax.ShapeDtypeStruct((b, s, H_MIX * d), jnp.float32),
                   jax.ShapeDtypeStruct((b, NSA_KV, s, w), jnp.bfloat16)],
        grid=(b, NSA_KV, s // tq),
        in_specs=[pl.BlockSpec((1, NSA_HPG, tq, d), lambda bi, g, i: (bi, g, i, 0)),
                  pl.BlockSpec((1, 1, nc, d), lambda bi, g, i: (bi, g, 0, 0)),
                  pl.BlockSpec((1, 1, nc, d), lambda bi, g, i: (bi, NSA_KV + g, 0, 0)),
                  pl.BlockSpec((w, nc), lambda bi, g, i: (0, 0))],
        out_specs=[pl.BlockSpec((1, tq, NSA_HPG * d), lambda bi, g, i: (bi, i, g)),
                   pl.BlockSpec((1, 1, tq, w), lambda bi, g, i: (bi, g, i, 0))],
        compiler_params=_cparams(("parallel", "parallel", "parallel")),
        name="nsa_cmp_select",
    )(hm, cmp_kv, cmp_kv, c2s.T)


def _nsa_combine_kernel(oc_ref, os_ref, ow_ref, g_ref, o_ref):
    gates = jax.nn.sigmoid(g_ref[0])
    for h in range(H_MIX):
        cols = slice(h * HEAD_DIM, (h + 1) * HEAD_DIM)
        c0 = GATE_COL0 + 3 * h
        out = (gates[:, c0:c0 + 1] * oc_ref[0, :, cols]
               + gates[:, c0 + 1:c0 + 2] * os_ref[0, :, cols]
               + gates[:, c0 + 2:c0 + 3] * ow_ref[0, :, cols])
        o_ref[0, :, cols] = out.astype(o_ref.dtype)


def _nsa_combine(o_cmp, o_slc, o_win, small, tq=512):
    b, s, w = o_cmp.shape
    big = pl.BlockSpec((1, tq, w), lambda bi, i: (bi, i, 0))
    return pl.pallas_call(
        _nsa_combine_kernel,
        out_shape=jax.ShapeDtypeStruct((b, s, w), jnp.bfloat16),
        grid=(b, s // tq),
        in_specs=[big, big, big, pl.BlockSpec((1, tq, small.shape[2]), lambda bi, i: (bi, i, 0))],
        out_specs=big,
        compiler_params=_cparams(("parallel", "parallel")),
        name="nsa_combine",
    )(o_cmp, o_slc, o_win, small)


def _dil_combine_kernel(o1, o2, o3, l1, l2, l3, o_ref):
    a, bb, c = l1[0, 0], l2[0, 0], l3[0, 0]
    m = jnp.maximum(jnp.maximum(a, bb), c)
    ea, eb, ec = jnp.exp2(a - m), jnp.exp2(bb - m), jnp.exp2(c - m)
    tot = ea + eb + ec
    out = o1[0, 0] * (ea / tot) + o2[0, 0] * (eb / tot) + o3[0, 0] * (ec / tot)
    o_ref[0] = out.astype(o_ref.dtype)


def _dil_combine(outs, lses, tq=1024):
    b, h, s, d = outs[0].shape
    spec = pl.BlockSpec((1, 1, tq, d), lambda bi, hh, i: (bi, hh, i, 0))
    return pl.pallas_call(
        _dil_combine_kernel,
        out_shape=jax.ShapeDtypeStruct((b, s, h * d), jnp.bfloat16),
        grid=(b, h, s // tq),
        in_specs=[spec] * 6,
        out_specs=pl.BlockSpec((1, tq, d), lambda bi, hh, i: (bi, i, hh)),
        compiler_params=_cparams(("parallel", "parallel", "parallel")),
        name="dil_combine",
    )(*outs, *lses)


def _to_residue_major(t, dil):
    b, h, s, d = t.shape
    return t.reshape(b, h, s // dil, dil, d).transpose(0, 1, 3, 2, 4).reshape(b, h, s, d)


def _from_residue_major(t, dil):
    b, h, s, d = t.shape
    return t.reshape(b, h, dil, s // dil, d).transpose(0, 1, 3, 2, 4).reshape(b, h, s, d)


def _cmp_to_slc_matrix(n_chunks, n_cmp, n_slc):
    ratio = SLC_LEN // CMP_STRIDE
    span = CMP_LEN // CMP_STRIDE
    jj, mm, nn = np.meshgrid(np.arange(n_slc), np.arange(ratio), np.arange(span), indexing="ij")
    cc = ratio * jj + mm + nn
    keep = cc < n_cmp
    mat = np.zeros((n_chunks, 2 * n_slc), np.float32)
    np.add.at(mat, (cc[keep], jj[keep]), 1.0)
    return jnp.asarray(mat)


def _mixers(hm, small, fox_bf, cmp_pe_k, cmp_w1_k, cmp_w2_k, cmp_pe_v, cmp_w1_v, cmp_w2_v):
    b, _, s, d = hm.shape
    bf16 = jnp.bfloat16

    n_chunks = s // CMP_STRIDE
    n_cmp = (s - CMP_LEN) // CMP_STRIDE + 1
    n_slc = s // SLC_LEN
    chunks = hm[:, HD_KC:HD_KC + 2 * NSA_KV].reshape(b, 2 * NSA_KV, n_chunks, CMP_STRIDE * d)
    cmp_kv = _compress(chunks,
                       jnp.stack([cmp_pe_k, cmp_pe_v]).reshape(2, 1, CMP_LEN * d),
                       jnp.stack([cmp_w1_k, cmp_w1_v]), jnp.stack([cmp_w2_k, cmp_w2_v]))
    o_cmp, sel = _cmp_select(hm, cmp_kv, _cmp_to_slc_matrix(n_chunks, n_cmp, n_slc), tq=256, n_cmp=n_cmp)
    onehot = jnp.asarray((np.arange(s)[:, None] // SLC_LEN) == np.arange(2 * n_slc)[None, :], bf16)
    o_slc = _causal_flash(hm, mode="sel", q_head0=HD_QA, k_head0=HD_KS, v_head0=HD_VS, hb=NSA_HPG, t=512,
                          sel=sel, onehot=onehot, out_dtype=jnp.float32, name="nsa_selected")
    o_win = _band(hm, q_head0=HD_QA, k_head0=HD_KW, v_head0=HD_VW, hb=NSA_HPG, kv_heads=1, t=256, wpad=NSA_WINDOW,
                  window=NSA_WINDOW - 1, seg_len=s, has_lse=False, name="nsa_window")
    o_a = _nsa_combine(o_cmp, o_slc, o_win, small)

    t_fox = 512
    bias_row = jnp.zeros((1, small.shape[2]), jnp.float32).at[0, FORGET_COL0:FORGET_COL0 + H_MIX].set(fox_bf)
    csum = _logf_cumsum(small, bias_row)
    key_bias = (-LOG2E) * csum[:, :, FORGET_COL0:FORGET_COL0 + H_MIX].transpose(0, 2, 1)
    o_b = _causal_flash(hm, mode="fox", q_head0=HD_QB, k_head0=HD_KB, v_head0=HD_VB, hb=4, t=t_fox,
                        bias=key_bias.reshape(b, H_MIX, s // t_fox, 1, t_fox), out_dtype=bf16, name="fox")

    outs, lses = [], []
    for window, dil in DIL_CONFIGS:
        if dil == 1:
            src, heads0 = hm, (HD_QC, HD_KC2, HD_VC2)
        else:
            src, heads0 = _to_residue_major(hm[:, HD_QC:HD_QC + 3 * H_MIX], dil), (0, H_MIX, 2 * H_MIX)
        hb = 4 if dil == 1 else H_MIX
        o, lse = _band(src, q_head0=heads0[0], k_head0=heads0[1], v_head0=heads0[2], hb=hb, kv_heads=hb, t=256,
                       wpad=HEAD_DIM, window=window // dil, seg_len=s // dil, has_lse=True, name=f"dilated_{dil}")
        if dil != 1:
            o, lse = _from_residue_major(o, dil), _from_residue_major(lse, dil)
        outs.append(o)
        lses.append(lse)
    o_c = _dil_combine(outs, lses)

    o_d = _stickbreak(hm, q_head0=HD_QD, k_head0=HD_KD, v_head0=HD_VD, hb=4, t=256, out_dtype=bf16)

    return o_a, o_b, o_c, o_d


def _small_w_in_t(w_in_t):
    pad = jnp.zeros((w_in_t.shape[0], HEAD_DIM - N_GATES - H_MIX, w_in_t.shape[2]), w_in_t.dtype)
    return jnp.concatenate([w_in_t[:, GATES_AT:GATES_AT + N_GATES], w_in_t[:, FORGET_AT:FORGET_AT + H_MIX], pad],
                           axis=1)


def _layer(h, layer, p, cos, sin, batch, norm_attn, w_in_t, w_small_t, fox_bf, cmp_pe_k, cmp_w1_k, cmp_w2_k,
           cmp_pe_v, cmp_w1_v, cmp_w2_v, w_o, norm_mlp, w_up, w_down, norm_ple, w_ple_gate, w_ple_proj):
    m, d = h.shape
    s = m // batch
    bf16 = jnp.bfloat16
    xn = _rmsnorm(h, norm_attn, bf16)
    hm = _in_proj_heads(xn, w_in_t, layer, cos, sin, batch)
    small = _in_proj_small(xn, w_small_t, layer)
    mix = _mixers(hm, small.reshape(batch, s, HEAD_DIM), fox_bf,
                  cmp_pe_k, cmp_w1_k, cmp_w2_k, cmp_pe_v, cmp_w1_v, cmp_w2_v)

    tm, tn = 2048, 256
    res_spec = pl.BlockSpec((tm, tn), lambda i, j: (i, j))
    h = _out_proj([o.reshape(m, -1) for o in mix], w_o, layer, h, tm=tm, tn=tn)

    x2 = _rmsnorm(h, norm_mlp, bf16)
    mid = _matmul(x2, w_up, layer, bf16, _epi_relu2, tm=1024, tn=512, name="mlp_up")
    h = _matmul_residual_ksplit(mid, w_down, layer, h, name="mlp_down")

    x3 = _rmsnorm(h, norm_ple, bf16)
    ple_dim = p.shape[2]
    h = _matmul(x3, w_ple_gate, layer, jnp.float32, _epi_ple, (h, p, w_ple_proj),
                (res_spec, pl.BlockSpec((None, tm, ple_dim), lambda i, j: (layer, i, 0)),
                 pl.BlockSpec((None, ple_dim, tn), lambda i, j: (layer, 0, j))),
                tm=tm, tn=tn, name="ple_gate")
    return h


def kernel(x, p, positions, norm_attn, w_in, fox_bf, cmp_pe_k, cmp_w1_k, cmp_w2_k, cmp_pe_v, cmp_w1_v, cmp_w2_v,
           w_o, norm_mlp, w_up, w_down, norm_ple, w_ple_gate, w_ple_proj, norm_final):
    batch, s, d = x.shape
    depth = p.shape[0]
    cos, sin = _rope_tables(positions)
    w_in_t = jnp.swapaxes(w_in, 1, 2)
    w_small_t = _small_w_in_t(w_in_t)
    p = p.reshape(depth, batch * s, -1)
    h = x.reshape(batch * s, d)
    for i in range(depth):
        h = _layer(h, i, p, cos, sin, batch, norm_attn[i], w_in_t, w_small_t, fox_bf[i],
                   cmp_pe_k[i], cmp_w1_k[i], cmp_w2_k[i], cmp_pe_v[i], cmp_w1_v[i], cmp_w2_v[i],
                   w_o, norm_mlp[i], w_up, w_down, norm_ple[i], w_ple_gate, w_ple_proj)
    return _rmsnorm(h, norm_final, x.dtype).reshape(batch, s, d)
```

```python
import functools
import math

import numpy as np
import jax
import jax.numpy as jnp
from jax import lax
from jax.experimental import pallas as pl
from jax.experimental.pallas import tpu as pltpu

HEAD_DIM = 128
H_MIX = 8
NSA_KV = 2
NSA_HPG = H_MIX // NSA_KV
CMP_LEN = 32
CMP_STRIDE = 16
SLC_LEN = 64
TOPK = 16
NSA_WINDOW = 512
DIL_CONFIGS = ((128, 1), (512, 4), (2048, 16))
ROPE_THETA = 10000.0
RMS_EPS = 1e-6
NEG = -1e30
M_INIT = -5e29
FORCED_SCORE = 1e9
LOG2E = math.log2(math.e)
Q_SCALE = HEAD_DIM ** -0.5 * LOG2E

VMEM_LIMIT_BYTES = 52 * 1024 * 1024

MXU_DIM = 256
ROW_TILE = 512
MM_TM, MM_TN = 1024, 512
DOWN_TM, DOWN_TN, DOWN_TK = 2048, 1024, 1024
DOWN_ROW_CHUNK = 512
FLASH_T = 512
SB_T = MXU_DIM
BAND_T = 256
CMP_TQ = 512
COMBINE_TQ = 512
CUMSUM_BLK = 128
HEADS_PER_STEP = 4

HD_QA, HD_KC, HD_VC, HD_KS, HD_VS, HD_KW, HD_VW = 0, 8, 10, 12, 14, 16, 18
HD_QB, HD_KB, HD_VB = 20, 28, 36
HD_QC, HD_KC2, HD_VC2 = 44, 52, 60
HD_QD, HD_KD, HD_VD = 68, 76, 84
N_HEADS_ALL = 92
N_BIG = N_HEADS_ALL * HEAD_DIM
GATE_COL0 = 0
FORGET_COL0 = 24
N_GATES = 3 * H_MIX
GATES_AT = (H_MIX + 6 * NSA_KV) * HEAD_DIM
FORGET_AT = GATES_AT + N_GATES + 3 * H_MIX * HEAD_DIM
_ROPED_HEADS = (tuple(range(HD_QA, HD_QA + H_MIX)) + tuple(range(HD_KS, HD_KS + NSA_KV))
                + tuple(range(HD_KW, HD_KW + NSA_KV)) + tuple(range(HD_QC, HD_QC + 2 * H_MIX)))
_QUERY_HEADS = tuple(h for q0 in (HD_QA, HD_QB, HD_QC, HD_QD) for h in range(q0, q0 + H_MIX))


def _cparams(sem):
    return pltpu.CompilerParams(dimension_semantics=sem, vmem_limit_bytes=VMEM_LIMIT_BYTES)


def _dot_nt(a, b):
    return lax.dot_general(a, b, (((1,), (1,)), ((), ())), preferred_element_type=jnp.float32)


def _dot(a, b):
    return jnp.dot(a, b, preferred_element_type=jnp.float32)


def _rmsnorm_kernel(x_ref, g_ref, o_ref):
    x = x_ref[...]
    ms = jnp.mean(x * x, axis=-1, keepdims=True)
    o_ref[...] = (x * lax.rsqrt(ms + RMS_EPS) * g_ref[...]).astype(o_ref.dtype)


def _rmsnorm(x, g, out_dtype, tm=ROW_TILE):
    m, d = x.shape
    return pl.pallas_call(
        _rmsnorm_kernel,
        out_shape=jax.ShapeDtypeStruct((m, d), out_dtype),
        grid=(m // tm,),
        in_specs=[pl.BlockSpec((tm, d), lambda i: (i, 0)),
                  pl.BlockSpec((1, d), lambda i: (0, 0))],
        out_specs=pl.BlockSpec((tm, d), lambda i: (i, 0)),
        compiler_params=_cparams(("parallel",)),
        name="rmsnorm",
    )(x, g.reshape(1, d))


def _mm_kernel(*refs, n_extra, epilogue):
    x_ref, w_ref = refs[0], refs[1]
    extras = refs[2:2 + n_extra]
    o_ref = refs[2 + n_extra]
    acc = _dot(x_ref[...].astype(jnp.bfloat16), w_ref[...].astype(jnp.bfloat16))
    o_ref[...] = epilogue(acc, *extras).astype(o_ref.dtype)


def _epi_none(acc):
    return acc


def _epi_relu2(acc):
    r = jnp.maximum(acc, 0.0)
    return r * r


def _epi_ple(acc, h_ref, p_ref, wp_ref):
    pp = _dot(p_ref[...].astype(jnp.bfloat16), wp_ref[...].astype(jnp.bfloat16))
    return h_ref[...] + jax.nn.sigmoid(acc) * pp


def _matmul(x, w, layer, out_dtype, epilogue=_epi_none, extras=(), extra_specs=(), tm=MM_TM, tn=MM_TN, name="matmul"):
    m, kdim = x.shape
    n = w.shape[2]
    tn = min(tn, n)
    return pl.pallas_call(
        functools.partial(_mm_kernel, n_extra=len(extras), epilogue=epilogue),
        out_shape=jax.ShapeDtypeStruct((m, n), out_dtype),
        grid=(m // tm, n // tn),
        in_specs=[pl.BlockSpec((tm, kdim), lambda i, j: (i, 0)),
                  pl.BlockSpec((None, kdim, tn), lambda i, j: (layer, 0, j)),
                  *extra_specs],
        out_specs=pl.BlockSpec((tm, tn), lambda i, j: (i, j)),
        compiler_params=_cparams(("parallel", "arbitrary")),
        name=name,
    )(x, w, *extras)


def _out_proj_kernel(*refs):
    *x_refs, w_ref, h_ref, o_ref = refs
    acc = h_ref[...]
    row = 0
    for x_ref in x_refs:
        width = x_ref.shape[1]
        acc = acc + _dot(x_ref[...], w_ref[row:row + width, :].astype(jnp.bfloat16))
        row += width
    o_ref[...] = acc


def _out_proj(xs, w, layer, h, tm, tn):
    m, n = h.shape
    return pl.pallas_call(
        _out_proj_kernel,
        out_shape=jax.ShapeDtypeStruct((m, n), jnp.float32),
        grid=(m // tm, n // tn),
        in_specs=[*[pl.BlockSpec((tm, x.shape[1]), lambda i, j: (i, 0)) for x in xs],
                  pl.BlockSpec((None, w.shape[1], tn), lambda i, j: (layer, 0, j)),
                  pl.BlockSpec((tm, tn), lambda i, j: (i, j))],
        out_specs=pl.BlockSpec((tm, tn), lambda i, j: (i, j)),
        compiler_params=_cparams(("parallel", "arbitrary")),
        name="out_proj",
    )(*xs, w, h)


def _mm_residual_ksplit_kernel(x_ref, w_ref, h_ref, o_ref, *, row_chunk):
    w = w_ref[...].astype(jnp.bfloat16)
    chunks = [pl.ds(r, row_chunk) for r in range(0, o_ref.shape[0], row_chunk)]

    @pl.when(pl.program_id(2) == 0)
    def _():
        for rows in chunks:
            o_ref[rows, :] = h_ref[rows, :] + _dot(x_ref[rows, :], w)

    @pl.when(pl.program_id(2) > 0)
    def _():
        for rows in chunks:
            o_ref[rows, :] += _dot(x_ref[rows, :], w)


def _matmul_residual_ksplit(x, w, layer, h, tm=DOWN_TM, tn=DOWN_TN, tk=DOWN_TK, name="matmul_ksplit"):
    m, kdim = x.shape
    n = w.shape[2]
    return pl.pallas_call(
        functools.partial(_mm_residual_ksplit_kernel, row_chunk=DOWN_ROW_CHUNK),
        out_shape=jax.ShapeDtypeStruct((m, n), jnp.float32),
        grid=(m // tm, n // tn, kdim // tk),
        in_specs=[pl.BlockSpec((tm, tk), lambda i, j, k: (i, k)),
                  pl.BlockSpec((None, tk, tn), lambda i, j, k: (layer, k, j)),
                  pl.BlockSpec((tm, tn), lambda i, j, k: (i, j), pipeline_mode=pl.Buffered(1))],
        out_specs=pl.BlockSpec((tm, tn), lambda i, j, k: (i, j)),
        compiler_params=_cparams(("parallel", "parallel", "arbitrary")),
        name=name,
    )(x, w, h)


def _rope_table_kernel(pos_ref, freq_ref, sign_ref, cos_ref, sin_ref):
    ang = pos_ref[...] * freq_ref[...]
    cos_ref[...] = jnp.cos(ang)
    sin_ref[...] = jnp.sin(ang) * sign_ref[...]


def _rope_tables(positions, ts=COMBINE_TQ):
    n = positions.size
    half = HEAD_DIM // 2
    inv_freq = ROPE_THETA ** (-jnp.arange(half, dtype=jnp.float32) / half)
    freq = jnp.concatenate([inv_freq, inv_freq]).reshape(1, HEAD_DIM)
    sign = jnp.concatenate([-jnp.ones((half,), jnp.float32), jnp.ones((half,), jnp.float32)]).reshape(1, HEAD_DIM)
    pos = positions.astype(jnp.float32).reshape(n, 1)
    row = pl.BlockSpec((1, HEAD_DIM), lambda i: (0, 0))
    return pl.pallas_call(
        _rope_table_kernel,
        out_shape=[jax.ShapeDtypeStruct((n, HEAD_DIM), jnp.float32)] * 2,
        grid=(n // ts,),
        in_specs=[pl.BlockSpec((ts, 1), lambda i: (i, 0)), row, row],
        out_specs=[pl.BlockSpec((ts, HEAD_DIM), lambda i: (i, 0))] * 2,
        compiler_params=_cparams(("parallel",)),
        name="rope_tables",
    )(pos, freq, sign)


_FLAG_ROPED, _FLAG_QUERY = 1, 2


def _in_proj_kernel(flags_ref, row0_ref, x_ref, wt_ref, cos_ref, sin_ref, o_ref):
    j = pl.program_id(1)
    heads = o_ref.shape[1]
    acc = _dot_nt(x_ref[...], wt_ref[0].astype(jnp.bfloat16))
    for hh in range(heads):
        t = acc[:, hh * HEAD_DIM:(hh + 1) * HEAD_DIM]
        flags = flags_ref[j * heads + hh]
        roped = (flags & _FLAG_ROPED) > 0
        scale = jnp.where((flags & _FLAG_QUERY) > 0, Q_SCALE, 1.0)
        cos = jnp.where(roped, cos_ref[...], 1.0)
        sin = jnp.where(roped, sin_ref[...], 0.0)
        o_ref[0, hh] = ((t * cos + pltpu.roll(t, HEAD_DIM // 2, 1) * sin) * scale).astype(o_ref.dtype)


def _in_proj_heads(xn, w_in_t, layer, cos, sin, batch, tm=MM_TM, tn=MM_TN):
    m, d = xn.shape
    s = m // batch
    heads = tn // HEAD_DIM
    q_tiles = s // tm
    flags = np.zeros((N_HEADS_ALL,), np.int32)
    flags[list(_ROPED_HEADS)] |= _FLAG_ROPED
    flags[list(_QUERY_HEADS)] |= _FLAG_QUERY
    col = np.arange(0, N_BIG, tn)
    row0 = col + np.where(col >= GATES_AT, N_GATES, 0) + np.where(col >= FORGET_AT - N_GATES, H_MIX, 0)
    table = pl.BlockSpec((tm, HEAD_DIM), lambda i, j, f, r: (i, 0))
    return pl.pallas_call(
        _in_proj_kernel,
        out_shape=jax.ShapeDtypeStruct((batch, N_HEADS_ALL, s, HEAD_DIM), jnp.bfloat16),
        grid_spec=pltpu.PrefetchScalarGridSpec(
            num_scalar_prefetch=2,
            grid=(m // tm, N_BIG // tn),
            in_specs=[pl.BlockSpec((tm, d), lambda i, j, f, r: (i, 0)),
                      pl.BlockSpec((pl.Element(1), pl.Element(tn), pl.Element(d)),
                                   lambda i, j, f, r: (layer, pl.multiple_of(r[j], 8), 0)),
                      table, table],
            out_specs=pl.BlockSpec((1, heads, tm, HEAD_DIM),
                                   lambda i, j, f, r: (i // q_tiles, j, i % q_tiles, 0)),
        ),
        compiler_params=_cparams(("parallel", "arbitrary")),
        name="in_proj",
    )(jnp.asarray(flags), jnp.asarray(row0.astype(np.int32)), xn, w_in_t, cos, sin)


def _in_proj_small_kernel(x_ref, wt_ref, o_ref):
    o_ref[...] = _dot_nt(x_ref[...], wt_ref[...].astype(jnp.bfloat16))


def _in_proj_small(xn, w_small_t, layer, tm=MM_TM):
    m, d = xn.shape
    n = w_small_t.shape[1]
    return pl.pallas_call(
        _in_proj_small_kernel,
        out_shape=jax.ShapeDtypeStruct((m, n), jnp.float32),
        grid=(m // tm,),
        in_specs=[pl.BlockSpec((tm, d), lambda i: (i, 0)),
                  pl.BlockSpec((None, n, d), lambda i: (layer, 0, 0))],
        out_specs=pl.BlockSpec((tm, n), lambda i: (i, 0)),
        compiler_params=_cparams(("parallel",)),
        name="in_proj_small",
    )(xn, w_small_t)


def _softplus(z):
    return jnp.maximum(z, 0.0) + jnp.log1p(jnp.exp(-jnp.abs(z)))


def _logf_cumsum_kernel(x_ref, bias_ref, o_ref, *, blk):
    s = x_ref.shape[1]
    r = lax.broadcasted_iota(jnp.int32, (blk, blk), 0)
    c = lax.broadcasted_iota(jnp.int32, (blk, blk), 1)
    tri = jnp.where(c <= r, 1.0, 0.0).astype(jnp.float32)

    def body(i, carry):
        x = x_ref[0, pl.ds(i * blk, blk), :]
        logf = -_softplus(-(x + bias_ref[...]))
        cs = jnp.dot(tri, logf, preferred_element_type=jnp.float32,
                     precision=lax.Precision.HIGHEST) + carry
        o_ref[0, pl.ds(i * blk, blk), :] = cs
        return cs[blk - 1:blk, :]

    lax.fori_loop(0, s // blk, body, jnp.zeros((1, x_ref.shape[2]), jnp.float32))


def _logf_cumsum(small, bias_row, blk=CUMSUM_BLK):
    b, s, w = small.shape
    return pl.pallas_call(
        functools.partial(_logf_cumsum_kernel, blk=blk),
        out_shape=jax.ShapeDtypeStruct((b, s, w), jnp.float32),
        grid=(b,),
        in_specs=[pl.BlockSpec((1, s, w), lambda bi: (bi, 0, 0)),
                  pl.BlockSpec((1, w), lambda bi: (0, 0))],
        out_specs=pl.BlockSpec((1, s, w), lambda bi: (bi, 0, 0)),
        compiler_params=_cparams(("parallel",)),
        name="logf_cumsum",
    )(small, bias_row)


def _rowmax_lanes(s):
    m = s[:, :HEAD_DIM]
    for c in range(1, s.shape[1] // HEAD_DIM):
        m = jnp.maximum(m, s[:, c * HEAD_DIM:(c + 1) * HEAD_DIM])
    return jnp.broadcast_to(jnp.max(m, axis=-1, keepdims=True), m.shape)


def _fold_lanes(p):
    a = p[:, :HEAD_DIM]
    for c in range(1, p.shape[1] // HEAD_DIM):
        a = a + p[:, c * HEAD_DIM:(c + 1) * HEAD_DIM]
    return a


def _tile_lanes(x, width):
    return jnp.concatenate([x] * (width // HEAD_DIM), axis=1)


def _causal_flash_kernel(*refs, mode, hb, t):
    it = iter(refs)
    q_ref, k_ref, v_ref = next(it), next(it), next(it)
    bias_ref = next(it) if mode == "fox" else None
    sel_ref, onehot_ref = (next(it), next(it)) if mode == "sel" else (None, None)
    o_ref = next(it)
    m_scr, l_scr, acc_scr = next(it), next(it), next(it)
    i = pl.program_id(2)

    m_scr[...] = jnp.full(m_scr.shape, M_INIT, jnp.float32)
    l_scr[...] = jnp.zeros(l_scr.shape, jnp.float32)
    acc_scr[...] = jnp.zeros(acc_scr.shape, jnp.float32)

    def tile(j, diagonal):
        keys = pl.ds(pl.multiple_of(j * t, t), t)
        if mode == "sel":
            k_shared = jnp.concatenate([k_ref[0, 0, keys, :], onehot_ref[keys, :]], axis=1)
            scores = [_dot_nt(jnp.concatenate([q_ref[0, hh], sel_ref[0, 0]], axis=1), k_shared)
                      for hh in range(hb)]
        else:
            scores = [_dot_nt(q_ref[0, hh], k_ref[0, hh, keys, :]) + bias_ref[0, hh, j] for hh in range(hb)]
        if diagonal:
            causal = (lax.broadcasted_iota(jnp.int32, (t, t), 1) <= lax.broadcasted_iota(jnp.int32, (t, t), 0))
            scores = [jnp.where(causal, s, NEG) for s in scores]
        probs = []
        for hh, s in enumerate(scores):
            m_prev = m_scr[hh]
            m_new = jnp.maximum(m_prev, _rowmax_lanes(s))
            alpha = jnp.exp2(m_prev - m_new)
            p = jnp.exp2(s - _tile_lanes(m_new, t))
            l_scr[hh] = alpha * l_scr[hh] + _fold_lanes(p)
            m_scr[hh] = m_new
            probs.append((alpha, p.astype(v_ref.dtype)))
        for hh, (alpha, p) in enumerate(probs):
            v = v_ref[0, 0 if mode == "sel" else hh, keys, :]
            acc_scr[hh] = alpha * acc_scr[hh] + _dot(p, v)

    def off_diagonal(j, carry):
        tile(j, False)
        return carry

    lax.fori_loop(0, i, off_diagonal, 0)
    tile(i, True)
    for hh in range(hb):
        l = jnp.sum(l_scr[hh], axis=-1, keepdims=True)
        o_ref[0, :, hh * HEAD_DIM:(hh + 1) * HEAD_DIM] = (acc_scr[hh] / l).astype(o_ref.dtype)


def _causal_flash(hm, *, mode, q_head0, k_head0, v_head0, hb, t, out_dtype, bias=None, sel=None, onehot=None,
                  name="causal_flash"):
    b, _, s, d = hm.shape
    groups = H_MIX // hb
    kv_heads = hb if mode == "fox" else 1
    in_specs = [pl.BlockSpec((1, hb, t, d), lambda bi, g, i: (bi, q_head0 // hb + g, i, 0)),
                pl.BlockSpec((1, kv_heads, s, d), lambda bi, g, i: (bi, k_head0 // kv_heads + g, 0, 0)),
                pl.BlockSpec((1, kv_heads, s, d), lambda bi, g, i: (bi, v_head0 // kv_heads + g, 0, 0))]
    args = [hm, hm, hm]
    if mode == "fox":
        in_specs.append(pl.BlockSpec((1, hb, s // t, 1, t), lambda bi, g, i: (bi, g, 0, 0, 0)))
        args.append(bias)
    else:
        in_specs.append(pl.BlockSpec((1, 1, t, sel.shape[-1]), lambda bi, g, i: (bi, g, i, 0)))
        in_specs.append(pl.BlockSpec(onehot.shape, lambda bi, g, i: (0, 0)))
        args += [sel, onehot]
    return pl.pallas_call(
        functools.partial(_causal_flash_kernel, mode=mode, hb=hb, t=t),
        out_shape=jax.ShapeDtypeStruct((b, s, H_MIX * d), out_dtype),
        grid=(b, groups, s // t),
        in_specs=in_specs,
        out_specs=pl.BlockSpec((1, t, hb * d), lambda bi, g, i: (bi, i, g)),
        scratch_shapes=[pltpu.VMEM((hb, t, d), jnp.float32)] * 3,
        compiler_params=_cparams(("parallel", "parallel", "arbitrary")),
        name=name,
    )(*args)


def _band_kernel(*refs, hb, kv_heads, t, wpad, window, seg_len, has_lse):
    q_ref, k_ref, v_ref, o_ref = refs[:4]
    lse_ref = refs[4] if has_lse else None
    s_len = k_ref.shape[2]
    span = t + wpad
    t0 = pl.program_id(2) * t
    seg0 = (t0 // seg_len) * seg_len
    start = jnp.minimum(jnp.maximum(t0 - wpad, seg0), s_len - span)
    start = pl.multiple_of(start, HEAD_DIM)
    keys = pl.ds(start, span)
    qpos = t0 + lax.broadcasted_iota(jnp.int32, (t, span), 0)
    kpos = start + lax.broadcasted_iota(jnp.int32, (t, span), 1)
    dist = qpos - kpos
    mask = (dist >= 0) & (dist <= window) & (kpos >= seg0)
    kv_of = [hh * kv_heads // hb for hh in range(hb)]
    scores = [jnp.where(mask, _dot_nt(q_ref[0, hh], k_ref[0, kv_of[hh], keys, :]), NEG) for hh in range(hb)]
    probs = []
    for s in scores:
        m = _rowmax_lanes(s)
        p = jnp.exp2(s - _tile_lanes(m, span))
        l = jnp.broadcast_to(jnp.sum(_fold_lanes(p), axis=-1, keepdims=True), m.shape)
        probs.append((m, l, p.astype(v_ref.dtype)))
    lane = lax.broadcasted_iota(jnp.int32, (t, HEAD_DIM), 1)
    lse_all = jnp.zeros((t, HEAD_DIM), jnp.float32)
    for hh, (m, l, p) in enumerate(probs):
        out = _dot(p, v_ref[0, kv_of[hh], keys, :]) / l
        if has_lse:
            o_ref[0, hh] = out.astype(o_ref.dtype)
            lse_all = jnp.where(lane == hh, m + jnp.log2(l), lse_all)
        else:
            o_ref[0, :, hh * HEAD_DIM:(hh + 1) * HEAD_DIM] = out.astype(o_ref.dtype)
    if has_lse:
        lse_ref[0, 0] = lse_all


def _band(src, *, q_head0, k_head0, v_head0, hb, kv_heads, t, wpad, window, seg_len, has_lse, name):
    b, _, s, d = src.shape
    groups = H_MIX // hb
    in_specs = [pl.BlockSpec((1, hb, t, d), lambda bi, g, i: (bi, q_head0 // hb + g, i, 0)),
                pl.BlockSpec((1, kv_heads, s, d), lambda bi, g, i: (bi, k_head0 // kv_heads + g, 0, 0)),
                pl.BlockSpec((1, kv_heads, s, d), lambda bi, g, i: (bi, v_head0 // kv_heads + g, 0, 0))]
    if has_lse:
        out_shape = [jax.ShapeDtypeStruct((b, H_MIX, s, d), jnp.float32),
                     jax.ShapeDtypeStruct((b, groups, s, d), jnp.float32)]
        out_specs = [pl.BlockSpec((1, hb, t, d), lambda bi, g, i: (bi, g, i, 0)),
                     pl.BlockSpec((1, 1, t, d), lambda bi, g, i: (bi, g, i, 0))]
    else:
        out_shape = jax.ShapeDtypeStruct((b, s, H_MIX * d), jnp.float32)
        out_specs = pl.BlockSpec((1, t, hb * d), lambda bi, g, i: (bi, i, g))
    return pl.pallas_call(
        functools.partial(_band_kernel, hb=hb, kv_heads=kv_heads, t=t, wpad=wpad, window=window,
                          seg_len=seg_len, has_lse=has_lse),
        out_shape=out_shape,
        grid=(b, groups, s // t),
        in_specs=in_specs,
        out_specs=out_specs,
        compiler_params=_cparams(("parallel", "parallel", "arbitrary")),
        name=name,
    )(src, src, src)


def _stickbreak_kernel(q_ref, k_ref, v_ref, u2_ref, o_ref, run_scr, acc_scr, *, hb, t):
    i = pl.program_id(2)
    run_scr[...] = jnp.zeros(run_scr.shape, jnp.float32)
    acc_scr[...] = jnp.zeros(acc_scr.shape, jnp.float32)
    u2 = u2_ref[...]
    sign_bit = jnp.uint32(0x80000000)

    def tile(j, diagonal):
        keys = pl.ds(pl.multiple_of(j * t, t), t)
        if diagonal:
            strict = (lax.broadcasted_iota(jnp.int32, (t, t), 1) < lax.broadcasted_iota(jnp.int32, (t, t), 0))
        zs = [_dot_nt(q_ref[0, hh], k_ref[0, hh, keys, :]) for hh in range(hb)]
        stage = []
        for z2 in zs:
            neg_abs = lax.bitcast_convert_type(lax.bitcast_convert_type(z2, jnp.uint32) | sign_bit, jnp.float32)
            lg = jnp.log2(1.0 + jnp.exp2(neg_abs))
            sp2 = jnp.maximum(z2, 0.0) + lg
            if diagonal:
                sp2 = jnp.where(strict, sp2, 0.0)
            hi = sp2.astype(jnp.bfloat16)
            lo = (sp2 - hi.astype(jnp.float32)).astype(jnp.bfloat16)
            stage.append((jnp.minimum(z2, 0.0) - lg, sp2[:, 0:1], jnp.concatenate([hi, lo], axis=1)))
        laters = [_dot(hl, u2) for (_, _, hl) in stage]
        weights = []
        for hh, later in enumerate(laters):
            log_beta, first, _ = stage[hh]
            run = run_scr[hh]
            a = jnp.exp2(log_beta - (later + _tile_lanes(run, t)))
            if diagonal:
                a = jnp.where(strict, a, 0.0)
            run_scr[hh] = run + jnp.broadcast_to(later[:, 0:1] + first, run.shape)
            weights.append(a.astype(v_ref.dtype))
        for hh, a in enumerate(weights):
            acc_scr[hh] += _dot(a, v_ref[0, hh, keys, :])

    def off_diagonal(n, carry):
        tile(i - 1 - n, False)
        return carry

    tile(i, True)
    lax.fori_loop(0, i, off_diagonal, 0)
    for hh in range(hb):
        o_ref[0, :, hh * HEAD_DIM:(hh + 1) * HEAD_DIM] = acc_scr[hh].astype(o_ref.dtype)


def _stickbreak(hm, *, q_head0, k_head0, v_head0, hb, t, out_dtype):
    b, _, s, d = hm.shape
    later = np.arange(t)[:, None] > np.arange(t)[None, :]
    u2 = jnp.asarray(np.concatenate([later, later], axis=0), jnp.bfloat16)

    def heads(h0):
        return pl.BlockSpec((1, hb, s, d), lambda bi, g, i: (bi, h0 // hb + g, 0, 0))

    return pl.pallas_call(
        functools.partial(_stickbreak_kernel, hb=hb, t=t),
        out_shape=jax.ShapeDtypeStruct((b, s, H_MIX * d), out_dtype),
        grid=(b, H_MIX // hb, s // t),
        in_specs=[pl.BlockSpec((1, hb, t, d), lambda bi, g, i: (bi, q_head0 // hb + g, i, 0)),
                  heads(k_head0), heads(v_head0),
                  pl.BlockSpec((2 * t, t), lambda bi, g, i: (0, 0))],
        out_specs=pl.BlockSpec((1, t, hb * d), lambda bi, g, i: (bi, i, g)),
        scratch_shapes=[pltpu.VMEM((hb, t, d), jnp.float32), pltpu.VMEM((hb, t, d), jnp.float32)],
        compiler_params=_cparams(("parallel", "parallel", "arbitrary")),
        name="stickbreak",
    )(hm, hm, hm, u2)


def _gelu_tanh(x):
    return 0.5 * x * (1.0 + jnp.tanh(math.sqrt(2.0 / math.pi) * (x + 0.044715 * (x * x * x))))


def _compress_kernel(x_ref, pe_ref, w1_ref, w2_ref, o_ref):
    half = x_ref.shape[3]
    x = x_ref[0, 0]
    w1 = w1_ref[0].astype(jnp.bfloat16)
    first = _dot(x, w1[:half])
    second = _dot(x, w1[half:])
    pe = jnp.broadcast_to(pe_ref[0], (8, pe_ref.shape[2])).astype(jnp.bfloat16)
    pe_term = _dot(pe, w1)[0:1]
    n_chunks = x.shape[0]
    hid = first + pltpu.roll(second, n_chunks - 1, 0) + pe_term
    act = _gelu_tanh(hid)
    o_ref[0, 0] = _dot(act.astype(jnp.bfloat16), w2_ref[0].astype(jnp.bfloat16))


def _compress(chunks, pe, w1, w2):
    b, nh, nc, half = chunks.shape
    return pl.pallas_call(
        _compress_kernel,
        out_shape=jax.ShapeDtypeStruct((b, nh, nc, HEAD_DIM), jnp.float32),
        grid=(b, nh),
        in_specs=[pl.BlockSpec((1, 1, nc, half), lambda bi, h: (bi, h, 0, 0)),
                  pl.BlockSpec((1, 1, 2 * half), lambda bi, h: (h // NSA_KV, 0, 0)),
                  pl.BlockSpec((1, 2 * half, w1.shape[2]), lambda bi, h: (h // NSA_KV, 0, 0)),
                  pl.BlockSpec((1, w2.shape[1], HEAD_DIM), lambda bi, h: (h // NSA_KV, 0, 0))],
        out_specs=pl.BlockSpec((1, 1, nc, HEAD_DIM), lambda bi, h: (bi, h, 0, 0)),
        compiler_params=_cparams(("parallel", "parallel")),
        name="nsa_compress",
    )(chunks, pe, w1, w2)


def _cmp_select_kernel(q_ref, kc_ref, vc_ref, c2s_t_ref, o_ref, sel_ref, *, tq, n_cmp):
    i = pl.program_id(2)
    nc = kc_ref.shape[2]
    kc = kc_ref[0, 0].astype(jnp.bfloat16)
    vc = vc_ref[0, 0].astype(jnp.bfloat16)
    t = i * tq + lax.broadcasted_iota(jnp.int32, (1, tq), 1)
    cidx = lax.broadcasted_iota(jnp.int32, (nc, 1), 0)
    visible = (cidx * CMP_STRIDE + (CMP_LEN - 1) <= t) & (cidx < n_cmp)
    psum = jnp.zeros((nc, tq), jnp.float32)
    for hh in range(NSA_HPG):
        s = jnp.where(visible, _dot_nt(kc, q_ref[0, hh]), NEG)
        m = jnp.max(s, axis=0, keepdims=True)
        e = jnp.where(visible, jnp.exp2(s - m), 0.0)
        l = jnp.sum(e, axis=0, keepdims=True)
        p = e / jnp.where(l > 0.0, l, 1.0)
        psum = psum + p
        o_ref[0, :, hh * HEAD_DIM:(hh + 1) * HEAD_DIM] = _dot(p.T.astype(jnp.bfloat16), vc)

    imp = jnp.dot(c2s_t_ref[...], psum, preferred_element_type=jnp.float32, precision=lax.Precision.HIGHEST)
    w = imp.shape[0]
    n_slc = w // 2
    score_rows = slice(0, n_slc)
    jblk = lax.broadcasted_iota(jnp.int32, (n_slc, 1), 0)
    cur = lax.shift_right_logical(t, int(math.log2(SLC_LEN)))
    forced = (jblk == 0) | (jblk == cur) | (jblk == cur - 1)
    causal_blk = jblk * SLC_LEN <= t
    score = jnp.where(causal_blk, jnp.where(forced, FORCED_SCORE, imp[score_rows]), -1.0)
    rank = jnp.zeros((n_slc, tq), jnp.float32)
    for jp in range(n_slc):
        row = score[jp:jp + 1, :]
        before = (row > score) | ((row == score) & (jblk > jp))
        rank = rank + jnp.where(before, 1.0, 0.0)
    chosen = (rank < float(min(TOPK, n_slc))) & causal_blk
    add_mask = jnp.concatenate([jnp.where(chosen, 0.0, NEG), jnp.zeros((w - n_slc, tq), jnp.float32)], axis=0)
    sel_ref[0, 0] = add_mask.T.astype(sel_ref.dtype)


def _cmp_select(hm, cmp_kv, c2s, *, tq, n_cmp):
    b, _, s, d = hm.shape
    nc = cmp_kv.shape[2]
    w = c2s.shape[1]
    return pl.pallas_call(
        functools.partial(_cmp_select_kernel, tq=tq, n_cmp=n_cmp),
        out_shape=[jax.ShapeDtypeStruct((b, s, H_MIX * d), jnp.float32),
                   jax.ShapeDtypeStruct((b, NSA_KV, s, w), jnp.bfloat16)],
        grid=(b, NSA_KV, s // tq),
        in_specs=[pl.BlockSpec((1, NSA_HPG, tq, d), lambda bi, g, i: (bi, g, i, 0)),
                  pl.BlockSpec((1, 1, nc, d), lambda bi, g, i: (bi, g, 0, 0)),
                  pl.BlockSpec((1, 1, nc, d), lambda bi, g, i: (bi, NSA_KV + g, 0, 0)),
                  pl.BlockSpec((w, nc), lambda bi, g, i: (0, 0))],
        out_specs=[pl.BlockSpec((1, tq, NSA_HPG * d), lambda bi, g, i: (bi, i, g)),
                   pl.BlockSpec((1, 1, tq, w), lambda bi, g, i: (bi, g, i, 0))],
        compiler_params=_cparams(("parallel", "parallel", "parallel")),
        name="nsa_cmp_select",
    )(hm, cmp_kv, cmp_kv, c2s.T)


def _nsa_combine_kernel(oc_ref, os_ref, ow_ref, g_ref, o_ref):
    gates = jax.nn.sigmoid(g_ref[0])
    for h in range(H_MIX):
        cols = slice(h * HEAD_DIM, (h + 1) * HEAD_DIM)
        c0 = GATE_COL0 + 3 * h
        out = (gates[:, c0:c0 + 1] * oc_ref[0, :, cols]
               + gates[:, c0 + 1:c0 + 2] * os_ref[0, :, cols]
               + gates[:, c0 + 2:c0 + 3] * ow_ref[0, :, cols])
        o_ref[0, :, cols] = out.astype(o_ref.dtype)


def _nsa_combine(o_cmp, o_slc, o_win, small, tq=COMBINE_TQ):
    b, s, w = o_cmp.shape
    big = pl.BlockSpec((1, tq, w), lambda bi, i: (bi, i, 0))
    return pl.pallas_call(
        _nsa_combine_kernel,
        out_shape=jax.ShapeDtypeStruct((b, s, w), jnp.bfloat16),
        grid=(b, s // tq),
        in_specs=[big, big, big, pl.BlockSpec((1, tq, small.shape[2]), lambda bi, i: (bi, i, 0))],
        out_specs=big,
        compiler_params=_cparams(("parallel", "parallel")),
        name="nsa_combine",
    )(o_cmp, o_slc, o_win, small)


def _dil_combine_kernel(*refs):
    n = (len(refs) - 1) // 2
    o_refs, lse_refs, out_ref = refs[:n], refs[n:2 * n], refs[-1]
    for h in range(H_MIX):
        lses = []
        for lse_ref in lse_refs:
            hb = H_MIX // lse_ref.shape[1]
            lses.append(lse_ref[0, h // hb][:, h % hb:h % hb + 1])
        m = functools.reduce(jnp.maximum, lses)
        es = [jnp.exp2(l - m) for l in lses]
        tot = functools.reduce(lambda a, b: a + b, es)
        out = functools.reduce(lambda a, b: a + b, [o_ref[0, h] * (e / tot) for o_ref, e in zip(o_refs, es)])
        out_ref[0, :, h * HEAD_DIM:(h + 1) * HEAD_DIM] = out.astype(out_ref.dtype)


def _dil_combine(outs, lses, tq=COMBINE_TQ):
    b, h, s, d = outs[0].shape
    o_spec = pl.BlockSpec((1, h, tq, d), lambda bi, i: (bi, 0, i, 0))
    lse_specs = [pl.BlockSpec((1, l.shape[1], tq, d), lambda bi, i: (bi, 0, i, 0)) for l in lses]
    return pl.pallas_call(
        _dil_combine_kernel,
        out_shape=jax.ShapeDtypeStruct((b, s, h * d), jnp.bfloat16),
        grid=(b, s // tq),
        in_specs=[o_spec] * len(outs) + lse_specs,
        out_specs=pl.BlockSpec((1, tq, h * d), lambda bi, i: (bi, i, 0)),
        compiler_params=_cparams(("parallel", "parallel")),
        name="dil_combine",
    )(*outs, *lses)


def _to_residue_major(t, dil):
    b, h, s, d = t.shape
    return t.reshape(b, h, s // dil, dil, d).transpose(0, 1, 3, 2, 4).reshape(b, h, s, d)


def _from_residue_major(t, dil):
    b, h, s, d = t.shape
    return t.reshape(b, h, dil, s // dil, d).transpose(0, 1, 3, 2, 4).reshape(b, h, s, d)


def _cmp_to_slc_matrix(n_chunks, n_cmp, n_slc):
    ratio = SLC_LEN // CMP_STRIDE
    span = CMP_LEN // CMP_STRIDE
    jj, mm, nn = np.meshgrid(np.arange(n_slc), np.arange(ratio), np.arange(span), indexing="ij")
    cc = ratio * jj + mm + nn
    keep = cc < n_cmp
    mat = np.zeros((n_chunks, 2 * n_slc), np.float32)
    np.add.at(mat, (cc[keep], jj[keep]), 1.0)
    return jnp.asarray(mat)


def _mixers(hm, small, fox_bf, cmp_pe_k, cmp_w1_k, cmp_w2_k, cmp_pe_v, cmp_w1_v, cmp_w2_v):
    b, _, s, d = hm.shape
    bf16 = jnp.bfloat16

    n_chunks = s // CMP_STRIDE
    n_cmp = (s - CMP_LEN) // CMP_STRIDE + 1
    n_slc = s // SLC_LEN
    chunks = hm[:, HD_KC:HD_KC + 2 * NSA_KV].reshape(b, 2 * NSA_KV, n_chunks, CMP_STRIDE * d)
    cmp_kv = _compress(chunks,
                       jnp.stack([cmp_pe_k, cmp_pe_v]).reshape(2, 1, CMP_LEN * d),
                       jnp.stack([cmp_w1_k, cmp_w1_v]), jnp.stack([cmp_w2_k, cmp_w2_v]))
    o_cmp, sel = _cmp_select(hm, cmp_kv, _cmp_to_slc_matrix(n_chunks, n_cmp, n_slc), tq=CMP_TQ, n_cmp=n_cmp)
    onehot = jnp.asarray((np.arange(s)[:, None] // SLC_LEN) == np.arange(2 * n_slc)[None, :], bf16)
    o_slc = _causal_flash(hm, mode="sel", q_head0=HD_QA, k_head0=HD_KS, v_head0=HD_VS, hb=NSA_HPG, t=FLASH_T,
                          sel=sel, onehot=onehot, out_dtype=jnp.float32, name="nsa_selected")
    o_win = _band(hm, q_head0=HD_QA, k_head0=HD_KW, v_head0=HD_VW, hb=NSA_HPG, kv_heads=1, t=BAND_T, wpad=NSA_WINDOW,
                  window=NSA_WINDOW - 1, seg_len=s, has_lse=False, name="nsa_window")
    o_a = _nsa_combine(o_cmp, o_slc, o_win, small)

    t_fox = FLASH_T
    bias_row = jnp.zeros((1, small.shape[2]), jnp.float32).at[0, FORGET_COL0:FORGET_COL0 + H_MIX].set(fox_bf)
    csum = _logf_cumsum(small, bias_row)
    key_bias = (-LOG2E) * csum[:, :, FORGET_COL0:FORGET_COL0 + H_MIX].transpose(0, 2, 1)
    o_b = _causal_flash(hm, mode="fox", q_head0=HD_QB, k_head0=HD_KB, v_head0=HD_VB, hb=HEADS_PER_STEP, t=t_fox,
                        bias=key_bias.reshape(b, H_MIX, s // t_fox, 1, t_fox), out_dtype=bf16, name="fox")

    outs, lses = [], []
    for window, dil in DIL_CONFIGS:
        if dil == 1:
            src, heads0 = hm, (HD_QC, HD_KC2, HD_VC2)
        else:
            src, heads0 = _to_residue_major(hm[:, HD_QC:HD_QC + 3 * H_MIX], dil), (0, H_MIX, 2 * H_MIX)
        hb = HEADS_PER_STEP if dil == 1 else H_MIX
        o, lse = _band(src, q_head0=heads0[0], k_head0=heads0[1], v_head0=heads0[2], hb=hb, kv_heads=hb, t=BAND_T,
                       wpad=HEAD_DIM, window=window // dil, seg_len=s // dil, has_lse=True, name=f"dilated_{dil}")
        if dil != 1:
            o, lse = _from_residue_major(o, dil), _from_residue_major(lse, dil)
        outs.append(o)
        lses.append(lse)
    o_c = _dil_combine(outs, lses)

    o_d = _stickbreak(hm, q_head0=HD_QD, k_head0=HD_KD, v_head0=HD_VD, hb=HEADS_PER_STEP, t=SB_T, out_dtype=bf16)

    return o_a, o_b, o_c, o_d


def _small_w_in_t(w_in_t):
    pad = jnp.zeros((w_in_t.shape[0], HEAD_DIM - N_GATES - H_MIX, w_in_t.shape[2]), w_in_t.dtype)
    return jnp.concatenate([w_in_t[:, GATES_AT:GATES_AT + N_GATES], w_in_t[:, FORGET_AT:FORGET_AT + H_MIX], pad],
                           axis=1)


def _layer(h, layer, p, cos, sin, batch, norm_attn, w_in_t, w_small_t, fox_bf, cmp_pe_k, cmp_w1_k, cmp_w2_k,
           cmp_pe_v, cmp_w1_v, cmp_w2_v, w_o, norm_mlp, w_up, w_down, norm_ple, w_ple_gate, w_ple_proj):
    m, d = h.shape
    s = m // batch
    bf16 = jnp.bfloat16
    xn = _rmsnorm(h, norm_attn, bf16)
    hm = _in_proj_heads(xn, w_in_t, layer, cos, sin, batch)
    small = _in_proj_small(xn, w_small_t, layer)
    mix = _mixers(hm, small.reshape(batch, s, HEAD_DIM), fox_bf,
                  cmp_pe_k, cmp_w1_k, cmp_w2_k, cmp_pe_v, cmp_w1_v, cmp_w2_v)

    tm, tn = MM_TM, MM_TN
    res_spec = pl.BlockSpec((tm, tn), lambda i, j: (i, j))
    h = _out_proj([o.reshape(m, -1) for o in mix], w_o, layer, h, tm=tm, tn=tn)

    x2 = _rmsnorm(h, norm_mlp, bf16)
    mid = _matmul(x2, w_up, layer, bf16, _epi_relu2, tm=tm, tn=tn, name="mlp_up")
    h = _matmul_residual_ksplit(mid, w_down, layer, h, name="mlp_down")

    x3 = _rmsnorm(h, norm_ple, bf16)
    ple_dim = p.shape[2]
    h = _matmul(x3, w_ple_gate, layer, jnp.float32, _epi_ple, (h, p, w_ple_proj),
                (res_spec, pl.BlockSpec((None, tm, ple_dim), lambda i, j: (layer, i, 0)),
                 pl.BlockSpec((None, ple_dim, tn), lambda i, j: (layer, 0, j))),
                tm=tm, tn=tn, name="ple_gate")
    return h


def kernel(x, p, positions, norm_attn, w_in, fox_bf, cmp_pe_k, cmp_w1_k, cmp_w2_k, cmp_pe_v, cmp_w1_v, cmp_w2_v,
           w_o, norm_mlp, w_up, w_down, norm_ple, w_ple_gate, w_ple_proj, norm_final):
    batch, s, d = x.shape
    depth = p.shape[0]
    cos, sin = _rope_tables(positions)
    w_in_t = jnp.swapaxes(w_in, 1, 2)
    w_small_t = _small_w_in_t(w_in_t)
    p = p.reshape(depth, batch * s, -1)
    h = x.reshape(batch * s, d)
    for i in range(depth):
        h = _layer(h, i, p, cos, sin, batch, norm_attn[i], w_in_t, w_small_t, fox_bf[i],
                   cmp_pe_k[i], cmp_w1_k[i], cmp_w2_k[i], cmp_pe_v[i], cmp_w1_v[i], cmp_w2_v[i],
                   w_o, norm_mlp[i], w_up, w_down, norm_ple[i], w_ple_gate, w_ple_proj)
    return _rmsnorm(h, norm_final, x.dtype).reshape(batch, s, d)
```

```python
import functools
import math

import numpy as np
import jax
import jax.numpy as jnp
from jax import lax
from jax.experimental import pallas as pl
from jax.experimental.pallas import tpu as pltpu

HEAD_DIM = 128
H_MIX = 8
NSA_KV = 2
NSA_HPG = H_MIX // NSA_KV
CMP_LEN = 32
CMP_STRIDE = 16
SLC_LEN = 64
TOPK = 16
NSA_WINDOW = 512
DIL_CONFIGS = ((128, 1), (512, 4), (2048, 16))
ROPE_THETA = 10000.0
RMS_EPS = 1e-6
NEG = -1e30
M_INIT = -5e29
FORCED_SCORE = 1e9
LOG2E = math.log2(math.e)
Q_SCALE = HEAD_DIM ** -0.5 * LOG2E

VMEM_LIMIT_BYTES = 52 * 1024 * 1024

MXU_DIM = 256
ROW_TILE = 512
MM_TM, MM_TN = 1024, 512
DOWN_TM, DOWN_TN, DOWN_TK = 2048, 1024, 1024
FLASH_T = 512
SB_T = MXU_DIM
BAND_T = 256
CMP_TQ = 512
COMBINE_TQ = 512
CUMSUM_BLK = 128
HEADS_PER_STEP = 4

HD_QA, HD_KC, HD_VC, HD_KS, HD_VS, HD_KW, HD_VW = 0, 8, 10, 12, 14, 16, 18
HD_QB, HD_KB, HD_VB = 20, 28, 36
HD_QC, HD_KC2, HD_VC2 = 44, 52, 60
HD_QD, HD_KD, HD_VD = 68, 76, 84
N_HEADS_ALL = 92
N_BIG = N_HEADS_ALL * HEAD_DIM
GATE_COL0 = 0
FORGET_COL0 = 24
N_GATES = 3 * H_MIX
GATES_AT = (H_MIX + 6 * NSA_KV) * HEAD_DIM
FORGET_AT = GATES_AT + N_GATES + 3 * H_MIX * HEAD_DIM
_ROPED_HEADS = (tuple(range(HD_QA, HD_QA + H_MIX)) + tuple(range(HD_KS, HD_KS + NSA_KV))
                + tuple(range(HD_KW, HD_KW + NSA_KV)) + tuple(range(HD_QC, HD_QC + 2 * H_MIX)))
_QUERY_HEADS = tuple(h for q0 in (HD_QA, HD_QB, HD_QC, HD_QD) for h in range(q0, q0 + H_MIX))


def _cparams(sem):
    return pltpu.CompilerParams(dimension_semantics=sem, vmem_limit_bytes=VMEM_LIMIT_BYTES)


def _dot_nt(a, b):
    return lax.dot_general(a, b, (((1,), (1,)), ((), ())), preferred_element_type=jnp.float32)


def _dot(a, b):
    return jnp.dot(a, b, preferred_element_type=jnp.float32)


def _rmsnorm_kernel(x_ref, g_ref, o_ref):
    x = x_ref[...]
    ms = jnp.mean(x * x, axis=-1, keepdims=True)
    o_ref[...] = (x * lax.rsqrt(ms + RMS_EPS) * g_ref[...]).astype(o_ref.dtype)


def _rmsnorm(x, g, out_dtype, tm=ROW_TILE):
    m, d = x.shape
    return pl.pallas_call(
        _rmsnorm_kernel,
        out_shape=jax.ShapeDtypeStruct((m, d), out_dtype),
        grid=(m // tm,),
        in_specs=[pl.BlockSpec((tm, d), lambda i: (i, 0)),
                  pl.BlockSpec((1, d), lambda i: (0, 0))],
        out_specs=pl.BlockSpec((tm, d), lambda i: (i, 0)),
        compiler_params=_cparams(("parallel",)),
        name="rmsnorm",
    )(x, g.reshape(1, d))


def _mm_kernel(*refs, n_extra, epilogue):
    x_ref, w_ref = refs[0], refs[1]
    extras = refs[2:2 + n_extra]
    o_ref = refs[2 + n_extra]
    acc = _dot(x_ref[...].astype(jnp.bfloat16), w_ref[...].astype(jnp.bfloat16))
    o_ref[...] = epilogue(acc, *extras).astype(o_ref.dtype)


def _epi_none(acc):
    return acc


def _epi_relu2(acc):
    r = jnp.maximum(acc, 0.0)
    return r * r


def _epi_ple(acc, h_ref, p_ref, wp_ref):
    pp = _dot(p_ref[...].astype(jnp.bfloat16), wp_ref[...].astype(jnp.bfloat16))
    return h_ref[...] + jax.nn.sigmoid(acc) * pp


def _matmul(x, w, layer, out_dtype, epilogue=_epi_none, extras=(), extra_specs=(), tm=MM_TM, tn=MM_TN, name="matmul"):
    m, kdim = x.shape
    n = w.shape[2]
    tn = min(tn, n)
    return pl.pallas_call(
        functools.partial(_mm_kernel, n_extra=len(extras), epilogue=epilogue),
        out_shape=jax.ShapeDtypeStruct((m, n), out_dtype),
        grid=(m // tm, n // tn),
        in_specs=[pl.BlockSpec((tm, kdim), lambda i, j: (i, 0)),
                  pl.BlockSpec((None, kdim, tn), lambda i, j: (layer, 0, j)),
                  *extra_specs],
        out_specs=pl.BlockSpec((tm, tn), lambda i, j: (i, j)),
        compiler_params=_cparams(("parallel", "arbitrary")),
        name=name,
    )(x, w, *extras)


def _out_proj_kernel(*refs):
    *x_refs, w_ref, h_ref, o_ref = refs
    acc = h_ref[...]
    row = 0
    for x_ref in x_refs:
        width = x_ref.shape[1]
        acc = acc + _dot(x_ref[...], w_ref[row:row + width, :].astype(jnp.bfloat16))
        row += width
    o_ref[...] = acc


def _out_proj(xs, w, layer, h, tm, tn):
    m, n = h.shape
    return pl.pallas_call(
        _out_proj_kernel,
        out_shape=jax.ShapeDtypeStruct((m, n), jnp.float32),
        grid=(m // tm, n // tn),
        in_specs=[*[pl.BlockSpec((tm, x.shape[1]), lambda i, j: (i, 0)) for x in xs],
                  pl.BlockSpec((None, w.shape[1], tn), lambda i, j: (layer, 0, j)),
                  pl.BlockSpec((tm, tn), lambda i, j: (i, j))],
        out_specs=pl.BlockSpec((tm, tn), lambda i, j: (i, j)),
        compiler_params=_cparams(("parallel", "arbitrary")),
        name="out_proj",
    )(*xs, w, h)


def _mm_residual_ksplit_kernel(x_ref, w_ref, h_ref, o_ref):
    @pl.when(pl.program_id(2) == 0)
    def _():
        o_ref[...] = h_ref[...] + _dot(x_ref[...], w_ref[...].astype(jnp.bfloat16))

    @pl.when(pl.program_id(2) > 0)
    def _():
        o_ref[...] += _dot(x_ref[...], w_ref[...].astype(jnp.bfloat16))


def _matmul_residual_ksplit(x, w, layer, h, tm=DOWN_TM, tn=DOWN_TN, tk=DOWN_TK, name="matmul_ksplit"):
    m, kdim = x.shape
    n = w.shape[2]
    return pl.pallas_call(
        _mm_residual_ksplit_kernel,
        out_shape=jax.ShapeDtypeStruct((m, n), jnp.float32),
        grid=(m // tm, n // tn, kdim // tk),
        in_specs=[pl.BlockSpec((tm, tk), lambda i, j, k: (i, k)),
                  pl.BlockSpec((None, tk, tn), lambda i, j, k: (layer, k, j)),
                  pl.BlockSpec((tm, tn), lambda i, j, k: (i, j), pipeline_mode=pl.Buffered(1))],
        out_specs=pl.BlockSpec((tm, tn), lambda i, j, k: (i, j)),
        compiler_params=_cparams(("parallel", "parallel", "arbitrary")),
        name=name,
    )(x, w, h)


def _rope_table_kernel(pos_ref, freq_ref, sign_ref, cos_ref, sin_ref):
    ang = pos_ref[...] * freq_ref[...]
    cos_ref[...] = jnp.cos(ang)
    sin_ref[...] = jnp.sin(ang) * sign_ref[...]


def _rope_tables(positions, ts=COMBINE_TQ):
    n = positions.size
    half = HEAD_DIM // 2
    inv_freq = ROPE_THETA ** (-jnp.arange(half, dtype=jnp.float32) / half)
    freq = jnp.concatenate([inv_freq, inv_freq]).reshape(1, HEAD_DIM)
    sign = jnp.concatenate([-jnp.ones((half,), jnp.float32), jnp.ones((half,), jnp.float32)]).reshape(1, HEAD_DIM)
    pos = positions.astype(jnp.float32).reshape(n, 1)
    row = pl.BlockSpec((1, HEAD_DIM), lambda i: (0, 0))
    return pl.pallas_call(
        _rope_table_kernel,
        out_shape=[jax.ShapeDtypeStruct((n, HEAD_DIM), jnp.float32)] * 2,
        grid=(n // ts,),
        in_specs=[pl.BlockSpec((ts, 1), lambda i: (i, 0)), row, row],
        out_specs=[pl.BlockSpec((ts, HEAD_DIM), lambda i: (i, 0))] * 2,
        compiler_params=_cparams(("parallel",)),
        name="rope_tables",
    )(pos, freq, sign)


_FLAG_ROPED, _FLAG_QUERY = 1, 2


def _in_proj_kernel(flags_ref, row0_ref, x_ref, wt_ref, cos_ref, sin_ref, o_ref):
    j = pl.program_id(1)
    heads = o_ref.shape[1]
    acc = _dot_nt(x_ref[...], wt_ref[0].astype(jnp.bfloat16))
    for hh in range(heads):
        t = acc[:, hh * HEAD_DIM:(hh + 1) * HEAD_DIM]
        flags = flags_ref[j * heads + hh]
        roped = (flags & _FLAG_ROPED) > 0
        scale = jnp.where((flags & _FLAG_QUERY) > 0, Q_SCALE, 1.0)
        cos = jnp.where(roped, cos_ref[...], 1.0)
        sin = jnp.where(roped, sin_ref[...], 0.0)
        o_ref[0, hh] = ((t * cos + pltpu.roll(t, HEAD_DIM // 2, 1) * sin) * scale).astype(o_ref.dtype)


def _in_proj_heads(xn, w_in_t, layer, cos, sin, batch, tm=MM_TM, tn=MM_TN):
    m, d = xn.shape
    s = m // batch
    heads = tn // HEAD_DIM
    q_tiles = s // tm
    flags = np.zeros((N_HEADS_ALL,), np.int32)
    flags[list(_ROPED_HEADS)] |= _FLAG_ROPED
    flags[list(_QUERY_HEADS)] |= _FLAG_QUERY
    col = np.arange(0, N_BIG, tn)
    row0 = col + np.where(col >= GATES_AT, N_GATES, 0) + np.where(col >= FORGET_AT - N_GATES, H_MIX, 0)
    table = pl.BlockSpec((tm, HEAD_DIM), lambda i, j, f, r: (i, 0))
    return pl.pallas_call(
        _in_proj_kernel,
        out_shape=jax.ShapeDtypeStruct((batch, N_HEADS_ALL, s, HEAD_DIM), jnp.bfloat16),
        grid_spec=pltpu.PrefetchScalarGridSpec(
            num_scalar_prefetch=2,
            grid=(m // tm, N_BIG // tn),
            in_specs=[pl.BlockSpec((tm, d), lambda i, j, f, r: (i, 0)),
                      pl.BlockSpec((pl.Element(1), pl.Element(tn), pl.Element(d)),
                                   lambda i, j, f, r: (layer, pl.multiple_of(r[j], 8), 0)),
                      table, table],
            out_specs=pl.BlockSpec((1, heads, tm, HEAD_DIM),
                                   lambda i, j, f, r: (i // q_tiles, j, i % q_tiles, 0)),
        ),
        compiler_params=_cparams(("parallel", "arbitrary")),
        name="in_proj",
    )(jnp.asarray(flags), jnp.asarray(row0.astype(np.int32)), xn, w_in_t, cos, sin)


def _in_proj_small_kernel(x_ref, wt_ref, o_ref):
    o_ref[...] = _dot_nt(x_ref[...], wt_ref[...].astype(jnp.bfloat16))


def _in_proj_small(xn, w_small_t, layer, tm=MM_TM):
    m, d = xn.shape
    n = w_small_t.shape[1]
    return pl.pallas_call(
        _in_proj_small_kernel,
        out_shape=jax.ShapeDtypeStruct((m, n), jnp.float32),
        grid=(m // tm,),
        in_specs=[pl.BlockSpec((tm, d), lambda i: (i, 0)),
                  pl.BlockSpec((None, n, d), lambda i: (layer, 0, 0))],
        out_specs=pl.BlockSpec((tm, n), lambda i: (i, 0)),
        compiler_params=_cparams(("parallel",)),
        name="in_proj_small",
    )(xn, w_small_t)


def _softplus(z):
    return jnp.maximum(z, 0.0) + jnp.log1p(jnp.exp(-jnp.abs(z)))


def _logf_cumsum_kernel(x_ref, bias_ref, o_ref, *, blk):
    s = x_ref.shape[1]
    r = lax.broadcasted_iota(jnp.int32, (blk, blk), 0)
    c = lax.broadcasted_iota(jnp.int32, (blk, blk), 1)
    tri = jnp.where(c <= r, 1.0, 0.0).astype(jnp.float32)

    def body(i, carry):
        x = x_ref[0, pl.ds(i * blk, blk), :]
        logf = -_softplus(-(x + bias_ref[...]))
        cs = jnp.dot(tri, logf, preferred_element_type=jnp.float32,
                     precision=lax.Precision.HIGHEST) + carry
        o_ref[0, pl.ds(i * blk, blk), :] = cs
        return cs[blk - 1:blk, :]

    lax.fori_loop(0, s // blk, body, jnp.zeros((1, x_ref.shape[2]), jnp.float32))


def _logf_cumsum(small, bias_row, blk=CUMSUM_BLK):
    b, s, w = small.shape
    return pl.pallas_call(
        functools.partial(_logf_cumsum_kernel, blk=blk),
        out_shape=jax.ShapeDtypeStruct((b, s, w), jnp.float32),
        grid=(b,),
        in_specs=[pl.BlockSpec((1, s, w), lambda bi: (bi, 0, 0)),
                  pl.BlockSpec((1, w), lambda bi: (0, 0))],
        out_specs=pl.BlockSpec((1, s, w), lambda bi: (bi, 0, 0)),
        compiler_params=_cparams(("parallel",)),
        name="logf_cumsum",
    )(small, bias_row)


def _rowmax_lanes(s):
    m = s[:, :HEAD_DIM]
    for c in range(1, s.shape[1] // HEAD_DIM):
        m = jnp.maximum(m, s[:, c * HEAD_DIM:(c + 1) * HEAD_DIM])
    return jnp.broadcast_to(jnp.max(m, axis=-1, keepdims=True), m.shape)


def _fold_lanes(p):
    a = p[:, :HEAD_DIM]
    for c in range(1, p.shape[1] // HEAD_DIM):
        a = a + p[:, c * HEAD_DIM:(c + 1) * HEAD_DIM]
    return a


def _tile_lanes(x, width):
    return jnp.concatenate([x] * (width // HEAD_DIM), axis=1)


def _causal_flash_kernel(*refs, mode, hb, t):
    it = iter(refs)
    q_ref, k_ref, v_ref = next(it), next(it), next(it)
    bias_ref = next(it) if mode == "fox" else None
    sel_ref, onehot_ref = (next(it), next(it)) if mode == "sel" else (None, None)
    o_ref = next(it)
    m_scr, l_scr, acc_scr = next(it), next(it), next(it)
    i = pl.program_id(2)

    m_scr[...] = jnp.full(m_scr.shape, M_INIT, jnp.float32)
    l_scr[...] = jnp.zeros(l_scr.shape, jnp.float32)
    acc_scr[...] = jnp.zeros(acc_scr.shape, jnp.float32)

    def tile(j, diagonal):
        keys = pl.ds(pl.multiple_of(j * t, t), t)
        if mode == "sel":
            k_shared = jnp.concatenate([k_ref[0, 0, keys, :], onehot_ref[keys, :]], axis=1)
            scores = [_dot_nt(jnp.concatenate([q_ref[0, hh], sel_ref[0, 0]], axis=1), k_shared)
                      for hh in range(hb)]
        else:
            scores = [_dot_nt(q_ref[0, hh], k_ref[0, hh, keys, :]) + bias_ref[0, hh, j] for hh in range(hb)]
        if diagonal:
            causal = (lax.broadcasted_iota(jnp.int32, (t, t), 1) <= lax.broadcasted_iota(jnp.int32, (t, t), 0))
            scores = [jnp.where(causal, s, NEG) for s in scores]
        probs = []
        for hh, s in enumerate(scores):
            m_prev = m_scr[hh]
            m_new = jnp.maximum(m_prev, _rowmax_lanes(s))
            alpha = jnp.exp2(m_prev - m_new)
            p = jnp.exp2(s - _tile_lanes(m_new, t))
            l_scr[hh] = alpha * l_scr[hh] + _fold_lanes(p)
            m_scr[hh] = m_new
            probs.append((alpha, p.astype(v_ref.dtype)))
        for hh, (alpha, p) in enumerate(probs):
            v = v_ref[0, 0 if mode == "sel" else hh, keys, :]
            acc_scr[hh] = alpha * acc_scr[hh] + _dot(p, v)

    def off_diagonal(j, carry):
        tile(j, False)
        return carry

    lax.fori_loop(0, i, off_diagonal, 0)
    tile(i, True)
    for hh in range(hb):
        l = jnp.sum(l_scr[hh], axis=-1, keepdims=True)
        o_ref[0, :, hh * HEAD_DIM:(hh + 1) * HEAD_DIM] = (acc_scr[hh] / l).astype(o_ref.dtype)


def _causal_flash(hm, *, mode, q_head0, k_head0, v_head0, hb, t, out_dtype, bias=None, sel=None, onehot=None,
                  name="causal_flash"):
    b, _, s, d = hm.shape
    groups = H_MIX // hb
    kv_heads = hb if mode == "fox" else 1
    in_specs = [pl.BlockSpec((1, hb, t, d), lambda bi, g, i: (bi, q_head0 // hb + g, i, 0)),
                pl.BlockSpec((1, kv_heads, s, d), lambda bi, g, i: (bi, k_head0 // kv_heads + g, 0, 0)),
                pl.BlockSpec((1, kv_heads, s, d), lambda bi, g, i: (bi, v_head0 // kv_heads + g, 0, 0))]
    args = [hm, hm, hm]
    if mode == "fox":
        in_specs.append(pl.BlockSpec((1, hb, s // t, 1, t), lambda bi, g, i: (bi, g, 0, 0, 0)))
        args.append(bias)
    else:
        in_specs.append(pl.BlockSpec((1, 1, t, sel.shape[-1]), lambda bi, g, i: (bi, g, i, 0)))
        in_specs.append(pl.BlockSpec(onehot.shape, lambda bi, g, i: (0, 0)))
        args += [sel, onehot]
    return pl.pallas_call(
        functools.partial(_causal_flash_kernel, mode=mode, hb=hb, t=t),
        out_shape=jax.ShapeDtypeStruct((b, s, H_MIX * d), out_dtype),
        grid=(b, groups, s // t),
        in_specs=in_specs,
        out_specs=pl.BlockSpec((1, t, hb * d), lambda bi, g, i: (bi, i, g)),
        scratch_shapes=[pltpu.VMEM((hb, t, d), jnp.float32)] * 3,
        compiler_params=_cparams(("parallel", "parallel", "arbitrary")),
        name=name,
    )(*args)


def _band_kernel(*refs, hb, kv_heads, t, wpad, window, seg_len, has_lse):
    q_ref, k_ref, v_ref, o_ref = refs[:4]
    lse_ref = refs[4] if has_lse else None
    s_len = k_ref.shape[2]
    span = t + wpad
    t0 = pl.program_id(2) * t
    seg0 = (t0 // seg_len) * seg_len
    start = jnp.minimum(jnp.maximum(t0 - wpad, seg0), s_len - span)
    start = pl.multiple_of(start, HEAD_DIM)
    keys = pl.ds(start, span)
    qpos = t0 + lax.broadcasted_iota(jnp.int32, (t, span), 0)
    kpos = start + lax.broadcasted_iota(jnp.int32, (t, span), 1)
    dist = qpos - kpos
    mask = (dist >= 0) & (dist <= window) & (kpos >= seg0)
    kv_of = [hh * kv_heads // hb for hh in range(hb)]
    scores = [jnp.where(mask, _dot_nt(q_ref[0, hh], k_ref[0, kv_of[hh], keys, :]), NEG) for hh in range(hb)]
    probs = []
    for s in scores:
        m = _rowmax_lanes(s)
        p = jnp.exp2(s - _tile_lanes(m, span))
        l = jnp.broadcast_to(jnp.sum(_fold_lanes(p), axis=-1, keepdims=True), m.shape)
        probs.append((m, l, p.astype(v_ref.dtype)))
    lane = lax.broadcasted_iota(jnp.int32, (t, HEAD_DIM), 1)
    lse_all = jnp.zeros((t, HEAD_DIM), jnp.float32)
    for hh, (m, l, p) in enumerate(probs):
        out = _dot(p, v_ref[0, kv_of[hh], keys, :]) / l
        if has_lse:
            o_ref[0, hh] = out.astype(o_ref.dtype)
            lse_all = jnp.where(lane == hh, m + jnp.log2(l), lse_all)
        else:
            o_ref[0, :, hh * HEAD_DIM:(hh + 1) * HEAD_DIM] = out.astype(o_ref.dtype)
    if has_lse:
        lse_ref[0, 0] = lse_all


def _band(src, *, q_head0, k_head0, v_head0, hb, kv_heads, t, wpad, window, seg_len, has_lse, name):
    b, _, s, d = src.shape
    groups = H_MIX // hb
    in_specs = [pl.BlockSpec((1, hb, t, d), lambda bi, g, i: (bi, q_head0 // hb + g, i, 0)),
                pl.BlockSpec((1, kv_heads, s, d), lambda bi, g, i: (bi, k_head0 // kv_heads + g, 0, 0)),
                pl.BlockSpec((1, kv_heads, s, d), lambda bi, g, i: (bi, v_head0 // kv_heads + g, 0, 0))]
    if has_lse:
        out_shape = [jax.ShapeDtypeStruct((b, H_MIX, s, d), jnp.float32),
                     jax.ShapeDtypeStruct((b, groups, s, d), jnp.float32)]
        out_specs = [pl.BlockSpec((1, hb, t, d), lambda bi, g, i: (bi, g, i, 0)),
                     pl.BlockSpec((1, 1, t, d), lambda bi, g, i: (bi, g, i, 0))]
    else:
        out_shape = jax.ShapeDtypeStruct((b, s, H_MIX * d), jnp.float32)
        out_specs = pl.BlockSpec((1, t, hb * d), lambda bi, g, i: (bi, i, g))
    return pl.pallas_call(
        functools.partial(_band_kernel, hb=hb, kv_heads=kv_heads, t=t, wpad=wpad, window=window,
                          seg_len=seg_len, has_lse=has_lse),
        out_shape=out_shape,
        grid=(b, groups, s // t),
        in_specs=in_specs,
        out_specs=out_specs,
        compiler_params=_cparams(("parallel", "parallel", "arbitrary")),
        name=name,
    )(src, src, src)


def _stickbreak_kernel(q_ref, k_ref, v_ref, u2_ref, o_ref, run_scr, acc_scr, *, hb, t):
    i = pl.program_id(2)
    run_scr[...] = jnp.zeros(run_scr.shape, jnp.float32)
    acc_scr[...] = jnp.zeros(acc_scr.shape, jnp.float32)
    u2 = u2_ref[...]
    sign_bit = jnp.uint32(0x80000000)

    def tile(j, diagonal):
        keys = pl.ds(pl.multiple_of(j * t, t), t)
        if diagonal:
            strict = (lax.broadcasted_iota(jnp.int32, (t, t), 1) < lax.broadcasted_iota(jnp.int32, (t, t), 0))
        zs = [_dot_nt(q_ref[0, hh], k_ref[0, hh, keys, :]) for hh in range(hb)]
        stage = []
        for z2 in zs:
            neg_abs = lax.bitcast_convert_type(lax.bitcast_convert_type(z2, jnp.uint32) | sign_bit, jnp.float32)
            lg = jnp.log2(1.0 + jnp.exp2(neg_abs))
            sp2 = jnp.maximum(z2, 0.0) + lg
            if diagonal:
                sp2 = jnp.where(strict, sp2, 0.0)
            hi = sp2.astype(jnp.bfloat16)
            lo = (sp2 - hi.astype(jnp.float32)).astype(jnp.bfloat16)
            stage.append((jnp.minimum(z2, 0.0) - lg, sp2[:, 0:1], jnp.concatenate([hi, lo], axis=1)))
        laters = [_dot(hl, u2) for (_, _, hl) in stage]
        weights = []
        for hh, later in enumerate(laters):
            log_beta, first, _ = stage[hh]
            run = run_scr[hh]
            a = jnp.exp2(log_beta - (later + _tile_lanes(run, t)))
            if diagonal:
                a = jnp.where(strict, a, 0.0)
            run_scr[hh] = run + jnp.broadcast_to(later[:, 0:1] + first, run.shape)
            weights.append(a.astype(v_ref.dtype))
        for hh, a in enumerate(weights):
            acc_scr[hh] += _dot(a, v_ref[0, hh, keys, :])

    def off_diagonal(n, carry):
        tile(i - 1 - n, False)
        return carry

    tile(i, True)
    lax.fori_loop(0, i, off_diagonal, 0)
    for hh in range(hb):
        o_ref[0, :, hh * HEAD_DIM:(hh + 1) * HEAD_DIM] = acc_scr[hh].astype(o_ref.dtype)


def _stickbreak(hm, *, q_head0, k_head0, v_head0, hb, t, out_dtype):
    b, _, s, d = hm.shape
    later = np.arange(t)[:, None] > np.arange(t)[None, :]
    u2 = jnp.asarray(np.concatenate([later, later], axis=0), jnp.bfloat16)

    def heads(h0):
        return pl.BlockSpec((1, hb, s, d), lambda bi, g, i: (bi, h0 // hb + g, 0, 0))

    return pl.pallas_call(
        functools.partial(_stickbreak_kernel, hb=hb, t=t),
        out_shape=jax.ShapeDtypeStruct((b, s, H_MIX * d), out_dtype),
        grid=(b, H_MIX // hb, s // t),
        in_specs=[pl.BlockSpec((1, hb, t, d), lambda bi, g, i: (bi, q_head0 // hb + g, i, 0)),
                  heads(k_head0), heads(v_head0),
                  pl.BlockSpec((2 * t, t), lambda bi, g, i: (0, 0))],
        out_specs=pl.BlockSpec((1, t, hb * d), lambda bi, g, i: (bi, i, g)),
        scratch_shapes=[pltpu.VMEM((hb, t, d), jnp.float32), pltpu.VMEM((hb, t, d), jnp.float32)],
        compiler_params=_cparams(("parallel", "parallel", "arbitrary")),
        name="stickbreak",
    )(hm, hm, hm, u2)


def _gelu_tanh(x):
    return 0.5 * x * (1.0 + jnp.tanh(math.sqrt(2.0 / math.pi) * (x + 0.044715 * (x * x * x))))


def _compress_kernel(x_ref, pe_ref, w1_ref, w2_ref, o_ref):
    half = x_ref.shape[3]
    x = x_ref[0, 0]
    w1 = w1_ref[0].astype(jnp.bfloat16)
    first = _dot(x, w1[:half])
    second = _dot(x, w1[half:])
    pe = jnp.broadcast_to(pe_ref[0], (8, pe_ref.shape[2])).astype(jnp.bfloat16)
    pe_term = _dot(pe, w1)[0:1]
    n_chunks = x.shape[0]
    hid = first + pltpu.roll(second, n_chunks - 1, 0) + pe_term
    act = _gelu_tanh(hid)
    o_ref[0, 0] = _dot(act.astype(jnp.bfloat16), w2_ref[0].astype(jnp.bfloat16))


def _compress(chunks, pe, w1, w2):
    b, nh, nc, half = chunks.shape
    return pl.pallas_call(
        _compress_kernel,
        out_shape=jax.ShapeDtypeStruct((b, nh, nc, HEAD_DIM), jnp.float32),
        grid=(b, nh),
        in_specs=[pl.BlockSpec((1, 1, nc, half), lambda bi, h: (bi, h, 0, 0)),
                  pl.BlockSpec((1, 1, 2 * half), lambda bi, h: (h // NSA_KV, 0, 0)),
                  pl.BlockSpec((1, 2 * half, w1.shape[2]), lambda bi, h: (h // NSA_KV, 0, 0)),
                  pl.BlockSpec((1, w2.shape[1], HEAD_DIM), lambda bi, h: (h // NSA_KV, 0, 0))],
        out_specs=pl.BlockSpec((1, 1, nc, HEAD_DIM), lambda bi, h: (bi, h, 0, 0)),
        compiler_params=_cparams(("parallel", "parallel")),
        name="nsa_compress",
    )(chunks, pe, w1, w2)


def _cmp_select_kernel(q_ref, kc_ref, vc_ref, c2s_t_ref, o_ref, sel_ref, *, tq, n_cmp):
    i = pl.program_id(2)
    nc = kc_ref.shape[2]
    kc = kc_ref[0, 0].astype(jnp.bfloat16)
    vc = vc_ref[0, 0].astype(jnp.bfloat16)
    t = i * tq + lax.broadcasted_iota(jnp.int32, (1, tq), 1)
    cidx = lax.broadcasted_iota(jnp.int32, (nc, 1), 0)
    visible = (cidx * CMP_STRIDE + (CMP_LEN - 1) <= t) & (cidx < n_cmp)
    psum = jnp.zeros((nc, tq), jnp.float32)
    for hh in range(NSA_HPG):
        s = jnp.where(visible, _dot_nt(kc, q_ref[0, hh]), NEG)
        m = jnp.max(s, axis=0, keepdims=True)
        e = jnp.where(visible, jnp.exp2(s - m), 0.0)
        l = jnp.sum(e, axis=0, keepdims=True)
        p = e / jnp.where(l > 0.0, l, 1.0)
        psum = psum + p
        o_ref[0, :, hh * HEAD_DIM:(hh + 1) * HEAD_DIM] = _dot(p.T.astype(jnp.bfloat16), vc)

    imp = jnp.dot(c2s_t_ref[...], psum, preferred_element_type=jnp.float32, precision=lax.Precision.HIGHEST)
    w = imp.shape[0]
    n_slc = w // 2
    score_rows = slice(0, n_slc)
    jblk = lax.broadcasted_iota(jnp.int32, (n_slc, 1), 0)
    cur = lax.shift_right_logical(t, int(math.log2(SLC_LEN)))
    forced = (jblk == 0) | (jblk == cur) | (jblk == cur - 1)
    causal_blk = jblk * SLC_LEN <= t
    score = jnp.where(causal_blk, jnp.where(forced, FORCED_SCORE, imp[score_rows]), -1.0)
    rank = jnp.zeros((n_slc, tq), jnp.float32)
    for jp in range(n_slc):
        row = score[jp:jp + 1, :]
        before = (row > score) | ((row == score) & (jblk > jp))
        rank = rank + jnp.where(before, 1.0, 0.0)
    chosen = (rank < float(min(TOPK, n_slc))) & causal_blk
    add_mask = jnp.concatenate([jnp.where(chosen, 0.0, NEG), jnp.zeros((w - n_slc, tq), jnp.float32)], axis=0)
    sel_ref[0, 0] = add_mask.T.astype(sel_ref.dtype)


def _cmp_select(hm, cmp_kv, c2s, *, tq, n_cmp):
    b, _, s, d = hm.shape
    nc = cmp_kv.shape[2]
    w = c2s.shape[1]
    return pl.pallas_call(
        functools.partial(_cmp_select_kernel, tq=tq, n_cmp=n_cmp),
        out_shape=[jax.ShapeDtypeStruct((b, s, H_MIX * d), jnp.float32),
                   jax.ShapeDtypeStruct((b, NSA_KV, s, w), jnp.bfloat16)],
        grid=(b, NSA_KV, s // tq),
        in_specs=[pl.BlockSpec((1, NSA_HPG, tq, d), lambda bi, g, i: (bi, g, i, 0)),
                  pl.BlockSpec((1, 1, nc, d), lambda bi, g, i: (bi, g, 0, 0)),
                  pl.BlockSpec((1, 1, nc, d), lambda bi, g, i: (bi, NSA_KV + g, 0, 0)),
                  pl.BlockSpec((w, nc), lambda bi, g, i: (0, 0))],
        out_specs=[pl.BlockSpec((1, tq, NSA_HPG * d), lambda bi, g, i: (bi, i, g)),
                   pl.BlockSpec((1, 1, tq, w), lambda bi, g, i: (bi, g, i, 0))],
        compiler_params=_cparams(("parallel", "parallel", "parallel")),
        name="nsa_cmp_select",
    )(hm, cmp_kv, cmp_kv, c2s.T)


def _nsa_combine_kernel(oc_ref, os_ref, ow_ref, g_ref, o_ref):
    gates = jax.nn.sigmoid(g_ref[0])
    for h in range(H_MIX):
        cols = slice(h * HEAD_DIM, (h + 1) * HEAD_DIM)
        c0 = GATE_COL0 + 3 * h
        out = (gates[:, c0:c0 + 1] * oc_ref[0, :, cols]
               + gates[:, c0 + 1:c0 + 2] * os_ref[0, :, cols]
               + gates[:, c0 + 2:c0 + 3] * ow_ref[0, :, cols])
        o_ref[0, :, cols] = out.astype(o_ref.dtype)


def _nsa_combine(o_cmp, o_slc, o_win, small, tq=COMBINE_TQ):
    b, s, w = o_cmp.shape
    big = pl.BlockSpec((1, tq, w), lambda bi, i: (bi, i, 0))
    return pl.pallas_call(
        _nsa_combine_kernel,
        out_shape=jax.ShapeDtypeStruct((b, s, w), jnp.bfloat16),
        grid=(b, s // tq),
        in_specs=[big, big, big, pl.BlockSpec((1, tq, small.shape[2]), lambda bi, i: (bi, i, 0))],
        out_specs=big,
        compiler_params=_cparams(("parallel", "parallel")),
        name="nsa_combine",
    )(o_cmp, o_slc, o_win, small)


def _dil_combine_kernel(*refs):
    n = (len(refs) - 1) // 2
    o_refs, lse_refs, out_ref = refs[:n], refs[n:2 * n], refs[-1]
    for h in range(H_MIX):
        lses = []
        for lse_ref in lse_refs:
            hb = H_MIX // lse_ref.shape[1]
            lses.append(lse_ref[0, h // hb][:, h % hb:h % hb + 1])
        m = functools.reduce(jnp.maximum, lses)
        es = [jnp.exp2(l - m) for l in lses]
        tot = functools.reduce(lambda a, b: a + b, es)
        out = functools.reduce(lambda a, b: a + b, [o_ref[0, h] * (e / tot) for o_ref, e in zip(o_refs, es)])
        out_ref[0, :, h * HEAD_DIM:(h + 1) * HEAD_DIM] = out.astype(out_ref.dtype)


def _dil_combine(outs, lses, tq=COMBINE_TQ):
    b, h, s, d = outs[0].shape
    o_spec = pl.BlockSpec((1, h, tq, d), lambda bi, i: (bi, 0, i, 0))
    lse_specs = [pl.BlockSpec((1, l.shape[1], tq, d), lambda bi, i: (bi, 0, i, 0)) for l in lses]
    return pl.pallas_call(
        _dil_combine_kernel,
        out_shape=jax.ShapeDtypeStruct((b, s, h * d), jnp.bfloat16),
        grid=(b, s // tq),
        in_specs=[o_spec] * len(outs) + lse_specs,
        out_specs=pl.BlockSpec((1, tq, h * d), lambda bi, i: (bi, i, 0)),
        compiler_params=_cparams(("parallel", "parallel")),
        name="dil_combine",
    )(*outs, *lses)


def _to_residue_major(t, dil):
    b, h, s, d = t.shape
    return t.reshape(b, h, s // dil, dil, d).transpose(0, 1, 3, 2, 4).reshape(b, h, s, d)


def _from_residue_major(t, dil):
    b, h, s, d = t.shape
    return t.reshape(b, h, dil, s // dil, d).transpose(0, 1, 3, 2, 4).reshape(b, h, s, d)


def _cmp_to_slc_matrix(n_chunks, n_cmp, n_slc):
    ratio = SLC_LEN // CMP_STRIDE
    span = CMP_LEN // CMP_STRIDE
    jj, mm, nn = np.meshgrid(np.arange(n_slc), np.arange(ratio), np.arange(span), indexing="ij")
    cc = ratio * jj + mm + nn
    keep = cc < n_cmp
    mat = np.zeros((n_chunks, 2 * n_slc), np.float32)
    np.add.at(mat, (cc[keep], jj[keep]), 1.0)
    return jnp.asarray(mat)


def _mixers(hm, small, fox_bf, cmp_pe_k, cmp_w1_k, cmp_w2_k, cmp_pe_v, cmp_w1_v, cmp_w2_v):
    b, _, s, d = hm.shape
    bf16 = jnp.bfloat16

    n_chunks = s // CMP_STRIDE
    n_cmp = (s - CMP_LEN) // CMP_STRIDE + 1
    n_slc = s // SLC_LEN
    chunks = hm[:, HD_KC:HD_KC + 2 * NSA_KV].reshape(b, 2 * NSA_KV, n_chunks, CMP_STRIDE * d)
    cmp_kv = _compress(chunks,
                       jnp.stack([cmp_pe_k, cmp_pe_v]).reshape(2, 1, CMP_LEN * d),
                       jnp.stack([cmp_w1_k, cmp_w1_v]), jnp.stack([cmp_w2_k, cmp_w2_v]))
    o_cmp, sel = _cmp_select(hm, cmp_kv, _cmp_to_slc_matrix(n_chunks, n_cmp, n_slc), tq=CMP_TQ, n_cmp=n_cmp)
    onehot = jnp.asarray((np.arange(s)[:, None] // SLC_LEN) == np.arange(2 * n_slc)[None, :], bf16)
    o_slc = _causal_flash(hm, mode="sel", q_head0=HD_QA, k_head0=HD_KS, v_head0=HD_VS, hb=NSA_HPG, t=FLASH_T,
                          sel=sel, onehot=onehot, out_dtype=jnp.float32, name="nsa_selected")
    o_win = _band(hm, q_head0=HD_QA, k_head0=HD_KW, v_head0=HD_VW, hb=NSA_HPG, kv_heads=1, t=BAND_T, wpad=NSA_WINDOW,
                  window=NSA_WINDOW - 1, seg_len=s, has_lse=False, name="nsa_window")
    o_a = _nsa_combine(o_cmp, o_slc, o_win, small)

    t_fox = FLASH_T
    bias_row = jnp.zeros((1, small.shape[2]), jnp.float32).at[0, FORGET_COL0:FORGET_COL0 + H_MIX].set(fox_bf)
    csum = _logf_cumsum(small, bias_row)
    key_bias = (-LOG2E) * csum[:, :, FORGET_COL0:FORGET_COL0 + H_MIX].transpose(0, 2, 1)
    o_b = _causal_flash(hm, mode="fox", q_head0=HD_QB, k_head0=HD_KB, v_head0=HD_VB, hb=HEADS_PER_STEP, t=t_fox,
                        bias=key_bias.reshape(b, H_MIX, s // t_fox, 1, t_fox), out_dtype=bf16, name="fox")

    outs, lses = [], []
    for window, dil in DIL_CONFIGS:
        if dil == 1:
            src, heads0 = hm, (HD_QC, HD_KC2, HD_VC2)
        else:
            src, heads0 = _to_residue_major(hm[:, HD_QC:HD_QC + 3 * H_MIX], dil), (0, H_MIX, 2 * H_MIX)
        hb = HEADS_PER_STEP if dil == 1 else H_MIX
        o, lse = _band(src, q_head0=heads0[0], k_head0=heads0[1], v_head0=heads0[2], hb=hb, kv_heads=hb, t=BAND_T,
                       wpad=HEAD_DIM, window=window // dil, seg_len=s // dil, has_lse=True, name=f"dilated_{dil}")
        if dil != 1:
            o, lse = _from_residue_major(o, dil), _from_residue_major(lse, dil)
        outs.append(o)
        lses.append(lse)
    o_c = _dil_combine(outs, lses)

    o_d = _stickbreak(hm, q_head0=HD_QD, k_head0=HD_KD, v_head0=HD_VD, hb=HEADS_PER_STEP, t=SB_T, out_dtype=bf16)

    return o_a, o_b, o_c, o_d


def _small_w_in_t(w_in_t):
    pad = jnp.zeros((w_in_t.shape[0], HEAD_DIM - N_GATES - H_MIX, w_in_t.shape[2]), w_in_t.dtype)
    return jnp.concatenate([w_in_t[:, GATES_AT:GATES_AT + N_GATES], w_in_t[:, FORGET_AT:FORGET_AT + H_MIX], pad],
                           axis=1)


def _layer(h, layer, p, cos, sin, batch, norm_attn, w_in_t, w_small_t, fox_bf, cmp_pe_k, cmp_w1_k, cmp_w2_k,
           cmp_pe_v, cmp_w1_v, cmp_w2_v, w_o, norm_mlp, w_up, w_down, norm_ple, w_ple_gate, w_ple_proj):
    m, d = h.shape
    s = m // batch
    bf16 = jnp.bfloat16
    xn = _rmsnorm(h, norm_attn, bf16)
    hm = _in_proj_heads(xn, w_in_t, layer, cos, sin, batch)
    small = _in_proj_small(xn, w_small_t, layer)
    mix = _mixers(hm, small.reshape(batch, s, HEAD_DIM), fox_bf,
                  cmp_pe_k, cmp_w1_k, cmp_w2_k, cmp_pe_v, cmp_w1_v, cmp_w2_v)

    tm, tn = MM_TM, MM_TN
    res_spec = pl.BlockSpec((tm, tn), lambda i, j: (i, j))
    h = _out_proj([o.reshape(m, -1) for o in mix], w_o, layer, h, tm=tm, tn=tn)

    x2 = _rmsnorm(h, norm_mlp, bf16)
    mid = _matmul(x2, w_up, layer, bf16, _epi_relu2, tm=tm, tn=tn, name="mlp_up")
    h = _matmul_residual_ksplit(mid, w_down, layer, h, name="mlp_down")

    x3 = _rmsnorm(h, norm_ple, bf16)
    ple_dim = p.shape[2]
    h = _matmul(x3, w_ple_gate, layer, jnp.float32, _epi_ple, (h, p, w_ple_proj),
                (res_spec, pl.BlockSpec((None, tm, ple_dim), lambda i, j: (layer, i, 0)),
                 pl.BlockSpec((None, ple_dim, tn), lambda i, j: (layer, 0, j))),
                tm=tm, tn=tn, name="ple_gate")
    return h


def kernel(x, p, positions, norm_attn, w_in, fox_bf, cmp_pe_k, cmp_w1_k, cmp_w2_k, cmp_pe_v, cmp_w1_v, cmp_w2_v,
           w_o, norm_mlp, w_up, w_down, norm_ple, w_ple_gate, w_ple_proj, norm_final):
    batch, s, d = x.shape
    depth = p.shape[0]
    cos, sin = _rope_tables(positions)
    w_in_t = jnp.swapaxes(w_in, 1, 2)
    w_small_t = _small_w_in_t(w_in_t)
    p = p.reshape(depth, batch * s, -1)
    h = x.reshape(batch * s, d)
    for i in range(depth):
        h = _layer(h, i, p, cos, sin, batch, norm_attn[i], w_in_t, w_small_t, fox_bf[i],
                   cmp_pe_k[i], cmp_w1_k[i], cmp_w2_k[i], cmp_pe_v[i], cmp_w1_v[i], cmp_w2_v[i],
                   w_o, norm_mlp[i], w_up, w_down, norm_ple[i], w_ple_gate, w_ple_proj)
    return _rmsnorm(h, norm_final, x.dtype).reshape(batch, s, d)
```

```python
import functools
import math

import numpy as np
import jax
import jax.numpy as jnp
from jax import lax
from jax.experimental import pallas as pl
from jax.experimental.pallas import tpu as pltpu

HEAD_DIM = 128
H_MIX = 8
NSA_KV = 2
NSA_HPG = H_MIX // NSA_KV
CMP_LEN = 32
CMP_STRIDE = 16
SLC_LEN = 64
TOPK = 16
NSA_WINDOW = 512
DIL_CONFIGS = ((128, 1), (512, 4), (2048, 16))
ROPE_THETA = 10000.0
RMS_EPS = 1e-6
NEG = -1e30
M_INIT = -5e29
FORCED_SCORE = 1e9
LOG2E = math.log2(math.e)
Q_SCALE = HEAD_DIM ** -0.5 * LOG2E

VMEM_LIMIT_BYTES = 58 * 1024 * 1024

MXU_DIM = 256
ROW_TILE = 512
MM_TM, MM_TN = 1024, 512
DOWN_TM, DOWN_TN, DOWN_TK = 2048, 1024, 1024
FLASH_T = 512
SB_T = MXU_DIM
BAND_T = 256
CMP_TQ = 512
COMBINE_TQ = 512
CUMSUM_BLK = 128
HEADS_PER_STEP = 4

HD_QA, HD_KC, HD_VC, HD_KS, HD_VS, HD_KW, HD_VW = 0, 8, 10, 12, 14, 16, 18
HD_QB, HD_KB, HD_VB = 20, 28, 36
HD_QC, HD_KC2, HD_VC2 = 44, 52, 60
HD_QD, HD_KD, HD_VD = 68, 76, 84
N_HEADS_ALL = 92
N_BIG = N_HEADS_ALL * HEAD_DIM
GATE_COL0 = 0
FORGET_COL0 = 24
N_GATES = 3 * H_MIX
GATES_AT = (H_MIX + 6 * NSA_KV) * HEAD_DIM
FORGET_AT = GATES_AT + N_GATES + 3 * H_MIX * HEAD_DIM
_ROPED_HEADS = (tuple(range(HD_QA, HD_QA + H_MIX)) + tuple(range(HD_KS, HD_KS + NSA_KV))
                + tuple(range(HD_KW, HD_KW + NSA_KV)) + tuple(range(HD_QC, HD_QC + 2 * H_MIX)))
_QUERY_HEADS = tuple(h for q0 in (HD_QA, HD_QB, HD_QC, HD_QD) for h in range(q0, q0 + H_MIX))


def _cparams(sem):
    return pltpu.CompilerParams(dimension_semantics=sem, vmem_limit_bytes=VMEM_LIMIT_BYTES)


def _dot_nt(a, b):
    return lax.dot_general(a, b, (((1,), (1,)), ((), ())), preferred_element_type=jnp.float32)


def _dot(a, b):
    return jnp.dot(a, b, preferred_element_type=jnp.float32)


def _rmsnorm_kernel(x_ref, g_ref, o_ref):
    x = x_ref[...]
    ms = jnp.mean(x * x, axis=-1, keepdims=True)
    o_ref[...] = (x * lax.rsqrt(ms + RMS_EPS) * g_ref[...]).astype(o_ref.dtype)


def _rmsnorm(x, g, out_dtype, tm=ROW_TILE):
    m, d = x.shape
    return pl.pallas_call(
        _rmsnorm_kernel,
        out_shape=jax.ShapeDtypeStruct((m, d), out_dtype),
        grid=(m // tm,),
        in_specs=[pl.BlockSpec((tm, d), lambda i: (i, 0)),
                  pl.BlockSpec((1, d), lambda i: (0, 0))],
        out_specs=pl.BlockSpec((tm, d), lambda i: (i, 0)),
        compiler_params=_cparams(("parallel",)),
        name="rmsnorm",
    )(x, g.reshape(1, d))


def _mm_kernel(*refs, n_extra, epilogue):
    x_ref, w_ref = refs[0], refs[1]
    extras = refs[2:2 + n_extra]
    o_ref = refs[2 + n_extra]
    acc = _dot(x_ref[...].astype(jnp.bfloat16), w_ref[...].astype(jnp.bfloat16))
    o_ref[...] = epilogue(acc, *extras).astype(o_ref.dtype)


def _epi_none(acc):
    return acc


def _epi_relu2(acc):
    r = jnp.maximum(acc, 0.0)
    return r * r


def _epi_ple(acc, h_ref, p_ref, wp_ref):
    pp = _dot(p_ref[...].astype(jnp.bfloat16), wp_ref[...].astype(jnp.bfloat16))
    return h_ref[...] + jax.nn.sigmoid(acc) * pp


def _matmul(x, w, layer, out_dtype, epilogue=_epi_none, extras=(), extra_specs=(), tm=MM_TM, tn=MM_TN, name="matmul"):
    m, kdim = x.shape
    n = w.shape[2]
    tn = min(tn, n)
    return pl.pallas_call(
        functools.partial(_mm_kernel, n_extra=len(extras), epilogue=epilogue),
        out_shape=jax.ShapeDtypeStruct((m, n), out_dtype),
        grid=(m // tm, n // tn),
        in_specs=[pl.BlockSpec((tm, kdim), lambda i, j: (i, 0)),
                  pl.BlockSpec((None, kdim, tn), lambda i, j: (layer, 0, j)),
                  *extra_specs],
        out_specs=pl.BlockSpec((tm, tn), lambda i, j: (i, j)),
        compiler_params=_cparams(("parallel", "arbitrary")),
        name=name,
    )(x, w, *extras)


def _out_proj_kernel(*refs):
    *x_refs, w_ref, h_ref, o_ref = refs
    acc = h_ref[...]
    row = 0
    for x_ref in x_refs:
        width = x_ref.shape[1]
        acc = acc + _dot(x_ref[...], w_ref[row:row + width, :].astype(jnp.bfloat16))
        row += width
    o_ref[...] = acc


def _out_proj(xs, w, layer, h, tm, tn):
    m, n = h.shape
    return pl.pallas_call(
        _out_proj_kernel,
        out_shape=jax.ShapeDtypeStruct((m, n), jnp.float32),
        grid=(m // tm, n // tn),
        in_specs=[*[pl.BlockSpec((tm, x.shape[1]), lambda i, j: (i, 0)) for x in xs],
                  pl.BlockSpec((None, w.shape[1], tn), lambda i, j: (layer, 0, j)),
                  pl.BlockSpec((tm, tn), lambda i, j: (i, j))],
        out_specs=pl.BlockSpec((tm, tn), lambda i, j: (i, j)),
        compiler_params=_cparams(("parallel", "arbitrary")),
        name="out_proj",
    )(*xs, w, h)


def _mm_residual_ksplit_kernel(x_ref, w_ref, h_ref, o_ref):
    @pl.when(pl.program_id(2) == 0)
    def _():
        o_ref[...] = h_ref[...] + _dot(x_ref[...], w_ref[...].astype(jnp.bfloat16))

    @pl.when(pl.program_id(2) > 0)
    def _():
        o_ref[...] += _dot(x_ref[...], w_ref[...].astype(jnp.bfloat16))


def _matmul_residual_ksplit(x, w, layer, h, tm=DOWN_TM, tn=DOWN_TN, tk=DOWN_TK, name="matmul_ksplit"):
    m, kdim = x.shape
    n = w.shape[2]
    return pl.pallas_call(
        _mm_residual_ksplit_kernel,
        out_shape=jax.ShapeDtypeStruct((m, n), jnp.float32),
        grid=(m // tm, n // tn, kdim // tk),
        in_specs=[pl.BlockSpec((tm, tk), lambda i, j, k: (i, k)),
                  pl.BlockSpec((None, tk, tn), lambda i, j, k: (layer, k, j)),
                  pl.BlockSpec((tm, tn), lambda i, j, k: (i, j))],
        out_specs=pl.BlockSpec((tm, tn), lambda i, j, k: (i, j)),
        compiler_params=_cparams(("parallel", "parallel", "arbitrary")),
        name=name,
    )(x, w, h)


def _rope_table_kernel(pos_ref, freq_ref, sign_ref, cos_ref, sin_ref):
    ang = pos_ref[...] * freq_ref[...]
    cos_ref[...] = jnp.cos(ang)
    sin_ref[...] = jnp.sin(ang) * sign_ref[...]


def _rope_tables(positions, ts=COMBINE_TQ):
    n = positions.size
    half = HEAD_DIM // 2
    inv_freq = ROPE_THETA ** (-jnp.arange(half, dtype=jnp.float32) / half)
    freq = jnp.concatenate([inv_freq, inv_freq]).reshape(1, HEAD_DIM)
    sign = jnp.concatenate([-jnp.ones((half,), jnp.float32), jnp.ones((half,), jnp.float32)]).reshape(1, HEAD_DIM)
    pos = positions.astype(jnp.float32).reshape(n, 1)
    row = pl.BlockSpec((1, HEAD_DIM), lambda i: (0, 0))
    return pl.pallas_call(
        _rope_table_kernel,
        out_shape=[jax.ShapeDtypeStruct((n, HEAD_DIM), jnp.float32)] * 2,
        grid=(n // ts,),
        in_specs=[pl.BlockSpec((ts, 1), lambda i: (i, 0)), row, row],
        out_specs=[pl.BlockSpec((ts, HEAD_DIM), lambda i: (i, 0))] * 2,
        compiler_params=_cparams(("parallel",)),
        name="rope_tables",
    )(pos, freq, sign)


_FLAG_ROPED, _FLAG_QUERY = 1, 2


def _in_proj_kernel(flags_ref, row0_ref, x_ref, wt_ref, cos_ref, sin_ref, o_ref):
    j = pl.program_id(1)
    heads = o_ref.shape[1]
    acc = _dot_nt(x_ref[...], wt_ref[0].astype(jnp.bfloat16))
    for hh in range(heads):
        t = acc[:, hh * HEAD_DIM:(hh + 1) * HEAD_DIM]
        flags = flags_ref[j * heads + hh]
        roped = (flags & _FLAG_ROPED) > 0
        scale = jnp.where((flags & _FLAG_QUERY) > 0, Q_SCALE, 1.0)
        cos = jnp.where(roped, cos_ref[...], 1.0)
        sin = jnp.where(roped, sin_ref[...], 0.0)
        o_ref[0, hh] = ((t * cos + pltpu.roll(t, HEAD_DIM // 2, 1) * sin) * scale).astype(o_ref.dtype)


def _in_proj_heads(xn, w_in_t, layer, cos, sin, batch, tm=MM_TM, tn=MM_TN):
    m, d = xn.shape
    s = m // batch
    heads = tn // HEAD_DIM
    q_tiles = s // tm
    flags = np.zeros((N_HEADS_ALL,), np.int32)
    flags[list(_ROPED_HEADS)] |= _FLAG_ROPED
    flags[list(_QUERY_HEADS)] |= _FLAG_QUERY
    col = np.arange(0, N_BIG, tn)
    row0 = col + np.where(col >= GATES_AT, N_GATES, 0) + np.where(col >= FORGET_AT - N_GATES, H_MIX, 0)
    table = pl.BlockSpec((tm, HEAD_DIM), lambda i, j, f, r: (i, 0))
    return pl.pallas_call(
        _in_proj_kernel,
        out_shape=jax.ShapeDtypeStruct((batch, N_HEADS_ALL, s, HEAD_DIM), jnp.bfloat16),
        grid_spec=pltpu.PrefetchScalarGridSpec(
            num_scalar_prefetch=2,
            grid=(m // tm, N_BIG // tn),
            in_specs=[pl.BlockSpec((tm, d), lambda i, j, f, r: (i, 0)),
                      pl.BlockSpec((pl.Element(1), pl.Element(tn), pl.Element(d)),
                                   lambda i, j, f, r: (layer, pl.multiple_of(r[j], 8), 0)),
                      table, table],
            out_specs=pl.BlockSpec((1, heads, tm, HEAD_DIM),
                                   lambda i, j, f, r: (i // q_tiles, j, i % q_tiles, 0)),
        ),
        compiler_params=_cparams(("parallel", "arbitrary")),
        name="in_proj",
    )(jnp.asarray(flags), jnp.asarray(row0.astype(np.int32)), xn, w_in_t, cos, sin)


def _in_proj_small_kernel(x_ref, wt_ref, o_ref):
    o_ref[...] = _dot_nt(x_ref[...], wt_ref[...].astype(jnp.bfloat16))


def _in_proj_small(xn, w_small_t, layer, tm=MM_TM):
    m, d = xn.shape
    n = w_small_t.shape[1]
    return pl.pallas_call(
        _in_proj_small_kernel,
        out_shape=jax.ShapeDtypeStruct((m, n), jnp.float32),
        grid=(m // tm,),
        in_specs=[pl.BlockSpec((tm, d), lambda i: (i, 0)),
                  pl.BlockSpec((None, n, d), lambda i: (layer, 0, 0))],
        out_specs=pl.BlockSpec((tm, n), lambda i: (i, 0)),
        compiler_params=_cparams(("parallel",)),
        name="in_proj_small",
    )(xn, w_small_t)


def _softplus(z):
    return jnp.maximum(z, 0.0) + jnp.log1p(jnp.exp(-jnp.abs(z)))


def _logf_cumsum_kernel(x_ref, bias_ref, o_ref, *, blk):
    s = x_ref.shape[1]
    r = lax.broadcasted_iota(jnp.int32, (blk, blk), 0)
    c = lax.broadcasted_iota(jnp.int32, (blk, blk), 1)
    tri = jnp.where(c <= r, 1.0, 0.0).astype(jnp.float32)

    def body(i, carry):
        x = x_ref[0, pl.ds(i * blk, blk), :]
        logf = -_softplus(-(x + bias_ref[...]))
        cs = jnp.dot(tri, logf, preferred_element_type=jnp.float32,
                     precision=lax.Precision.HIGHEST) + carry
        o_ref[0, pl.ds(i * blk, blk), :] = cs
        return cs[blk - 1:blk, :]

    lax.fori_loop(0, s // blk, body, jnp.zeros((1, x_ref.shape[2]), jnp.float32))


def _logf_cumsum(small, bias_row, blk=CUMSUM_BLK):
    b, s, w = small.shape
    return pl.pallas_call(
        functools.partial(_logf_cumsum_kernel, blk=blk),
        out_shape=jax.ShapeDtypeStruct((b, s, w), jnp.float32),
        grid=(b,),
        in_specs=[pl.BlockSpec((1, s, w), lambda bi: (bi, 0, 0)),
                  pl.BlockSpec((1, w), lambda bi: (0, 0))],
        out_specs=pl.BlockSpec((1, s, w), lambda bi: (bi, 0, 0)),
        compiler_params=_cparams(("parallel",)),
        name="logf_cumsum",
    )(small, bias_row)


def _rowmax_lanes(s):
    m = s[:, :HEAD_DIM]
    for c in range(1, s.shape[1] // HEAD_DIM):
        m = jnp.maximum(m, s[:, c * HEAD_DIM:(c + 1) * HEAD_DIM])
    return jnp.broadcast_to(jnp.max(m, axis=-1, keepdims=True), m.shape)


def _fold_lanes(p):
    a = p[:, :HEAD_DIM]
    for c in range(1, p.shape[1] // HEAD_DIM):
        a = a + p[:, c * HEAD_DIM:(c + 1) * HEAD_DIM]
    return a


def _tile_lanes(x, width):
    return jnp.concatenate([x] * (width // HEAD_DIM), axis=1)


def _causal_flash_kernel(*refs, mode, hb, t):
    it = iter(refs)
    q_ref, k_ref, v_ref = next(it), next(it), next(it)
    bias_ref = next(it) if mode == "fox" else None
    sel_ref, onehot_ref = (next(it), next(it)) if mode == "sel" else (None, None)
    o_ref = next(it)
    m_scr, l_scr, acc_scr = next(it), next(it), next(it)
    i = pl.program_id(2)

    m_scr[...] = jnp.full(m_scr.shape, M_INIT, jnp.float32)
    l_scr[...] = jnp.zeros(l_scr.shape, jnp.float32)
    acc_scr[...] = jnp.zeros(acc_scr.shape, jnp.float32)

    def tile(j, diagonal):
        keys = pl.ds(pl.multiple_of(j * t, t), t)
        if mode == "sel":
            k_shared = jnp.concatenate([k_ref[0, 0, keys, :], onehot_ref[keys, :]], axis=1)
            scores = [_dot_nt(jnp.concatenate([q_ref[0, hh], sel_ref[0, 0]], axis=1), k_shared)
                      for hh in range(hb)]
        else:
            scores = [_dot_nt(q_ref[0, hh], k_ref[0, hh, keys, :]) + bias_ref[0, hh, j] for hh in range(hb)]
        if diagonal:
            causal = (lax.broadcasted_iota(jnp.int32, (t, t), 1) <= lax.broadcasted_iota(jnp.int32, (t, t), 0))
            scores = [jnp.where(causal, s, NEG) for s in scores]
        probs = []
        for hh, s in enumerate(scores):
            m_prev = m_scr[hh]
            m_new = jnp.maximum(m_prev, _rowmax_lanes(s))
            alpha = jnp.exp2(m_prev - m_new)
            p = jnp.exp2(s - _tile_lanes(m_new, t))
            l_scr[hh] = alpha * l_scr[hh] + _fold_lanes(p)
            m_scr[hh] = m_new
            probs.append((alpha, p.astype(v_ref.dtype)))
        for hh, (alpha, p) in enumerate(probs):
            v = v_ref[0, 0 if mode == "sel" else hh, keys, :]
            acc_scr[hh] = alpha * acc_scr[hh] + _dot(p, v)

    def off_diagonal(j, carry):
        tile(j, False)
        return carry

    lax.fori_loop(0, i, off_diagonal, 0)
    tile(i, True)
    for hh in range(hb):
        l = jnp.sum(l_scr[hh], axis=-1, keepdims=True)
        o_ref[0, :, hh * HEAD_DIM:(hh + 1) * HEAD_DIM] = (acc_scr[hh] / l).astype(o_ref.dtype)


def _causal_flash(hm, *, mode, q_head0, k_head0, v_head0, hb, t, out_dtype, bias=None, sel=None, onehot=None,
                  name="causal_flash"):
    b, _, s, d = hm.shape
    groups = H_MIX // hb
    kv_heads = hb if mode == "fox" else 1
    in_specs = [pl.BlockSpec((1, hb, t, d), lambda bi, g, i: (bi, q_head0 // hb + g, i, 0)),
                pl.BlockSpec((1, kv_heads, s, d), lambda bi, g, i: (bi, k_head0 // kv_heads + g, 0, 0)),
                pl.BlockSpec((1, kv_heads, s, d), lambda bi, g, i: (bi, v_head0 // kv_heads + g, 0, 0))]
    args = [hm, hm, hm]
    if mode == "fox":
        in_specs.append(pl.BlockSpec((1, hb, s // t, 1, t), lambda bi, g, i: (bi, g, 0, 0, 0)))
        args.append(bias)
    else:
        in_specs.append(pl.BlockSpec((1, 1, t, sel.shape[-1]), lambda bi, g, i: (bi, g, i, 0)))
        in_specs.append(pl.BlockSpec(onehot.shape, lambda bi, g, i: (0, 0)))
        args += [sel, onehot]
    return pl.pallas_call(
        functools.partial(_causal_flash_kernel, mode=mode, hb=hb, t=t),
        out_shape=jax.ShapeDtypeStruct((b, s, H_MIX * d), out_dtype),
        grid=(b, groups, s // t),
        in_specs=in_specs,
        out_specs=pl.BlockSpec((1, t, hb * d), lambda bi, g, i: (bi, i, g)),
        scratch_shapes=[pltpu.VMEM((hb, t, d), jnp.float32)] * 3,
        compiler_params=_cparams(("parallel", "parallel", "arbitrary")),
        name=name,
    )(*args)


def _band_kernel(*refs, hb, kv_heads, t, wpad, window, seg_len, has_lse):
    q_ref, k_ref, v_ref, o_ref = refs[:4]
    lse_ref = refs[4] if has_lse else None
    s_len = k_ref.shape[2]
    span = t + wpad
    t0 = pl.program_id(2) * t
    seg0 = (t0 // seg_len) * seg_len
    start = jnp.minimum(jnp.maximum(t0 - wpad, seg0), s_len - span)
    start = pl.multiple_of(start, HEAD_DIM)
    keys = pl.ds(start, span)
    qpos = t0 + lax.broadcasted_iota(jnp.int32, (t, span), 0)
    kpos = start + lax.broadcasted_iota(jnp.int32, (t, span), 1)
    dist = qpos - kpos
    mask = (dist >= 0) & (dist <= window) & (kpos >= seg0)
    kv_of = [hh * kv_heads // hb for hh in range(hb)]
    scores = [jnp.where(mask, _dot_nt(q_ref[0, hh], k_ref[0, kv_of[hh], keys, :]), NEG) for hh in range(hb)]
    probs = []
    for s in scores:
        m = _rowmax_lanes(s)
        p = jnp.exp2(s - _tile_lanes(m, span))
        l = jnp.broadcast_to(jnp.sum(_fold_lanes(p), axis=-1, keepdims=True), m.shape)
        probs.append((m, l, p.astype(v_ref.dtype)))
    lane = lax.broadcasted_iota(jnp.int32, (t, HEAD_DIM), 1)
    lse_all = jnp.zeros((t, HEAD_DIM), jnp.float32)
    for hh, (m, l, p) in enumerate(probs):
        out = _dot(p, v_ref[0, kv_of[hh], keys, :]) / l
        if has_lse:
            o_ref[0, hh] = out.astype(o_ref.dtype)
            lse_all = jnp.where(lane == hh, m + jnp.log2(l), lse_all)
        else:
            o_ref[0, :, hh * HEAD_DIM:(hh + 1) * HEAD_DIM] = out.astype(o_ref.dtype)
    if has_lse:
        lse_ref[0, 0] = lse_all


def _band(src, *, q_head0, k_head0, v_head0, hb, kv_heads, t, wpad, window, seg_len, has_lse, name):
    b, _, s, d = src.shape
    groups = H_MIX // hb
    in_specs = [pl.BlockSpec((1, hb, t, d), lambda bi, g, i: (bi, q_head0 // hb + g, i, 0)),
                pl.BlockSpec((1, kv_heads, s, d), lambda bi, g, i: (bi, k_head0 // kv_heads + g, 0, 0)),
                pl.BlockSpec((1, kv_heads, s, d), lambda bi, g, i: (bi, v_head0 // kv_heads + g, 0, 0))]
    if has_lse:
        out_shape = [jax.ShapeDtypeStruct((b, H_MIX, s, d), jnp.float32),
                     jax.ShapeDtypeStruct((b, groups, s, d), jnp.float32)]
        out_specs = [pl.BlockSpec((1, hb, t, d), lambda bi, g, i: (bi, g, i, 0)),
                     pl.BlockSpec((1, 1, t, d), lambda bi, g, i: (bi, g, i, 0))]
    else:
        out_shape = jax.ShapeDtypeStruct((b, s, H_MIX * d), jnp.float32)
        out_specs = pl.BlockSpec((1, t, hb * d), lambda bi, g, i: (bi, i, g))
    return pl.pallas_call(
        functools.partial(_band_kernel, hb=hb, kv_heads=kv_heads, t=t, wpad=wpad, window=window,
                          seg_len=seg_len, has_lse=has_lse),
        out_shape=out_shape,
        grid=(b, groups, s // t),
        in_specs=in_specs,
        out_specs=out_specs,
        compiler_params=_cparams(("parallel", "parallel", "arbitrary")),
        name=name,
    )(src, src, src)


def _stickbreak_kernel(q_ref, k_ref, v_ref, u2_ref, o_ref, run_scr, acc_scr, *, hb, t):
    i = pl.program_id(2)
    run_scr[...] = jnp.zeros(run_scr.shape, jnp.float32)
    acc_scr[...] = jnp.zeros(acc_scr.shape, jnp.float32)
    u2 = u2_ref[...]
    sign_bit = jnp.uint32(0x80000000)

    def tile(j, diagonal):
        keys = pl.ds(pl.multiple_of(j * t, t), t)
        if diagonal:
            strict = (lax.broadcasted_iota(jnp.int32, (t, t), 1) < lax.broadcasted_iota(jnp.int32, (t, t), 0))
        zs = [_dot_nt(q_ref[0, hh], k_ref[0, hh, keys, :]) for hh in range(hb)]
        stage = []
        for z2 in zs:
            neg_abs = lax.bitcast_convert_type(lax.bitcast_convert_type(z2, jnp.uint32) | sign_bit, jnp.float32)
            lg = jnp.log2(1.0 + jnp.exp2(neg_abs))
            sp2 = jnp.maximum(z2, 0.0) + lg
            if diagonal:
                sp2 = jnp.where(strict, sp2, 0.0)
            hi = sp2.astype(jnp.bfloat16)
            lo = (sp2 - hi.astype(jnp.float32)).astype(jnp.bfloat16)
            stage.append((jnp.minimum(z2, 0.0) - lg, sp2[:, 0:1], jnp.concatenate([hi, lo], axis=1)))
        laters = [_dot(hl, u2) for (_, _, hl) in stage]
        weights = []
        for hh, later in enumerate(laters):
            log_beta, first, _ = stage[hh]
            run = run_scr[hh]
            a = jnp.exp2(log_beta - (later + _tile_lanes(run, t)))
            if diagonal:
                a = jnp.where(strict, a, 0.0)
            run_scr[hh] = run + jnp.broadcast_to(later[:, 0:1] + first, run.shape)
            weights.append(a.astype(v_ref.dtype))
        for hh, a in enumerate(weights):
            acc_scr[hh] += _dot(a, v_ref[0, hh, keys, :])

    def off_diagonal(n, carry):
        tile(i - 1 - n, False)
        return carry

    tile(i, True)
    lax.fori_loop(0, i, off_diagonal, 0)
    for hh in range(hb):
        o_ref[0, :, hh * HEAD_DIM:(hh + 1) * HEAD_DIM] = acc_scr[hh].astype(o_ref.dtype)


def _stickbreak(hm, *, q_head0, k_head0, v_head0, hb, t, out_dtype):
    b, _, s, d = hm.shape
    later = np.arange(t)[:, None] > np.arange(t)[None, :]
    u2 = jnp.asarray(np.concatenate([later, later], axis=0), jnp.bfloat16)

    def heads(h0):
        return pl.BlockSpec((1, hb, s, d), lambda bi, g, i: (bi, h0 // hb + g, 0, 0))

    return pl.pallas_call(
        functools.partial(_stickbreak_kernel, hb=hb, t=t),
        out_shape=jax.ShapeDtypeStruct((b, s, H_MIX * d), out_dtype),
        grid=(b, H_MIX // hb, s // t),
        in_specs=[pl.BlockSpec((1, hb, t, d), lambda bi, g, i: (bi, q_head0 // hb + g, i, 0)),
                  heads(k_head0), heads(v_head0),
                  pl.BlockSpec((2 * t, t), lambda bi, g, i: (0, 0))],
        out_specs=pl.BlockSpec((1, t, hb * d), lambda bi, g, i: (bi, i, g)),
        scratch_shapes=[pltpu.VMEM((hb, t, d), jnp.float32), pltpu.VMEM((hb, t, d), jnp.float32)],
        compiler_params=_cparams(("parallel", "parallel", "arbitrary")),
        name="stickbreak",
    )(hm, hm, hm, u2)


def _gelu_tanh(x):
    return 0.5 * x * (1.0 + jnp.tanh(math.sqrt(2.0 / math.pi) * (x + 0.044715 * (x * x * x))))


def _compress_kernel(x_ref, pe_ref, w1_ref, w2_ref, o_ref):
    half = x_ref.shape[3]
    x = x_ref[0, 0]
    w1 = w1_ref[0].astype(jnp.bfloat16)
    first = _dot(x, w1[:half])
    second = _dot(x, w1[half:])
    pe = jnp.broadcast_to(pe_ref[0], (8, pe_ref.shape[2])).astype(jnp.bfloat16)
    pe_term = _dot(pe, w1)[0:1]
    n_chunks = x.shape[0]
    hid = first + pltpu.roll(second, n_chunks - 1, 0) + pe_term
    act = _gelu_tanh(hid)
    o_ref[0, 0] = _dot(act.astype(jnp.bfloat16), w2_ref[0].astype(jnp.bfloat16))


def _compress(chunks, pe, w1, w2):
    b, nh, nc, half = chunks.shape
    return pl.pallas_call(
        _compress_kernel,
        out_shape=jax.ShapeDtypeStruct((b, nh, nc, HEAD_DIM), jnp.float32),
        grid=(b, nh),
        in_specs=[pl.BlockSpec((1, 1, nc, half), lambda bi, h: (bi, h, 0, 0)),
                  pl.BlockSpec((1, 1, 2 * half), lambda bi, h: (h // NSA_KV, 0, 0)),
                  pl.BlockSpec((1, 2 * half, w1.shape[2]), lambda bi, h: (h // NSA_KV, 0, 0)),
                  pl.BlockSpec((1, w2.shape[1], HEAD_DIM), lambda bi, h: (h // NSA_KV, 0, 0))],
        out_specs=pl.BlockSpec((1, 1, nc, HEAD_DIM), lambda bi, h: (bi, h, 0, 0)),
        compiler_params=_cparams(("parallel", "parallel")),
        name="nsa_compress",
    )(chunks, pe, w1, w2)


def _cmp_select_kernel(q_ref, kc_ref, vc_ref, c2s_t_ref, o_ref, sel_ref, *, tq, n_cmp):
    i = pl.program_id(2)
    nc = kc_ref.shape[2]
    kc = kc_ref[0, 0].astype(jnp.bfloat16)
    vc = vc_ref[0, 0].astype(jnp.bfloat16)
    t = i * tq + lax.broadcasted_iota(jnp.int32, (1, tq), 1)
    cidx = lax.broadcasted_iota(jnp.int32, (nc, 1), 0)
    visible = (cidx * CMP_STRIDE + (CMP_LEN - 1) <= t) & (cidx < n_cmp)
    psum = jnp.zeros((nc, tq), jnp.float32)
    for hh in range(NSA_HPG):
        s = jnp.where(visible, _dot_nt(kc, q_ref[0, hh]), NEG)
        m = jnp.max(s, axis=0, keepdims=True)
        e = jnp.where(visible, jnp.exp2(s - m), 0.0)
        l = jnp.sum(e, axis=0, keepdims=True)
        p = e / jnp.where(l > 0.0, l, 1.0)
        psum = psum + p
        o_ref[0, :, hh * HEAD_DIM:(hh + 1) * HEAD_DIM] = _dot(p.T.astype(jnp.bfloat16), vc)

    imp = jnp.dot(c2s_t_ref[...], psum, preferred_element_type=jnp.float32, precision=lax.Precision.HIGHEST)
    w = imp.shape[0]
    n_slc = w // 2
    score_rows = slice(0, n_slc)
    jblk = lax.broadcasted_iota(jnp.int32, (n_slc, 1), 0)
    cur = lax.shift_right_logical(t, int(math.log2(SLC_LEN)))
    forced = (jblk == 0) | (jblk == cur) | (jblk == cur - 1)
    causal_blk = jblk * SLC_LEN <= t
    score = jnp.where(causal_blk, jnp.where(forced, FORCED_SCORE, imp[score_rows]), -1.0)
    rank = jnp.zeros((n_slc, tq), jnp.float32)
    for jp in range(n_slc):
        row = score[jp:jp + 1, :]
        before = (row > score) | ((row == score) & (jblk > jp))
        rank = rank + jnp.where(before, 1.0, 0.0)
    chosen = (rank < float(min(TOPK, n_slc))) & causal_blk
    add_mask = jnp.concatenate([jnp.where(chosen, 0.0, NEG), jnp.zeros((w - n_slc, tq), jnp.float32)], axis=0)
    sel_ref[0, 0] = add_mask.T.astype(sel_ref.dtype)


def _cmp_select(hm, cmp_kv, c2s, *, tq, n_cmp):
    b, _, s, d = hm.shape
    nc = cmp_kv.shape[2]
    w = c2s.shape[1]
    return pl.pallas_call(
        functools.partial(_cmp_select_kernel, tq=tq, n_cmp=n_cmp),
        out_shape=[jax.ShapeDtypeStruct((b, s, H_MIX * d), jnp.float32),
                   jax.ShapeDtypeStruct((b, NSA_KV, s, w), jnp.bfloat16)],
        grid=(b, NSA_KV, s // tq),
        in_specs=[pl.BlockSpec((1, NSA_HPG, tq, d), lambda bi, g, i: (bi, g, i, 0)),
                  pl.BlockSpec((1, 1, nc, d), lambda bi, g, i: (bi, g, 0, 0)),
                  pl.BlockSpec((1, 1, nc, d), lambda bi, g, i: (bi, NSA_KV + g, 0, 0)),
                  pl.BlockSpec((w, nc), lambda bi, g, i: (0, 0))],
        out_specs=[pl.BlockSpec((1, tq, NSA_HPG * d), lambda bi, g, i: (bi, i, g)),
                   pl.BlockSpec((1, 1, tq, w), lambda bi, g, i: (bi, g, i, 0))],
        compiler_params=_cparams(("parallel", "parallel", "parallel")),
        name="nsa_cmp_select",
    )(hm, cmp_kv, cmp_kv, c2s.T)


def _nsa_combine_kernel(oc_ref, os_ref, ow_ref, g_ref, o_ref):
    gates = jax.nn.sigmoid(g_ref[0])
    for h in range(H_MIX):
        cols = slice(h * HEAD_DIM, (h + 1) * HEAD_DIM)
        c0 = GATE_COL0 + 3 * h
        out = (gates[:, c0:c0 + 1] * oc_ref[0, :, cols]
               + gates[:, c0 + 1:c0 + 2] * os_ref[0, :, cols]
               + gates[:, c0 + 2:c0 + 3] * ow_ref[0, :, cols])
        o_ref[0, :, cols] = out.astype(o_ref.dtype)


def _nsa_combine(o_cmp, o_slc, o_win, small, tq=COMBINE_TQ):
    b, s, w = o_cmp.shape
    big = pl.BlockSpec((1, tq, w), lambda bi, i: (bi, i, 0))
    return pl.pallas_call(
        _nsa_combine_kernel,
        out_shape=jax.ShapeDtypeStruct((b, s, w), jnp.bfloat16),
        grid=(b, s // tq),
        in_specs=[big, big, big, pl.BlockSpec((1, tq, small.shape[2]), lambda bi, i: (bi, i, 0))],
        out_specs=big,
        compiler_params=_cparams(("parallel", "parallel")),
        name="nsa_combine",
    )(o_cmp, o_slc, o_win, small)


def _dil_combine_kernel(*refs):
    n = (len(refs) - 1) // 2
    o_refs, lse_refs, out_ref = refs[:n], refs[n:2 * n], refs[-1]
    for h in range(H_MIX):
        lses = []
        for lse_ref in lse_refs:
            hb = H_MIX // lse_ref.shape[1]
            lses.append(lse_ref[0, h // hb][:, h % hb:h % hb + 1])
        m = functools.reduce(jnp.maximum, lses)
        es = [jnp.exp2(l - m) for l in lses]
        tot = functools.reduce(lambda a, b: a + b, es)
        out = functools.reduce(lambda a, b: a + b, [o_ref[0, h] * (e / tot) for o_ref, e in zip(o_refs, es)])
        out_ref[0, :, h * HEAD_DIM:(h + 1) * HEAD_DIM] = out.astype(out_ref.dtype)


def _dil_combine(outs, lses, tq=COMBINE_TQ):
    b, h, s, d = outs[0].shape
    o_spec = pl.BlockSpec((1, h, tq, d), lambda bi, i: (bi, 0, i, 0))
    lse_specs = [pl.BlockSpec((1, l.shape[1], tq, d), lambda bi, i: (bi, 0, i, 0)) for l in lses]
    return pl.pallas_call(
        _dil_combine_kernel,
        out_shape=jax.ShapeDtypeStruct((b, s, h * d), jnp.bfloat16),
        grid=(b, s // tq),
        in_specs=[o_spec] * len(outs) + lse_specs,
        out_specs=pl.BlockSpec((1, tq, h * d), lambda bi, i: (bi, i, 0)),
        compiler_params=_cparams(("parallel", "parallel")),
        name="dil_combine",
    )(*outs, *lses)


def _to_residue_major(t, dil):
    b, h, s, d = t.shape
    return t.reshape(b, h, s // dil, dil, d).transpose(0, 1, 3, 2, 4).reshape(b, h, s, d)


def _from_residue_major(t, dil):
    b, h, s, d = t.shape
    return t.reshape(b, h, dil, s // dil, d).transpose(0, 1, 3, 2, 4).reshape(b, h, s, d)


def _cmp_to_slc_matrix(n_chunks, n_cmp, n_slc):
    ratio = SLC_LEN // CMP_STRIDE
    span = CMP_LEN // CMP_STRIDE
    jj, mm, nn = np.meshgrid(np.arange(n_slc), np.arange(ratio), np.arange(span), indexing="ij")
    cc = ratio * jj + mm + nn
    keep = cc < n_cmp
    mat = np.zeros((n_chunks, 2 * n_slc), np.float32)
    np.add.at(mat, (cc[keep], jj[keep]), 1.0)
    return jnp.asarray(mat)


def _mixers(hm, small, fox_bf, cmp_pe_k, cmp_w1_k, cmp_w2_k, cmp_pe_v, cmp_w1_v, cmp_w2_v):
    b, _, s, d = hm.shape
    bf16 = jnp.bfloat16

    n_chunks = s // CMP_STRIDE
    n_cmp = (s - CMP_LEN) // CMP_STRIDE + 1
    n_slc = s // SLC_LEN
    chunks = hm[:, HD_KC:HD_KC + 2 * NSA_KV].reshape(b, 2 * NSA_KV, n_chunks, CMP_STRIDE * d)
    cmp_kv = _compress(chunks,
                       jnp.stack([cmp_pe_k, cmp_pe_v]).reshape(2, 1, CMP_LEN * d),
                       jnp.stack([cmp_w1_k, cmp_w1_v]), jnp.stack([cmp_w2_k, cmp_w2_v]))
    o_cmp, sel = _cmp_select(hm, cmp_kv, _cmp_to_slc_matrix(n_chunks, n_cmp, n_slc), tq=CMP_TQ, n_cmp=n_cmp)
    onehot = jnp.asarray((np.arange(s)[:, None] // SLC_LEN) == np.arange(2 * n_slc)[None, :], bf16)
    o_slc = _causal_flash(hm, mode="sel", q_head0=HD_QA, k_head0=HD_KS, v_head0=HD_VS, hb=NSA_HPG, t=FLASH_T,
                          sel=sel, onehot=onehot, out_dtype=jnp.float32, name="nsa_selected")
    o_win = _band(hm, q_head0=HD_QA, k_head0=HD_KW, v_head0=HD_VW, hb=NSA_HPG, kv_heads=1, t=BAND_T, wpad=NSA_WINDOW,
                  window=NSA_WINDOW - 1, seg_len=s, has_lse=False, name="nsa_window")
    o_a = _nsa_combine(o_cmp, o_slc, o_win, small)

    t_fox = FLASH_T
    bias_row = jnp.zeros((1, small.shape[2]), jnp.float32).at[0, FORGET_COL0:FORGET_COL0 + H_MIX].set(fox_bf)
    csum = _logf_cumsum(small, bias_row)
    key_bias = (-LOG2E) * csum[:, :, FORGET_COL0:FORGET_COL0 + H_MIX].transpose(0, 2, 1)
    o_b = _causal_flash(hm, mode="fox", q_head0=HD_QB, k_head0=HD_KB, v_head0=HD_VB, hb=HEADS_PER_STEP, t=t_fox,
                        bias=key_bias.reshape(b, H_MIX, s // t_fox, 1, t_fox), out_dtype=bf16, name="fox")

    outs, lses = [], []
    for window, dil in DIL_CONFIGS:
        if dil == 1:
            src, heads0 = hm, (HD_QC, HD_KC2, HD_VC2)
        else:
            src, heads0 = _to_residue_major(hm[:, HD_QC:HD_QC + 3 * H_MIX], dil), (0, H_MIX, 2 * H_MIX)
        hb = HEADS_PER_STEP if dil == 1 else H_MIX
        o, lse = _band(src, q_head0=heads0[0], k_head0=heads0[1], v_head0=heads0[2], hb=hb, kv_heads=hb, t=BAND_T,
                       wpad=HEAD_DIM, window=window // dil, seg_len=s // dil, has_lse=True, name=f"dilated_{dil}")
        if dil != 1:
            o, lse = _from_residue_major(o, dil), _from_residue_major(lse, dil)
        outs.append(o)
        lses.append(lse)
    o_c = _dil_combine(outs, lses)

    o_d = _stickbreak(hm, q_head0=HD_QD, k_head0=HD_KD, v_head0=HD_VD, hb=HEADS_PER_STEP, t=SB_T, out_dtype=bf16)

    return o_a, o_b, o_c, o_d


def _small_w_in_t(w_in_t):
    pad = jnp.zeros((w_in_t.shape[0], HEAD_DIM - N_GATES - H_MIX, w_in_t.shape[2]), w_in_t.dtype)
    return jnp.concatenate([w_in_t[:, GATES_AT:GATES_AT + N_GATES], w_in_t[:, FORGET_AT:FORGET_AT + H_MIX], pad],
                           axis=1)


def _layer(h, layer, p, cos, sin, batch, norm_attn, w_in_t, w_small_t, fox_bf, cmp_pe_k, cmp_w1_k, cmp_w2_k,
           cmp_pe_v, cmp_w1_v, cmp_w2_v, w_o, norm_mlp, w_up, w_down, norm_ple, w_ple_gate, w_ple_proj):
    m, d = h.shape
    s = m // batch
    bf16 = jnp.bfloat16
    xn = _rmsnorm(h, norm_attn, bf16)
    hm = _in_proj_heads(xn, w_in_t, layer, cos, sin, batch)
    small = _in_proj_small(xn, w_small_t, layer)
    mix = _mixers(hm, small.reshape(batch, s, HEAD_DIM), fox_bf,
                  cmp_pe_k, cmp_w1_k, cmp_w2_k, cmp_pe_v, cmp_w1_v, cmp_w2_v)

    tm, tn = MM_TM, MM_TN
    res_spec = pl.BlockSpec((tm, tn), lambda i, j: (i, j))
    h = _out_proj([o.reshape(m, -1) for o in mix], w_o, layer, h, tm=tm, tn=tn)

    x2 = _rmsnorm(h, norm_mlp, bf16)
    mid = _matmul(x2, w_up, layer, bf16, _epi_relu2, tm=tm, tn=tn, name="mlp_up")
    h = _matmul_residual_ksplit(mid, w_down, layer, h, name="mlp_down")

    x3 = _rmsnorm(h, norm_ple, bf16)
    ple_dim = p.shape[2]
    h = _matmul(x3, w_ple_gate, layer, jnp.float32, _epi_ple, (h, p, w_ple_proj),
                (res_spec, pl.BlockSpec((None, tm, ple_dim), lambda i, j: (layer, i, 0)),
                 pl.BlockSpec((None, ple_dim, tn), lambda i, j: (layer, 0, j))),
                tm=tm, tn=tn, name="ple_gate")
    return h


def kernel(x, p, positions, norm_attn, w_in, fox_bf, cmp_pe_k, cmp_w1_k, cmp_w2_k, cmp_pe_v, cmp_w1_v, cmp_w2_v,
           w_o, norm_mlp, w_up, w_down, norm_ple, w_ple_gate, w_ple_proj, norm_final):
    batch, s, d = x.shape
    depth = p.shape[0]
    cos, sin = _rope_tables(positions)
    w_in_t = jnp.swapaxes(w_in, 1, 2)
    w_small_t = _small_w_in_t(w_in_t)
    p = p.reshape(depth, batch * s, -1)
    h = x.reshape(batch * s, d)
    for i in range(depth):
        h = _layer(h, i, p, cos, sin, batch, norm_attn[i], w_in_t, w_small_t, fox_bf[i],
                   cmp_pe_k[i], cmp_w1_k[i], cmp_w2_k[i], cmp_pe_v[i], cmp_w1_v[i], cmp_w2_v[i],
                   w_o, norm_mlp[i], w_up, w_down, norm_ple[i], w_ple_gate, w_ple_proj)
    return _rmsnorm(h, norm_final, x.dtype).reshape(batch, s, d)
```

```python
import functools
import math

import numpy as np
import jax
import jax.numpy as jnp
from jax import lax
from jax.experimental import pallas as pl
from jax.experimental.pallas import tpu as pltpu

HEAD_DIM = 128
H_MIX = 8
NSA_KV = 2
NSA_HPG = H_MIX // NSA_KV
CMP_LEN = 32
CMP_STRIDE = 16
SLC_LEN = 64
TOPK = 16
NSA_WINDOW = 512
DIL_CONFIGS = ((128, 1), (512, 4), (2048, 16))
ROPE_THETA = 10000.0
RMS_EPS = 1e-6
NEG = -1e30
M_INIT = -5e29
FORCED_SCORE = 1e9
LOG2E = math.log2(math.e)
Q_SCALE = HEAD_DIM ** -0.5 * LOG2E

VMEM_LIMIT_BYTES = 58 * 1024 * 1024

MXU_DIM = 256
ROW_TILE = 512
MM_TM, MM_TN = 1024, 512
OUT_TM, OUT_TN = 2048, 256
DOWN_TM, DOWN_TN, DOWN_TK = 2048, 1024, 1024
FLASH_T = 512
SB_T = MXU_DIM
BAND_T = 256
CMP_TQ = 512
COMBINE_TQ = 512
CUMSUM_BLK = 128
HEADS_PER_STEP = 4

HD_QA, HD_KC, HD_VC, HD_KS, HD_VS, HD_KW, HD_VW = 0, 8, 10, 12, 14, 16, 18
HD_QB, HD_KB, HD_VB = 20, 28, 36
HD_QC, HD_KC2, HD_VC2 = 44, 52, 60
HD_QD, HD_KD, HD_VD = 68, 76, 84
N_HEADS_ALL = 92
N_BIG = N_HEADS_ALL * HEAD_DIM
GATE_COL0 = 0
FORGET_COL0 = 24
N_GATES = 3 * H_MIX
GATES_AT = (H_MIX + 6 * NSA_KV) * HEAD_DIM
FORGET_AT = GATES_AT + N_GATES + 3 * H_MIX * HEAD_DIM
_ROPED_HEADS = (tuple(range(HD_QA, HD_QA + H_MIX)) + tuple(range(HD_KS, HD_KS + NSA_KV))
                + tuple(range(HD_KW, HD_KW + NSA_KV)) + tuple(range(HD_QC, HD_QC + 2 * H_MIX)))
_QUERY_HEADS = tuple(h for q0 in (HD_QA, HD_QB, HD_QC, HD_QD) for h in range(q0, q0 + H_MIX))


def _cparams(sem):
    return pltpu.CompilerParams(dimension_semantics=sem, vmem_limit_bytes=VMEM_LIMIT_BYTES)


def _dot_nt(a, b):
    return lax.dot_general(a, b, (((1,), (1,)), ((), ())), preferred_element_type=jnp.float32)


def _dot(a, b):
    return jnp.dot(a, b, preferred_element_type=jnp.float32)


def _rmsnorm_kernel(x_ref, g_ref, o_ref):
    x = x_ref[...]
    ms = jnp.mean(x * x, axis=-1, keepdims=True)
    o_ref[...] = (x * lax.rsqrt(ms + RMS_EPS) * g_ref[...]).astype(o_ref.dtype)


def _rmsnorm(x, g, out_dtype, tm=ROW_TILE):
    m, d = x.shape
    return pl.pallas_call(
        _rmsnorm_kernel,
        out_shape=jax.ShapeDtypeStruct((m, d), out_dtype),
        grid=(m // tm,),
        in_specs=[pl.BlockSpec((tm, d), lambda i: (i, 0)),
                  pl.BlockSpec((1, d), lambda i: (0, 0))],
        out_specs=pl.BlockSpec((tm, d), lambda i: (i, 0)),
        compiler_params=_cparams(("parallel",)),
        name="rmsnorm",
    )(x, g.reshape(1, d))


def _mm_kernel(*refs, n_extra, epilogue):
    x_ref, w_ref = refs[0], refs[1]
    extras = refs[2:2 + n_extra]
    o_ref = refs[2 + n_extra]
    acc = _dot(x_ref[...].astype(jnp.bfloat16), w_ref[...].astype(jnp.bfloat16))
    o_ref[...] = epilogue(acc, *extras).astype(o_ref.dtype)


def _epi_none(acc):
    return acc


def _epi_relu2(acc):
    r = jnp.maximum(acc, 0.0)
    return r * r


def _epi_ple(acc, h_ref, p_ref, wp_ref):
    pp = _dot(p_ref[...].astype(jnp.bfloat16), wp_ref[...].astype(jnp.bfloat16))
    return h_ref[...] + jax.nn.sigmoid(acc) * pp


def _matmul(x, w, layer, out_dtype, epilogue=_epi_none, extras=(), extra_specs=(), tm=MM_TM, tn=MM_TN, name="matmul"):
    m, kdim = x.shape
    n = w.shape[2]
    tn = min(tn, n)
    return pl.pallas_call(
        functools.partial(_mm_kernel, n_extra=len(extras), epilogue=epilogue),
        out_shape=jax.ShapeDtypeStruct((m, n), out_dtype),
        grid=(m // tm, n // tn),
        in_specs=[pl.BlockSpec((tm, kdim), lambda i, j: (i, 0)),
                  pl.BlockSpec((None, kdim, tn), lambda i, j: (layer, 0, j)),
                  *extra_specs],
        out_specs=pl.BlockSpec((tm, tn), lambda i, j: (i, j)),
        compiler_params=_cparams(("parallel", "arbitrary")),
        name=name,
    )(x, w, *extras)


def _out_proj_kernel(*refs):
    *x_refs, w_ref, h_ref, o_ref = refs
    acc = h_ref[...]
    row = 0
    for x_ref in x_refs:
        width = x_ref.shape[1]
        acc = acc + _dot(x_ref[...], w_ref[row:row + width, :].astype(jnp.bfloat16))
        row += width
    o_ref[...] = acc


def _out_proj(xs, w, layer, h, tm=OUT_TM, tn=OUT_TN):
    m, n = h.shape
    buffers = [2 if idx < len(xs) // 2 else 1 for idx in range(len(xs))]
    return pl.pallas_call(
        _out_proj_kernel,
        out_shape=jax.ShapeDtypeStruct((m, n), jnp.float32),
        grid=(m // tm, n // tn),
        in_specs=[*[pl.BlockSpec((tm, x.shape[1]), lambda i, j: (i, 0), pipeline_mode=pl.Buffered(nb))
                    for x, nb in zip(xs, buffers)],
                  pl.BlockSpec((None, w.shape[1], tn), lambda i, j: (layer, 0, j)),
                  pl.BlockSpec((tm, tn), lambda i, j: (i, j))],
        out_specs=pl.BlockSpec((tm, tn), lambda i, j: (i, j)),
        compiler_params=_cparams(("parallel", "arbitrary")),
        name="out_proj",
    )(*xs, w, h)


def _mm_residual_ksplit_kernel(x_ref, w_ref, h_ref, o_ref):
    @pl.when(pl.program_id(2) == 0)
    def _():
        o_ref[...] = h_ref[...] + _dot(x_ref[...], w_ref[...].astype(jnp.bfloat16))

    @pl.when(pl.program_id(2) > 0)
    def _():
        o_ref[...] += _dot(x_ref[...], w_ref[...].astype(jnp.bfloat16))


def _matmul_residual_ksplit(x, w, layer, h, tm=DOWN_TM, tn=DOWN_TN, tk=DOWN_TK, name="matmul_ksplit"):
    m, kdim = x.shape
    n = w.shape[2]
    return pl.pallas_call(
        _mm_residual_ksplit_kernel,
        out_shape=jax.ShapeDtypeStruct((m, n), jnp.float32),
        grid=(m // tm, n // tn, kdim // tk),
        in_specs=[pl.BlockSpec((tm, tk), lambda i, j, k: (i, k)),
                  pl.BlockSpec((None, tk, tn), lambda i, j, k: (layer, k, j)),
                  pl.BlockSpec((tm, tn), lambda i, j, k: (i, j))],
        out_specs=pl.BlockSpec((tm, tn), lambda i, j, k: (i, j)),
        compiler_params=_cparams(("parallel", "parallel", "arbitrary")),
        name=name,
    )(x, w, h)


def _rope_table_kernel(pos_ref, freq_ref, sign_ref, cos_ref, sin_ref):
    ang = pos_ref[...] * freq_ref[...]
    cos_ref[...] = jnp.cos(ang)
    sin_ref[...] = jnp.sin(ang) * sign_ref[...]


def _rope_tables(positions, ts=COMBINE_TQ):
    n = positions.size
    half = HEAD_DIM // 2
    inv_freq = ROPE_THETA ** (-jnp.arange(half, dtype=jnp.float32) / half)
    freq = jnp.concatenate([inv_freq, inv_freq]).reshape(1, HEAD_DIM)
    sign = jnp.concatenate([-jnp.ones((half,), jnp.float32), jnp.ones((half,), jnp.float32)]).reshape(1, HEAD_DIM)
    pos = positions.astype(jnp.float32).reshape(n, 1)
    row = pl.BlockSpec((1, HEAD_DIM), lambda i: (0, 0))
    return pl.pallas_call(
        _rope_table_kernel,
        out_shape=[jax.ShapeDtypeStruct((n, HEAD_DIM), jnp.float32)] * 2,
        grid=(n // ts,),
        in_specs=[pl.BlockSpec((ts, 1), lambda i: (i, 0)), row, row],
        out_specs=[pl.BlockSpec((ts, HEAD_DIM), lambda i: (i, 0))] * 2,
        compiler_params=_cparams(("parallel",)),
        name="rope_tables",
    )(pos, freq, sign)


_FLAG_ROPED, _FLAG_QUERY = 1, 2


def _in_proj_kernel(flags_ref, row0_ref, x_ref, wt_ref, cos_ref, sin_ref, o_ref):
    j = pl.program_id(1)
    heads = o_ref.shape[1]
    acc = _dot_nt(x_ref[...], wt_ref[0].astype(jnp.bfloat16))
    for hh in range(heads):
        t = acc[:, hh * HEAD_DIM:(hh + 1) * HEAD_DIM]
        flags = flags_ref[j * heads + hh]
        roped = (flags & _FLAG_ROPED) > 0
        scale = jnp.where((flags & _FLAG_QUERY) > 0, Q_SCALE, 1.0)
        cos = jnp.where(roped, cos_ref[...], 1.0)
        sin = jnp.where(roped, sin_ref[...], 0.0)
        o_ref[0, hh] = ((t * cos + pltpu.roll(t, HEAD_DIM // 2, 1) * sin) * scale).astype(o_ref.dtype)


def _in_proj_heads(xn, w_in_t, layer, cos, sin, batch, tm=MM_TM, tn=MM_TN):
    m, d = xn.shape
    s = m // batch
    heads = tn // HEAD_DIM
    q_tiles = s // tm
    flags = np.zeros((N_HEADS_ALL,), np.int32)
    flags[list(_ROPED_HEADS)] |= _FLAG_ROPED
    flags[list(_QUERY_HEADS)] |= _FLAG_QUERY
    col = np.arange(0, N_BIG, tn)
    row0 = col + np.where(col >= GATES_AT, N_GATES, 0) + np.where(col >= FORGET_AT - N_GATES, H_MIX, 0)
    table = pl.BlockSpec((tm, HEAD_DIM), lambda i, j, f, r: (i, 0))
    return pl.pallas_call(
        _in_proj_kernel,
        out_shape=jax.ShapeDtypeStruct((batch, N_HEADS_ALL, s, HEAD_DIM), jnp.bfloat16),
        grid_spec=pltpu.PrefetchScalarGridSpec(
            num_scalar_prefetch=2,
            grid=(m // tm, N_BIG // tn),
            in_specs=[pl.BlockSpec((tm, d), lambda i, j, f, r: (i, 0)),
                      pl.BlockSpec((pl.Element(1), pl.Element(tn), pl.Element(d)),
                                   lambda i, j, f, r: (layer, pl.multiple_of(r[j], 8), 0)),
                      table, table],
            out_specs=pl.BlockSpec((1, heads, tm, HEAD_DIM),
                                   lambda i, j, f, r: (i // q_tiles, j, i % q_tiles, 0)),
        ),
        compiler_params=_cparams(("parallel", "arbitrary")),
        name="in_proj",
    )(jnp.asarray(flags), jnp.asarray(row0.astype(np.int32)), xn, w_in_t, cos, sin)


def _in_proj_small_kernel(x_ref, wt_ref, o_ref):
    o_ref[...] = _dot_nt(x_ref[...], wt_ref[...].astype(jnp.bfloat16))


def _in_proj_small(xn, w_small_t, layer, tm=MM_TM):
    m, d = xn.shape
    n = w_small_t.shape[1]
    return pl.pallas_call(
        _in_proj_small_kernel,
        out_shape=jax.ShapeDtypeStruct((m, n), jnp.float32),
        grid=(m // tm,),
        in_specs=[pl.BlockSpec((tm, d), lambda i: (i, 0)),
                  pl.BlockSpec((None, n, d), lambda i: (layer, 0, 0))],
        out_specs=pl.BlockSpec((tm, n), lambda i: (i, 0)),
        compiler_params=_cparams(("parallel",)),
        name="in_proj_small",
    )(xn, w_small_t)


def _softplus(z):
    return jnp.maximum(z, 0.0) + jnp.log1p(jnp.exp(-jnp.abs(z)))


def _logf_cumsum_kernel(x_ref, bias_ref, o_ref, *, blk):
    s = x_ref.shape[1]
    r = lax.broadcasted_iota(jnp.int32, (blk, blk), 0)
    c = lax.broadcasted_iota(jnp.int32, (blk, blk), 1)
    tri = jnp.where(c <= r, 1.0, 0.0).astype(jnp.float32)

    def body(i, carry):
        x = x_ref[0, pl.ds(i * blk, blk), :]
        logf = -_softplus(-(x + bias_ref[...]))
        cs = jnp.dot(tri, logf, preferred_element_type=jnp.float32,
                     precision=lax.Precision.HIGHEST) + carry
        o_ref[0, pl.ds(i * blk, blk), :] = cs
        return cs[blk - 1:blk, :]

    lax.fori_loop(0, s // blk, body, jnp.zeros((1, x_ref.shape[2]), jnp.float32))


def _logf_cumsum(small, bias_row, blk=CUMSUM_BLK):
    b, s, w = small.shape
    return pl.pallas_call(
        functools.partial(_logf_cumsum_kernel, blk=blk),
        out_shape=jax.ShapeDtypeStruct((b, s, w), jnp.float32),
        grid=(b,),
        in_specs=[pl.BlockSpec((1, s, w), lambda bi: (bi, 0, 0)),
                  pl.BlockSpec((1, w), lambda bi: (0, 0))],
        out_specs=pl.BlockSpec((1, s, w), lambda bi: (bi, 0, 0)),
        compiler_params=_cparams(("parallel",)),
        name="logf_cumsum",
    )(small, bias_row)


def _rowmax_lanes(s):
    m = s[:, :HEAD_DIM]
    for c in range(1, s.shape[1] // HEAD_DIM):
        m = jnp.maximum(m, s[:, c * HEAD_DIM:(c + 1) * HEAD_DIM])
    return jnp.broadcast_to(jnp.max(m, axis=-1, keepdims=True), m.shape)


def _fold_lanes(p):
    a = p[:, :HEAD_DIM]
    for c in range(1, p.shape[1] // HEAD_DIM):
        a = a + p[:, c * HEAD_DIM:(c + 1) * HEAD_DIM]
    return a


def _tile_lanes(x, width):
    return jnp.concatenate([x] * (width // HEAD_DIM), axis=1)


def _causal_flash_kernel(*refs, mode, hb, t):
    it = iter(refs)
    q_ref, k_ref, v_ref = next(it), next(it), next(it)
    bias_ref = next(it) if mode == "fox" else None
    sel_ref, onehot_ref = (next(it), next(it)) if mode == "sel" else (None, None)
    o_ref = next(it)
    m_scr, l_scr, acc_scr = next(it), next(it), next(it)
    i = pl.program_id(2)

    m_scr[...] = jnp.full(m_scr.shape, M_INIT, jnp.float32)
    l_scr[...] = jnp.zeros(l_scr.shape, jnp.float32)
    acc_scr[...] = jnp.zeros(acc_scr.shape, jnp.float32)

    def tile(j, diagonal):
        keys = pl.ds(pl.multiple_of(j * t, t), t)
        if mode == "sel":
            k_shared = jnp.concatenate([k_ref[0, 0, keys, :], onehot_ref[keys, :]], axis=1)
            scores = [_dot_nt(jnp.concatenate([q_ref[0, hh], sel_ref[0, 0]], axis=1), k_shared)
                      for hh in range(hb)]
        else:
            scores = [_dot_nt(q_ref[0, hh], k_ref[0, hh, keys, :]) + bias_ref[0, hh, j] for hh in range(hb)]
        if diagonal:
            causal = (lax.broadcasted_iota(jnp.int32, (t, t), 1) <= lax.broadcasted_iota(jnp.int32, (t, t), 0))
            scores = [jnp.where(causal, s, NEG) for s in scores]
        probs = []
        for hh, s in enumerate(scores):
            m_prev = m_scr[hh]
            m_new = jnp.maximum(m_prev, _rowmax_lanes(s))
            alpha = jnp.exp2(m_prev - m_new)
            p = jnp.exp2(s - _tile_lanes(m_new, t))
            l_scr[hh] = alpha * l_scr[hh] + _fold_lanes(p)
            m_scr[hh] = m_new
            probs.append((alpha, p.astype(v_ref.dtype)))
        for hh, (alpha, p) in enumerate(probs):
            v = v_ref[0, 0 if mode == "sel" else hh, keys, :]
            acc_scr[hh] = alpha * acc_scr[hh] + _dot(p, v)

    def off_diagonal(j, carry):
        tile(j, False)
        return carry

    lax.fori_loop(0, i, off_diagonal, 0)
    tile(i, True)
    for hh in range(hb):
        l = jnp.sum(l_scr[hh], axis=-1, keepdims=True)
        o_ref[0, :, hh * HEAD_DIM:(hh + 1) * HEAD_DIM] = (acc_scr[hh] / l).astype(o_ref.dtype)


def _causal_flash(hm, *, mode, q_head0, k_head0, v_head0, hb, t, out_dtype, bias=None, sel=None, onehot=None,
                  name="causal_flash"):
    b, _, s, d = hm.shape
    groups = H_MIX // hb
    kv_heads = hb if mode == "fox" else 1
    in_specs = [pl.BlockSpec((1, hb, t, d), lambda bi, g, i: (bi, q_head0 // hb + g, i, 0)),
                pl.BlockSpec((1, kv_heads, s, d), lambda bi, g, i: (bi, k_head0 // kv_heads + g, 0, 0)),
                pl.BlockSpec((1, kv_heads, s, d), lambda bi, g, i: (bi, v_head0 // kv_heads + g, 0, 0))]
    args = [hm, hm, hm]
    if mode == "fox":
        in_specs.append(pl.BlockSpec((1, hb, s // t, 1, t), lambda bi, g, i: (bi, g, 0, 0, 0)))
        args.append(bias)
    else:
        in_specs.append(pl.BlockSpec((1, 1, t, sel.shape[-1]), lambda bi, g, i: (bi, g, i, 0)))
        in_specs.append(pl.BlockSpec(onehot.shape, lambda bi, g, i: (0, 0)))
        args += [sel, onehot]
    return pl.pallas_call(
        functools.partial(_causal_flash_kernel, mode=mode, hb=hb, t=t),
        out_shape=jax.ShapeDtypeStruct((b, s, H_MIX * d), out_dtype),
        grid=(b, groups, s // t),
        in_specs=in_specs,
        out_specs=pl.BlockSpec((1, t, hb * d), lambda bi, g, i: (bi, i, g)),
        scratch_shapes=[pltpu.VMEM((hb, t, d), jnp.float32)] * 3,
        compiler_params=_cparams(("parallel", "parallel", "arbitrary")),
        name=name,
    )(*args)


def _band_kernel(*refs, hb, kv_heads, t, wpad, window, seg_len, has_lse):
    q_ref, k_ref, v_ref, o_ref = refs[:4]
    lse_ref = refs[4] if has_lse else None
    s_len = k_ref.shape[2]
    span = t + wpad
    t0 = pl.program_id(2) * t
    seg0 = (t0 // seg_len) * seg_len
    start = jnp.minimum(jnp.maximum(t0 - wpad, seg0), s_len - span)
    start = pl.multiple_of(start, HEAD_DIM)
    keys = pl.ds(start, span)
    qpos = t0 + lax.broadcasted_iota(jnp.int32, (t, span), 0)
    kpos = start + lax.broadcasted_iota(jnp.int32, (t, span), 1)
    dist = qpos - kpos
    mask = (dist >= 0) & (dist <= window) & (kpos >= seg0)
    kv_of = [hh * kv_heads // hb for hh in range(hb)]
    scores = [jnp.where(mask, _dot_nt(q_ref[0, hh], k_ref[0, kv_of[hh], keys, :]), NEG) for hh in range(hb)]
    probs = []
    for s in scores:
        m = _rowmax_lanes(s)
        p = jnp.exp2(s - _tile_lanes(m, span))
        l = jnp.broadcast_to(jnp.sum(_fold_lanes(p), axis=-1, keepdims=True), m.shape)
        probs.append((m, l, p.astype(v_ref.dtype)))
    lane = lax.broadcasted_iota(jnp.int32, (t, HEAD_DIM), 1)
    lse_all = jnp.zeros((t, HEAD_DIM), jnp.float32)
    for hh, (m, l, p) in enumerate(probs):
        out = _dot(p, v_ref[0, kv_of[hh], keys, :]) / l
        if has_lse:
            o_ref[0, hh] = out.astype(o_ref.dtype)
            lse_all = jnp.where(lane == hh, m + jnp.log2(l), lse_all)
        else:
            o_ref[0, :, hh * HEAD_DIM:(hh + 1) * HEAD_DIM] = out.astype(o_ref.dtype)
    if has_lse:
        lse_ref[0, 0] = lse_all


def _band(src, *, q_head0, k_head0, v_head0, hb, kv_heads, t, wpad, window, seg_len, has_lse, name):
    b, _, s, d = src.shape
    groups = H_MIX // hb
    in_specs = [pl.BlockSpec((1, hb, t, d), lambda bi, g, i: (bi, q_head0 // hb + g, i, 0)),
                pl.BlockSpec((1, kv_heads, s, d), lambda bi, g, i: (bi, k_head0 // kv_heads + g, 0, 0)),
                pl.BlockSpec((1, kv_heads, s, d), lambda bi, g, i: (bi, v_head0 // kv_heads + g, 0, 0))]
    if has_lse:
        out_shape = [jax.ShapeDtypeStruct((b, H_MIX, s, d), jnp.float32),
                     jax.ShapeDtypeStruct((b, groups, s, d), jnp.float32)]
        out_specs = [pl.BlockSpec((1, hb, t, d), lambda bi, g, i: (bi, g, i, 0)),
                     pl.BlockSpec((1, 1, t, d), lambda bi, g, i: (bi, g, i, 0))]
    else:
        out_shape = jax.ShapeDtypeStruct((b, s, H_MIX * d), jnp.float32)
        out_specs = pl.BlockSpec((1, t, hb * d), lambda bi, g, i: (bi, i, g))
    return pl.pallas_call(
        functools.partial(_band_kernel, hb=hb, kv_heads=kv_heads, t=t, wpad=wpad, window=window,
                          seg_len=seg_len, has_lse=has_lse),
        out_shape=out_shape,
        grid=(b, groups, s // t),
        in_specs=in_specs,
        out_specs=out_specs,
        compiler_params=_cparams(("parallel", "parallel", "arbitrary")),
        name=name,
    )(src, src, src)


def _stickbreak_kernel(q_ref, k_ref, v_ref, u2_ref, o_ref, run_scr, acc_scr, *, hb, t):
    i = pl.program_id(2)
    run_scr[...] = jnp.zeros(run_scr.shape, jnp.float32)
    acc_scr[...] = jnp.zeros(acc_scr.shape, jnp.float32)
    u2 = u2_ref[...]
    sign_bit = jnp.uint32(0x80000000)

    def tile(j, diagonal):
        keys = pl.ds(pl.multiple_of(j * t, t), t)
        if diagonal:
            strict = (lax.broadcasted_iota(jnp.int32, (t, t), 1) < lax.broadcasted_iota(jnp.int32, (t, t), 0))
        zs = [_dot_nt(q_ref[0, hh], k_ref[0, hh, keys, :]) for hh in range(hb)]
        stage = []
        for z2 in zs:
            neg_abs = lax.bitcast_convert_type(lax.bitcast_convert_type(z2, jnp.uint32) | sign_bit, jnp.float32)
            lg = jnp.log2(1.0 + jnp.exp2(neg_abs))
            sp2 = jnp.maximum(z2, 0.0) + lg
            if diagonal:
                sp2 = jnp.where(strict, sp2, 0.0)
            hi = sp2.astype(jnp.bfloat16)
            lo = (sp2 - hi.astype(jnp.float32)).astype(jnp.bfloat16)
            stage.append((jnp.minimum(z2, 0.0) - lg, sp2[:, 0:1], jnp.concatenate([hi, lo], axis=1)))
        laters = [_dot(hl, u2) for (_, _, hl) in stage]
        weights = []
        for hh, later in enumerate(laters):
            log_beta, first, _ = stage[hh]
            run = run_scr[hh]
            a = jnp.exp2(log_beta - (later + _tile_lanes(run, t)))
            if diagonal:
                a = jnp.where(strict, a, 0.0)
            run_scr[hh] = run + jnp.broadcast_to(later[:, 0:1] + first, run.shape)
            weights.append(a.astype(v_ref.dtype))
        for hh, a in enumerate(weights):
            acc_scr[hh] += _dot(a, v_ref[0, hh, keys, :])

    def off_diagonal(n, carry):
        tile(i - 1 - n, False)
        return carry

    tile(i, True)
    lax.fori_loop(0, i, off_diagonal, 0)
    for hh in range(hb):
        o_ref[0, :, hh * HEAD_DIM:(hh + 1) * HEAD_DIM] = acc_scr[hh].astype(o_ref.dtype)


def _stickbreak(hm, *, q_head0, k_head0, v_head0, hb, t, out_dtype):
    b, _, s, d = hm.shape
    later = np.arange(t)[:, None] > np.arange(t)[None, :]
    u2 = jnp.asarray(np.concatenate([later, later], axis=0), jnp.bfloat16)

    def heads(h0):
        return pl.BlockSpec((1, hb, s, d), lambda bi, g, i: (bi, h0 // hb + g, 0, 0))

    return pl.pallas_call(
        functools.partial(_stickbreak_kernel, hb=hb, t=t),
        out_shape=jax.ShapeDtypeStruct((b, s, H_MIX * d), out_dtype),
        grid=(b, H_MIX // hb, s // t),
        in_specs=[pl.BlockSpec((1, hb, t, d), lambda bi, g, i: (bi, q_head0 // hb + g, i, 0)),
                  heads(k_head0), heads(v_head0),
                  pl.BlockSpec((2 * t, t), lambda bi, g, i: (0, 0))],
        out_specs=pl.BlockSpec((1, t, hb * d), lambda bi, g, i: (bi, i, g)),
        scratch_shapes=[pltpu.VMEM((hb, t, d), jnp.float32), pltpu.VMEM((hb, t, d), jnp.float32)],
        compiler_params=_cparams(("parallel", "parallel", "arbitrary")),
        name="stickbreak",
    )(hm, hm, hm, u2)


def _gelu_tanh(x):
    return 0.5 * x * (1.0 + jnp.tanh(math.sqrt(2.0 / math.pi) * (x + 0.044715 * (x * x * x))))


def _compress_kernel(x_ref, pe_ref, w1_ref, w2_ref, o_ref):
    half = x_ref.shape[3]
    x = x_ref[0, 0]
    w1 = w1_ref[0].astype(jnp.bfloat16)
    first = _dot(x, w1[:half])
    second = _dot(x, w1[half:])
    pe = jnp.broadcast_to(pe_ref[0], (8, pe_ref.shape[2])).astype(jnp.bfloat16)
    pe_term = _dot(pe, w1)[0:1]
    n_chunks = x.shape[0]
    hid = first + pltpu.roll(second, n_chunks - 1, 0) + pe_term
    act = _gelu_tanh(hid)
    o_ref[0, 0] = _dot(act.astype(jnp.bfloat16), w2_ref[0].astype(jnp.bfloat16))


def _compress(chunks, pe, w1, w2):
    b, nh, nc, half = chunks.shape
    return pl.pallas_call(
        _compress_kernel,
        out_shape=jax.ShapeDtypeStruct((b, nh, nc, HEAD_DIM), jnp.float32),
        grid=(b, nh),
        in_specs=[pl.BlockSpec((1, 1, nc, half), lambda bi, h: (bi, h, 0, 0)),
                  pl.BlockSpec((1, 1, 2 * half), lambda bi, h: (h // NSA_KV, 0, 0)),
                  pl.BlockSpec((1, 2 * half, w1.shape[2]), lambda bi, h: (h // NSA_KV, 0, 0)),
                  pl.BlockSpec((1, w2.shape[1], HEAD_DIM), lambda bi, h: (h // NSA_KV, 0, 0))],
        out_specs=pl.BlockSpec((1, 1, nc, HEAD_DIM), lambda bi, h: (bi, h, 0, 0)),
        compiler_params=_cparams(("parallel", "parallel")),
        name="nsa_compress",
    )(chunks, pe, w1, w2)


def _cmp_select_kernel(q_ref, kc_ref, vc_ref, c2s_t_ref, o_ref, sel_ref, *, tq, n_cmp):
    i = pl.program_id(2)
    nc = kc_ref.shape[2]
    kc = kc_ref[0, 0].astype(jnp.bfloat16)
    vc = vc_ref[0, 0].astype(jnp.bfloat16)
    t = i * tq + lax.broadcasted_iota(jnp.int32, (1, tq), 1)
    cidx = lax.broadcasted_iota(jnp.int32, (nc, 1), 0)
    visible = (cidx * CMP_STRIDE + (CMP_LEN - 1) <= t) & (cidx < n_cmp)
    psum = jnp.zeros((nc, tq), jnp.float32)
    for hh in range(NSA_HPG):
        s = jnp.where(visible, _dot_nt(kc, q_ref[0, hh]), NEG)
        m = jnp.max(s, axis=0, keepdims=True)
        e = jnp.where(visible, jnp.exp2(s - m), 0.0)
        l = jnp.sum(e, axis=0, keepdims=True)
        p = e / jnp.where(l > 0.0, l, 1.0)
        psum = psum + p
        o_ref[0, :, hh * HEAD_DIM:(hh + 1) * HEAD_DIM] = _dot(p.T.astype(jnp.bfloat16), vc)

    imp = jnp.dot(c2s_t_ref[...], psum, preferred_element_type=jnp.float32, precision=lax.Precision.HIGHEST)
    w = imp.shape[0]
    n_slc = w // 2
    score_rows = slice(0, n_slc)
    jblk = lax.broadcasted_iota(jnp.int32, (n_slc, 1), 0)
    cur = lax.shift_right_logical(t, int(math.log2(SLC_LEN)))
    forced = (jblk == 0) | (jblk == cur) | (jblk == cur - 1)
    causal_blk = jblk * SLC_LEN <= t
    score = jnp.where(causal_blk, jnp.where(forced, FORCED_SCORE, imp[score_rows]), -1.0)
    rank = jnp.zeros((n_slc, tq), jnp.float32)
    for jp in range(n_slc):
        row = score[jp:jp + 1, :]
        before = (row > score) | ((row == score) & (jblk > jp))
        rank = rank + jnp.where(before, 1.0, 0.0)
    chosen = (rank < float(min(TOPK, n_slc))) & causal_blk
    add_mask = jnp.concatenate([jnp.where(chosen, 0.0, NEG), jnp.zeros((w - n_slc, tq), jnp.float32)], axis=0)
    sel_ref[0, 0] = add_mask.T.astype(sel_ref.dtype)


def _cmp_select(hm, cmp_kv, c2s, *, tq, n_cmp):
    b, _, s, d = hm.shape
    nc = cmp_kv.shape[2]
    w = c2s.shape[1]
    return pl.pallas_call(
        functools.partial(_cmp_select_kernel, tq=tq, n_cmp=n_cmp),
        out_shape=[jax.ShapeDtypeStruct((b, s, H_MIX * d), jnp.float32),
                   jax.ShapeDtypeStruct((b, NSA_KV, s, w), jnp.bfloat16)],
        grid=(b, NSA_KV, s // tq),
        in_specs=[pl.BlockSpec((1, NSA_HPG, tq, d), lambda bi, g, i: (bi, g, i, 0)),
                  pl.BlockSpec((1, 1, nc, d), lambda bi, g, i: (bi, g, 0, 0)),
                  pl.BlockSpec((1, 1, nc, d), lambda bi, g, i: (bi, NSA_KV + g, 0, 0)),
                  pl.BlockSpec((w, nc), lambda bi, g, i: (0, 0))],
        out_specs=[pl.BlockSpec((1, tq, NSA_HPG * d), lambda bi, g, i: (bi, i, g)),
                   pl.BlockSpec((1, 1, tq, w), lambda bi, g, i: (bi, g, i, 0))],
        compiler_params=_cparams(("parallel", "parallel", "parallel")),
        name="nsa_cmp_select",
    )(hm, cmp_kv, cmp_kv, c2s.T)


def _nsa_combine_kernel(oc_ref, os_ref, ow_ref, g_ref, o_ref):
    gates = jax.nn.sigmoid(g_ref[0])
    for h in range(H_MIX):
        cols = slice(h * HEAD_DIM, (h + 1) * HEAD_DIM)
        c0 = GATE_COL0 + 3 * h
        out = (gates[:, c0:c0 + 1] * oc_ref[0, :, cols]
               + gates[:, c0 + 1:c0 + 2] * os_ref[0, :, cols]
               + gates[:, c0 + 2:c0 + 3] * ow_ref[0, :, cols])
        o_ref[0, :, cols] = out.astype(o_ref.dtype)


def _nsa_combine(o_cmp, o_slc, o_win, small, tq=COMBINE_TQ):
    b, s, w = o_cmp.shape
    big = pl.BlockSpec((1, tq, w), lambda bi, i: (bi, i, 0))
    return pl.pallas_call(
        _nsa_combine_kernel,
        out_shape=jax.ShapeDtypeStruct((b, s, w), jnp.bfloat16),
        grid=(b, s // tq),
        in_specs=[big, big, big, pl.BlockSpec((1, tq, small.shape[2]), lambda bi, i: (bi, i, 0))],
        out_specs=big,
        compiler_params=_cparams(("parallel", "parallel")),
        name="nsa_combine",
    )(o_cmp, o_slc, o_win, small)


def _dil_combine_kernel(*refs):
    n = (len(refs) - 1) // 2
    o_refs, lse_refs, out_ref = refs[:n], refs[n:2 * n], refs[-1]
    for h in range(H_MIX):
        lses = []
        for lse_ref in lse_refs:
            hb = H_MIX // lse_ref.shape[1]
            lses.append(lse_ref[0, h // hb][:, h % hb:h % hb + 1])
        m = functools.reduce(jnp.maximum, lses)
        es = [jnp.exp2(l - m) for l in lses]
        tot = functools.reduce(lambda a, b: a + b, es)
        out = functools.reduce(lambda a, b: a + b, [o_ref[0, h] * (e / tot) for o_ref, e in zip(o_refs, es)])
        out_ref[0, :, h * HEAD_DIM:(h + 1) * HEAD_DIM] = out.astype(out_ref.dtype)


def _dil_combine(outs, lses, tq=COMBINE_TQ):
    b, h, s, d = outs[0].shape
    o_spec = pl.BlockSpec((1, h, tq, d), lambda bi, i: (bi, 0, i, 0))
    lse_specs = [pl.BlockSpec((1, l.shape[1], tq, d), lambda bi, i: (bi, 0, i, 0)) for l in lses]
    return pl.pallas_call(
        _dil_combine_kernel,
        out_shape=jax.ShapeDtypeStruct((b, s, h * d), jnp.bfloat16),
        grid=(b, s // tq),
        in_specs=[o_spec] * len(outs) + lse_specs,
        out_specs=pl.BlockSpec((1, tq, h * d), lambda bi, i: (bi, i, 0)),
        compiler_params=_cparams(("parallel", "parallel")),
        name="dil_combine",
    )(*outs, *lses)


def _to_residue_major(t, dil):
    b, h, s, d = t.shape
    return t.reshape(b, h, s // dil, dil, d).transpose(0, 1, 3, 2, 4).reshape(b, h, s, d)


def _from_residue_major(t, dil):
    b, h, s, d = t.shape
    return t.reshape(b, h, dil, s // dil, d).transpose(0, 1, 3, 2, 4).reshape(b, h, s, d)


def _cmp_to_slc_matrix(n_chunks, n_cmp, n_slc):
    ratio = SLC_LEN // CMP_STRIDE
    span = CMP_LEN // CMP_STRIDE
    jj, mm, nn = np.meshgrid(np.arange(n_slc), np.arange(ratio), np.arange(span), indexing="ij")
    cc = ratio * jj + mm + nn
    keep = cc < n_cmp
    mat = np.zeros((n_chunks, 2 * n_slc), np.float32)
    np.add.at(mat, (cc[keep], jj[keep]), 1.0)
    return jnp.asarray(mat)


def _mixers(hm, small, fox_bf, cmp_pe_k, cmp_w1_k, cmp_w2_k, cmp_pe_v, cmp_w1_v, cmp_w2_v):
    b, _, s, d = hm.shape
    bf16 = jnp.bfloat16

    n_chunks = s // CMP_STRIDE
    n_cmp = (s - CMP_LEN) // CMP_STRIDE + 1
    n_slc = s // SLC_LEN
    chunks = hm[:, HD_KC:HD_KC + 2 * NSA_KV].reshape(b, 2 * NSA_KV, n_chunks, CMP_STRIDE * d)
    cmp_kv = _compress(chunks,
                       jnp.stack([cmp_pe_k, cmp_pe_v]).reshape(2, 1, CMP_LEN * d),
                       jnp.stack([cmp_w1_k, cmp_w1_v]), jnp.stack([cmp_w2_k, cmp_w2_v]))
    o_cmp, sel = _cmp_select(hm, cmp_kv, _cmp_to_slc_matrix(n_chunks, n_cmp, n_slc), tq=CMP_TQ, n_cmp=n_cmp)
    onehot = jnp.asarray((np.arange(s)[:, None] // SLC_LEN) == np.arange(2 * n_slc)[None, :], bf16)
    o_slc = _causal_flash(hm, mode="sel", q_head0=HD_QA, k_head0=HD_KS, v_head0=HD_VS, hb=NSA_HPG, t=FLASH_T,
                          sel=sel, onehot=onehot, out_dtype=jnp.float32, name="nsa_selected")
    o_win = _band(hm, q_head0=HD_QA, k_head0=HD_KW, v_head0=HD_VW, hb=NSA_HPG, kv_heads=1, t=BAND_T, wpad=NSA_WINDOW,
                  window=NSA_WINDOW - 1, seg_len=s, has_lse=False, name="nsa_window")
    o_a = _nsa_combine(o_cmp, o_slc, o_win, small)

    t_fox = FLASH_T
    bias_row = jnp.zeros((1, small.shape[2]), jnp.float32).at[0, FORGET_COL0:FORGET_COL0 + H_MIX].set(fox_bf)
    csum = _logf_cumsum(small, bias_row)
    key_bias = (-LOG2E) * csum[:, :, FORGET_COL0:FORGET_COL0 + H_MIX].transpose(0, 2, 1)
    o_b = _causal_flash(hm, mode="fox", q_head0=HD_QB, k_head0=HD_KB, v_head0=HD_VB, hb=HEADS_PER_STEP, t=t_fox,
                        bias=key_bias.reshape(b, H_MIX, s // t_fox, 1, t_fox), out_dtype=bf16, name="fox")

    outs, lses = [], []
    for window, dil in DIL_CONFIGS:
        if dil == 1:
            src, heads0 = hm, (HD_QC, HD_KC2, HD_VC2)
        else:
            src, heads0 = _to_residue_major(hm[:, HD_QC:HD_QC + 3 * H_MIX], dil), (0, H_MIX, 2 * H_MIX)
        hb = HEADS_PER_STEP if dil == 1 else H_MIX
        o, lse = _band(src, q_head0=heads0[0], k_head0=heads0[1], v_head0=heads0[2], hb=hb, kv_heads=hb, t=BAND_T,
                       wpad=HEAD_DIM, window=window // dil, seg_len=s // dil, has_lse=True, name=f"dilated_{dil}")
        if dil != 1:
            o, lse = _from_residue_major(o, dil), _from_residue_major(lse, dil)
        outs.append(o)
        lses.append(lse)
    o_c = _dil_combine(outs, lses)

    o_d = _stickbreak(hm, q_head0=HD_QD, k_head0=HD_KD, v_head0=HD_VD, hb=HEADS_PER_STEP, t=SB_T, out_dtype=bf16)

    return o_a, o_b, o_c, o_d


def _small_w_in_t(w_in_t):
    pad = jnp.zeros((w_in_t.shape[0], HEAD_DIM - N_GATES - H_MIX, w_in_t.shape[2]), w_in_t.dtype)
    return jnp.concatenate([w_in_t[:, GATES_AT:GATES_AT + N_GATES], w_in_t[:, FORGET_AT:FORGET_AT + H_MIX], pad],
                           axis=1)


def _layer(h, layer, p, cos, sin, batch, norm_attn, w_in_t, w_small_t, fox_bf, cmp_pe_k, cmp_w1_k, cmp_w2_k,
           cmp_pe_v, cmp_w1_v, cmp_w2_v, w_o, norm_mlp, w_up, w_down, norm_ple, w_ple_gate, w_ple_proj):
    m, d = h.shape
    s = m // batch
    bf16 = jnp.bfloat16
    xn = _rmsnorm(h, norm_attn, bf16)
    hm = _in_proj_heads(xn, w_in_t, layer, cos, sin, batch)
    small = _in_proj_small(xn, w_small_t, layer)
    mix = _mixers(hm, small.reshape(batch, s, HEAD_DIM), fox_bf,
                  cmp_pe_k, cmp_w1_k, cmp_w2_k, cmp_pe_v, cmp_w1_v, cmp_w2_v)

    tm, tn = MM_TM, MM_TN
    res_spec = pl.BlockSpec((tm, tn), lambda i, j: (i, j))
    h = _out_proj([o.reshape(m, -1) for o in mix], w_o, layer, h)

    x2 = _rmsnorm(h, norm_mlp, bf16)
    mid = _matmul(x2, w_up, layer, bf16, _epi_relu2, tm=tm, tn=tn, name="mlp_up")
    h = _matmul_residual_ksplit(mid, w_down, layer, h, name="mlp_down")

    x3 = _rmsnorm(h, norm_ple, bf16)
    ple_dim = p.shape[2]
    h = _matmul(x3, w_ple_gate, layer, jnp.float32, _epi_ple, (h, p, w_ple_proj),
                (res_spec, pl.BlockSpec((None, tm, ple_dim), lambda i, j: (layer, i, 0)),
                 pl.BlockSpec((None, ple_dim, tn), lambda i, j: (layer, 0, j))),
                tm=tm, tn=tn, name="ple_gate")
    return h


def kernel(x, p, positions, norm_attn, w_in, fox_bf, cmp_pe_k, cmp_w1_k, cmp_w2_k, cmp_pe_v, cmp_w1_v, cmp_w2_v,
           w_o, norm_mlp, w_up, w_down, norm_ple, w_ple_gate, w_ple_proj, norm_final):
    batch, s, d = x.shape
    depth = p.shape[0]
    cos, sin = _rope_tables(positions)
    w_in_t = jnp.swapaxes(w_in, 1, 2)
    w_small_t = _small_w_in_t(w_in_t)
    p = p.reshape(depth, batch * s, -1)
    h = x.reshape(batch * s, d)
    for i in range(depth):
        h = _layer(h, i, p, cos, sin, batch, norm_attn[i], w_in_t, w_small_t, fox_bf[i],
                   cmp_pe_k[i], cmp_w1_k[i], cmp_w2_k[i], cmp_pe_v[i], cmp_w1_v[i], cmp_w2_v[i],
                   w_o, norm_mlp[i], w_up, w_down, norm_ple[i], w_ple_gate, w_ple_proj)
    return _rmsnorm(h, norm_final, x.dtype).reshape(batch, s, d)
```

```python
import functools
import math

import numpy as np
import jax
import jax.numpy as jnp
from jax import lax
from jax.experimental import pallas as pl
from jax.experimental.pallas import tpu as pltpu

HEAD_DIM = 128
H_MIX = 8
NSA_KV = 2
NSA_HPG = H_MIX // NSA_KV
CMP_LEN = 32
CMP_STRIDE = 16
SLC_LEN = 64
TOPK = 16
NSA_WINDOW = 512
DIL_CONFIGS = ((128, 1), (512, 4), (2048, 16))
ROPE_THETA = 10000.0
RMS_EPS = 1e-6
NEG = -1e30
M_INIT = -5e29
FORCED_SCORE = 1e9
LOG2E = math.log2(math.e)
Q_SCALE = HEAD_DIM ** -0.5 * LOG2E

VMEM_LIMIT_BYTES = 58 * 1024 * 1024

MXU_DIM = 256
ROW_TILE = 512
MM_TM, MM_TN = 1024, 512
DOWN_TM, DOWN_TN, DOWN_TK = 2048, 1024, 1024
FLASH_T = 512
SB_T = MXU_DIM
BAND_T = 256
CMP_TQ = 512
COMBINE_TQ = 512
CUMSUM_BLK = 128
HEADS_PER_STEP = 4

HD_QA, HD_KC, HD_VC, HD_KS, HD_VS, HD_KW, HD_VW = 0, 8, 10, 12, 14, 16, 18
HD_QB, HD_KB, HD_VB = 20, 28, 36
HD_QC, HD_KC2, HD_VC2 = 44, 52, 60
HD_QD, HD_KD, HD_VD = 68, 76, 84
N_HEADS_ALL = 92
N_BIG = N_HEADS_ALL * HEAD_DIM
GATE_COL0 = 0
FORGET_COL0 = 24
N_GATES = 3 * H_MIX
GATES_AT = (H_MIX + 6 * NSA_KV) * HEAD_DIM
FORGET_AT = GATES_AT + N_GATES + 3 * H_MIX * HEAD_DIM
_ROPED_HEADS = (tuple(range(HD_QA, HD_QA + H_MIX)) + tuple(range(HD_KS, HD_KS + NSA_KV))
                + tuple(range(HD_KW, HD_KW + NSA_KV)) + tuple(range(HD_QC, HD_QC + 2 * H_MIX)))
_QUERY_HEADS = tuple(h for q0 in (HD_QA, HD_QB, HD_QC, HD_QD) for h in range(q0, q0 + H_MIX))


def _cparams(sem):
    return pltpu.CompilerParams(dimension_semantics=sem, vmem_limit_bytes=VMEM_LIMIT_BYTES)


def _dot_nt(a, b):
    return lax.dot_general(a, b, (((1,), (1,)), ((), ())), preferred_element_type=jnp.float32)


def _dot(a, b):
    return jnp.dot(a, b, preferred_element_type=jnp.float32)


def _rmsnorm_kernel(x_ref, g_ref, o_ref):
    x = x_ref[...]
    ms = jnp.mean(x * x, axis=-1, keepdims=True)
    o_ref[...] = (x * lax.rsqrt(ms + RMS_EPS) * g_ref[...]).astype(o_ref.dtype)


def _rmsnorm(x, g, out_dtype, tm=ROW_TILE):
    m, d = x.shape
    return pl.pallas_call(
        _rmsnorm_kernel,
        out_shape=jax.ShapeDtypeStruct((m, d), out_dtype),
        grid=(m // tm,),
        in_specs=[pl.BlockSpec((tm, d), lambda i: (i, 0)),
                  pl.BlockSpec((1, d), lambda i: (0, 0))],
        out_specs=pl.BlockSpec((tm, d), lambda i: (i, 0)),
        compiler_params=_cparams(("parallel",)),
        name="rmsnorm",
    )(x, g.reshape(1, d))


def _mm_kernel(*refs, n_extra, epilogue):
    x_ref, w_ref = refs[0], refs[1]
    extras = refs[2:2 + n_extra]
    o_ref = refs[2 + n_extra]
    acc = _dot(x_ref[...].astype(jnp.bfloat16), w_ref[...].astype(jnp.bfloat16))
    o_ref[...] = epilogue(acc, *extras).astype(o_ref.dtype)


def _epi_none(acc):
    return acc


def _epi_relu2(acc):
    r = jnp.maximum(acc, 0.0)
    return r * r


def _epi_ple(acc, h_ref, p_ref, wp_ref):
    pp = _dot(p_ref[...].astype(jnp.bfloat16), wp_ref[...].astype(jnp.bfloat16))
    return h_ref[...] + jax.nn.sigmoid(acc) * pp


def _matmul(x, w, layer, out_dtype, epilogue=_epi_none, extras=(), extra_specs=(), tm=MM_TM, tn=MM_TN, name="matmul"):
    m, kdim = x.shape
    n = w.shape[2]
    tn = min(tn, n)
    return pl.pallas_call(
        functools.partial(_mm_kernel, n_extra=len(extras), epilogue=epilogue),
        out_shape=jax.ShapeDtypeStruct((m, n), out_dtype),
        grid=(m // tm, n // tn),
        in_specs=[pl.BlockSpec((tm, kdim), lambda i, j: (i, 0)),
                  pl.BlockSpec((None, kdim, tn), lambda i, j: (layer, 0, j)),
                  *extra_specs],
        out_specs=pl.BlockSpec((tm, tn), lambda i, j: (i, j)),
        compiler_params=_cparams(("parallel", "arbitrary")),
        name=name,
    )(x, w, *extras)


def _out_proj_kernel(*refs):
    *x_refs, w_ref, h_ref, o_ref = refs
    acc = h_ref[...]
    row = 0
    for x_ref in x_refs:
        width = x_ref.shape[1]
        acc = acc + _dot(x_ref[...], w_ref[row:row + width, :].astype(jnp.bfloat16))
        row += width
    o_ref[...] = acc


def _out_proj(xs, w, layer, h, tm, tn):
    m, n = h.shape
    return pl.pallas_call(
        _out_proj_kernel,
        out_shape=jax.ShapeDtypeStruct((m, n), jnp.float32),
        grid=(m // tm, n // tn),
        in_specs=[*[pl.BlockSpec((tm, x.shape[1]), lambda i, j: (i, 0)) for x in xs],
                  pl.BlockSpec((None, w.shape[1], tn), lambda i, j: (layer, 0, j)),
                  pl.BlockSpec((tm, tn), lambda i, j: (i, j))],
        out_specs=pl.BlockSpec((tm, tn), lambda i, j: (i, j)),
        compiler_params=_cparams(("parallel", "arbitrary")),
        name="out_proj",
    )(*xs, w, h)


def _mm_residual_ksplit_kernel(x_ref, w_ref, h_ref, o_ref):
    @pl.when(pl.program_id(2) == 0)
    def _():
        o_ref[...] = h_ref[...] + _dot(x_ref[...], w_ref[...].astype(jnp.bfloat16))

    @pl.when(pl.program_id(2) > 0)
    def _():
        o_ref[...] += _dot(x_ref[...], w_ref[...].astype(jnp.bfloat16))


def _matmul_residual_ksplit(x, w, layer, h, tm=DOWN_TM, tn=DOWN_TN, tk=DOWN_TK, name="matmul_ksplit"):
    m, kdim = x.shape
    n = w.shape[2]
    return pl.pallas_call(
        _mm_residual_ksplit_kernel,
        out_shape=jax.ShapeDtypeStruct((m, n), jnp.float32),
        grid=(m // tm, n // tn, kdim // tk),
        in_specs=[pl.BlockSpec((tm, tk), lambda i, j, k: (i, k)),
                  pl.BlockSpec((None, tk, tn), lambda i, j, k: (layer, k, j)),
                  pl.BlockSpec((tm, tn), lambda i, j, k: (i, j))],
        out_specs=pl.BlockSpec((tm, tn), lambda i, j, k: (i, j)),
        compiler_params=_cparams(("parallel", "parallel", "arbitrary")),
        name=name,
    )(x, w, h)


def _rope_table_kernel(pos_ref, freq_ref, sign_ref, cos_ref, sin_ref):
    ang = pos_ref[...] * freq_ref[...]
    cos_ref[...] = jnp.cos(ang)
    sin_ref[...] = jnp.sin(ang) * sign_ref[...]


def _rope_tables(positions, ts=COMBINE_TQ):
    n = positions.size
    half = HEAD_DIM // 2
    inv_freq = ROPE_THETA ** (-jnp.arange(half, dtype=jnp.float32) / half)
    freq = jnp.concatenate([inv_freq, inv_freq]).reshape(1, HEAD_DIM)
    sign = jnp.concatenate([-jnp.ones((half,), jnp.float32), jnp.ones((half,), jnp.float32)]).reshape(1, HEAD_DIM)
    pos = positions.astype(jnp.float32).reshape(n, 1)
    row = pl.BlockSpec((1, HEAD_DIM), lambda i: (0, 0))
    return pl.pallas_call(
        _rope_table_kernel,
        out_shape=[jax.ShapeDtypeStruct((n, HEAD_DIM), jnp.float32)] * 2,
        grid=(n // ts,),
        in_specs=[pl.BlockSpec((ts, 1), lambda i: (i, 0)), row, row],
        out_specs=[pl.BlockSpec((ts, HEAD_DIM), lambda i: (i, 0))] * 2,
        compiler_params=_cparams(("parallel",)),
        name="rope_tables",
    )(pos, freq, sign)


_FLAG_ROPED, _FLAG_QUERY = 1, 2


def _in_proj_kernel(flags_ref, row0_ref, x_ref, wt_ref, cos_ref, sin_ref, o_ref):
    j = pl.program_id(1)
    heads = o_ref.shape[1]
    acc = _dot_nt(x_ref[...], wt_ref[0].astype(jnp.bfloat16))
    for hh in range(heads):
        t = acc[:, hh * HEAD_DIM:(hh + 1) * HEAD_DIM]
        flags = flags_ref[j * heads + hh]
        roped = (flags & _FLAG_ROPED) > 0
        scale = jnp.where((flags & _FLAG_QUERY) > 0, Q_SCALE, 1.0)
        cos = jnp.where(roped, cos_ref[...], 1.0)
        sin = jnp.where(roped, sin_ref[...], 0.0)
        o_ref[0, hh] = ((t * cos + pltpu.roll(t, HEAD_DIM // 2, 1) * sin) * scale).astype(o_ref.dtype)


def _in_proj_heads(xn, w_in_t, layer, cos, sin, batch, tm=MM_TM, tn=MM_TN):
    m, d = xn.shape
    s = m // batch
    heads = tn // HEAD_DIM
    q_tiles = s // tm
    flags = np.zeros((N_HEADS_ALL,), np.int32)
    flags[list(_ROPED_HEADS)] |= _FLAG_ROPED
    flags[list(_QUERY_HEADS)] |= _FLAG_QUERY
    col = np.arange(0, N_BIG, tn)
    row0 = col + np.where(col >= GATES_AT, N_GATES, 0) + np.where(col >= FORGET_AT - N_GATES, H_MIX, 0)
    table = pl.BlockSpec((tm, HEAD_DIM), lambda i, j, f, r: (i, 0))
    return pl.pallas_call(
        _in_proj_kernel,
        out_shape=jax.ShapeDtypeStruct((batch, N_HEADS_ALL, s, HEAD_DIM), jnp.bfloat16),
        grid_spec=pltpu.PrefetchScalarGridSpec(
            num_scalar_prefetch=2,
            grid=(m // tm, N_BIG // tn),
            in_specs=[pl.BlockSpec((tm, d), lambda i, j, f, r: (i, 0)),
                      pl.BlockSpec((pl.Element(1), pl.Element(tn), pl.Element(d)),
                                   lambda i, j, f, r: (layer, pl.multiple_of(r[j], 8), 0)),
                      table, table],
            out_specs=pl.BlockSpec((1, heads, tm, HEAD_DIM),
                                   lambda i, j, f, r: (i // q_tiles, j, i % q_tiles, 0)),
        ),
        compiler_params=_cparams(("parallel", "arbitrary")),
        name="in_proj",
    )(jnp.asarray(flags), jnp.asarray(row0.astype(np.int32)), xn, w_in_t, cos, sin)


def _in_proj_small_kernel(x_ref, wt_ref, o_ref):
    o_ref[...] = _dot_nt(x_ref[...], wt_ref[...].astype(jnp.bfloat16))


def _in_proj_small(xn, w_small_t, layer, tm=MM_TM):
    m, d = xn.shape
    n = w_small_t.shape[1]
    return pl.pallas_call(
        _in_proj_small_kernel,
        out_shape=jax.ShapeDtypeStruct((m, n), jnp.float32),
        grid=(m // tm,),
        in_specs=[pl.BlockSpec((tm, d), lambda i: (i, 0)),
                  pl.BlockSpec((None, n, d), lambda i: (layer, 0, 0))],
        out_specs=pl.BlockSpec((tm, n), lambda i: (i, 0)),
        compiler_params=_cparams(("parallel",)),
        name="in_proj_small",
    )(xn, w_small_t)


def _softplus(z):
    return jnp.maximum(z, 0.0) + jnp.log1p(jnp.exp(-jnp.abs(z)))


def _logf_cumsum_kernel(x_ref, bias_ref, o_ref, *, blk):
    s = x_ref.shape[1]
    r = lax.broadcasted_iota(jnp.int32, (blk, blk), 0)
    c = lax.broadcasted_iota(jnp.int32, (blk, blk), 1)
    tri = jnp.where(c <= r, 1.0, 0.0).astype(jnp.float32)

    def body(i, carry):
        x = x_ref[0, pl.ds(i * blk, blk), :]
        logf = -_softplus(-(x + bias_ref[...]))
        cs = jnp.dot(tri, logf, preferred_element_type=jnp.float32,
                     precision=lax.Precision.HIGHEST) + carry
        o_ref[0, pl.ds(i * blk, blk), :] = cs
        return cs[blk - 1:blk, :]

    lax.fori_loop(0, s // blk, body, jnp.zeros((1, x_ref.shape[2]), jnp.float32))


def _logf_cumsum(small, bias_row, blk=CUMSUM_BLK):
    b, s, w = small.shape
    return pl.pallas_call(
        functools.partial(_logf_cumsum_kernel, blk=blk),
        out_shape=jax.ShapeDtypeStruct((b, s, w), jnp.float32),
        grid=(b,),
        in_specs=[pl.BlockSpec((1, s, w), lambda bi: (bi, 0, 0)),
                  pl.BlockSpec((1, w), lambda bi: (0, 0))],
        out_specs=pl.BlockSpec((1, s, w), lambda bi: (bi, 0, 0)),
        compiler_params=_cparams(("parallel",)),
        name="logf_cumsum",
    )(small, bias_row)


def _rowmax_lanes(s):
    m = s[:, :HEAD_DIM]
    for c in range(1, s.shape[1] // HEAD_DIM):
        m = jnp.maximum(m, s[:, c * HEAD_DIM:(c + 1) * HEAD_DIM])
    return jnp.broadcast_to(jnp.max(m, axis=-1, keepdims=True), m.shape)


def _fold_lanes(p):
    a = p[:, :HEAD_DIM]
    for c in range(1, p.shape[1] // HEAD_DIM):
        a = a + p[:, c * HEAD_DIM:(c + 1) * HEAD_DIM]
    return a


def _tile_lanes(x, width):
    return jnp.concatenate([x] * (width // HEAD_DIM), axis=1)


def _causal_flash_kernel(*refs, mode, hb, t):
    it = iter(refs)
    q_ref, k_ref, v_ref = next(it), next(it), next(it)
    bias_ref = next(it) if mode == "fox" else None
    sel_ref, onehot_ref = (next(it), next(it)) if mode == "sel" else (None, None)
    o_ref = next(it)
    m_scr, l_scr, acc_scr = next(it), next(it), next(it)
    i = pl.program_id(2)

    m_scr[...] = jnp.full(m_scr.shape, M_INIT, jnp.float32)
    l_scr[...] = jnp.zeros(l_scr.shape, jnp.float32)
    acc_scr[...] = jnp.zeros(acc_scr.shape, jnp.float32)

    def tile(j, diagonal):
        keys = pl.ds(pl.multiple_of(j * t, t), t)
        if mode == "sel":
            k_shared = jnp.concatenate([k_ref[0, 0, keys, :], onehot_ref[keys, :]], axis=1)
            scores = [_dot_nt(jnp.concatenate([q_ref[0, hh], sel_ref[0, 0]], axis=1), k_shared)
                      for hh in range(hb)]
        else:
            scores = [_dot_nt(q_ref[0, hh], k_ref[0, hh, keys, :]) + bias_ref[0, hh, j] for hh in range(hb)]
        if diagonal:
            causal = (lax.broadcasted_iota(jnp.int32, (t, t), 1) <= lax.broadcasted_iota(jnp.int32, (t, t), 0))
            scores = [jnp.where(causal, s, NEG) for s in scores]
        probs = []
        for hh, s in enumerate(scores):
            m_prev = m_scr[hh]
            m_new = jnp.maximum(m_prev, _rowmax_lanes(s))
            alpha = jnp.exp2(m_prev - m_new)
            p = jnp.exp2(s - _tile_lanes(m_new, t))
            l_scr[hh] = alpha * l_scr[hh] + _fold_lanes(p)
            m_scr[hh] = m_new
            probs.append((alpha, p.astype(v_ref.dtype)))
        for hh, (alpha, p) in enumerate(probs):
            v = v_ref[0, 0 if mode == "sel" else hh, keys, :]
            acc_scr[hh] = alpha * acc_scr[hh] + _dot(p, v)

    def off_diagonal(j, carry):
        tile(j, False)
        return carry

    lax.fori_loop(0, i, off_diagonal, 0)
    tile(i, True)
    for hh in range(hb):
        l = jnp.sum(l_scr[hh], axis=-1, keepdims=True)
        o_ref[0, :, hh * HEAD_DIM:(hh + 1) * HEAD_DIM] = (acc_scr[hh] / l).astype(o_ref.dtype)


def _causal_flash(hm, *, mode, q_head0, k_head0, v_head0, hb, t, out_dtype, bias=None, sel=None, onehot=None,
                  name="causal_flash"):
    b, _, s, d = hm.shape
    groups = H_MIX // hb
    kv_heads = hb if mode == "fox" else 1
    in_specs = [pl.BlockSpec((1, hb, t, d), lambda bi, g, i: (bi, q_head0 // hb + g, i, 0)),
                pl.BlockSpec((1, kv_heads, s, d), lambda bi, g, i: (bi, k_head0 // kv_heads + g, 0, 0)),
                pl.BlockSpec((1, kv_heads, s, d), lambda bi, g, i: (bi, v_head0 // kv_heads + g, 0, 0))]
    args = [hm, hm, hm]
    if mode == "fox":
        in_specs.append(pl.BlockSpec((1, hb, s // t, 1, t), lambda bi, g, i: (bi, g, 0, 0, 0)))
        args.append(bias)
    else:
        in_specs.append(pl.BlockSpec((1, 1, t, sel.shape[-1]), lambda bi, g, i: (bi, g, i, 0)))
        in_specs.append(pl.BlockSpec(onehot.shape, lambda bi, g, i: (0, 0)))
        args += [sel, onehot]
    return pl.pallas_call(
        functools.partial(_causal_flash_kernel, mode=mode, hb=hb, t=t),
        out_shape=jax.ShapeDtypeStruct((b, s, H_MIX * d), out_dtype),
        grid=(b, groups, s // t),
        in_specs=in_specs,
        out_specs=pl.BlockSpec((1, t, hb * d), lambda bi, g, i: (bi, i, g)),
        scratch_shapes=[pltpu.VMEM((hb, t, d), jnp.float32)] * 3,
        compiler_params=_cparams(("parallel", "parallel", "arbitrary")),
        name=name,
    )(*args)


def _band_kernel(*refs, hb, kv_heads, t, wpad, window, seg_len, has_lse):
    q_ref, k_ref, v_ref, o_ref = refs[:4]
    lse_ref = refs[4] if has_lse else None
    s_len = k_ref.shape[2]
    span = t + wpad
    t0 = pl.program_id(2) * t
    seg0 = (t0 // seg_len) * seg_len
    start = jnp.minimum(jnp.maximum(t0 - wpad, seg0), s_len - span)
    start = pl.multiple_of(start, HEAD_DIM)
    keys = pl.ds(start, span)
    qpos = t0 + lax.broadcasted_iota(jnp.int32, (t, span), 0)
    kpos = start + lax.broadcasted_iota(jnp.int32, (t, span), 1)
    dist = qpos - kpos
    mask = (dist >= 0) & (dist <= window) & (kpos >= seg0)
    kv_of = [hh * kv_heads // hb for hh in range(hb)]
    scores = [jnp.where(mask, _dot_nt(q_ref[0, hh], k_ref[0, kv_of[hh], keys, :]), NEG) for hh in range(hb)]
    probs = []
    for s in scores:
        m = _rowmax_lanes(s)
        p = jnp.exp2(s - _tile_lanes(m, span))
        l = jnp.broadcast_to(jnp.sum(_fold_lanes(p), axis=-1, keepdims=True), m.shape)
        probs.append((m, l, p.astype(v_ref.dtype)))
    lane = lax.broadcasted_iota(jnp.int32, (t, HEAD_DIM), 1)
    lse_all = jnp.zeros((t, HEAD_DIM), jnp.float32)
    for hh, (m, l, p) in enumerate(probs):
        out = _dot(p, v_ref[0, kv_of[hh], keys, :]) / l
        if has_lse:
            o_ref[0, hh] = out.astype(o_ref.dtype)
            lse_all = jnp.where(lane == hh, m + jnp.log2(l), lse_all)
        else:
            o_ref[0, :, hh * HEAD_DIM:(hh + 1) * HEAD_DIM] = out.astype(o_ref.dtype)
    if has_lse:
        lse_ref[0, 0] = lse_all


def _band(src, *, q_head0, k_head0, v_head0, hb, kv_heads, t, wpad, window, seg_len, has_lse, name):
    b, _, s, d = src.shape
    groups = H_MIX // hb
    in_specs = [pl.BlockSpec((1, hb, t, d), lambda bi, g, i: (bi, q_head0 // hb + g, i, 0)),
                pl.BlockSpec((1, kv_heads, s, d), lambda bi, g, i: (bi, k_head0 // kv_heads + g, 0, 0)),
                pl.BlockSpec((1, kv_heads, s, d), lambda bi, g, i: (bi, v_head0 // kv_heads + g, 0, 0))]
    if has_lse:
        out_shape = [jax.ShapeDtypeStruct((b, H_MIX, s, d), jnp.bfloat16),
                     jax.ShapeDtypeStruct((b, groups, s, d), jnp.float32)]
        out_specs = [pl.BlockSpec((1, hb, t, d), lambda bi, g, i: (bi, g, i, 0)),
                     pl.BlockSpec((1, 1, t, d), lambda bi, g, i: (bi, g, i, 0))]
    else:
        out_shape = jax.ShapeDtypeStruct((b, s, H_MIX * d), jnp.bfloat16)
        out_specs = pl.BlockSpec((1, t, hb * d), lambda bi, g, i: (bi, i, g))
    return pl.pallas_call(
        functools.partial(_band_kernel, hb=hb, kv_heads=kv_heads, t=t, wpad=wpad, window=window,
                          seg_len=seg_len, has_lse=has_lse),
        out_shape=out_shape,
        grid=(b, groups, s // t),
        in_specs=in_specs,
        out_specs=out_specs,
        compiler_params=_cparams(("parallel", "parallel", "arbitrary")),
        name=name,
    )(src, src, src)


def _stickbreak_kernel(q_ref, k_ref, v_ref, u_ref, o_ref, run_scr, acc_scr, *, hb, t):
    i = pl.program_id(2)
    run_scr[...] = jnp.zeros(run_scr.shape, jnp.float32)
    acc_scr[...] = jnp.zeros(acc_scr.shape, jnp.float32)
    u = u_ref[...]
    sign_bit = jnp.uint32(0x80000000)

    def tile(j, diagonal):
        keys = pl.ds(pl.multiple_of(j * t, t), t)
        if diagonal:
            strict = (lax.broadcasted_iota(jnp.int32, (t, t), 1) < lax.broadcasted_iota(jnp.int32, (t, t), 0))
        zs = [_dot_nt(q_ref[0, hh], k_ref[0, hh, keys, :]) for hh in range(hb)]
        stage = []
        for z2 in zs:
            neg_abs = lax.bitcast_convert_type(lax.bitcast_convert_type(z2, jnp.uint32) | sign_bit, jnp.float32)
            lg = jnp.log2(1.0 + jnp.exp2(neg_abs))
            sp2 = jnp.maximum(z2, 0.0) + lg
            if diagonal:
                sp2 = jnp.where(strict, sp2, 0.0)
            stage.append((jnp.minimum(z2, 0.0) - lg, sp2[:, 0:1], sp2.astype(jnp.bfloat16)))
        laters = [_dot(sp_bf16, u) for (_, _, sp_bf16) in stage]
        weights = []
        for hh, later in enumerate(laters):
            log_beta, first, _ = stage[hh]
            run = run_scr[hh]
            a = jnp.exp2(log_beta - (later + _tile_lanes(run, t)))
            if diagonal:
                a = jnp.where(strict, a, 0.0)
            run_scr[hh] = run + jnp.broadcast_to(later[:, 0:1] + first, run.shape)
            weights.append(a.astype(v_ref.dtype))
        for hh, a in enumerate(weights):
            acc_scr[hh] += _dot(a, v_ref[0, hh, keys, :])

    def off_diagonal(n, carry):
        tile(i - 1 - n, False)
        return carry

    tile(i, True)
    lax.fori_loop(0, i, off_diagonal, 0)
    for hh in range(hb):
        o_ref[0, :, hh * HEAD_DIM:(hh + 1) * HEAD_DIM] = acc_scr[hh].astype(o_ref.dtype)


def _stickbreak(hm, *, q_head0, k_head0, v_head0, hb, t, out_dtype):
    b, _, s, d = hm.shape
    later = np.arange(t)[:, None] > np.arange(t)[None, :]
    u = jnp.asarray(later, jnp.bfloat16)

    def heads(h0):
        return pl.BlockSpec((1, hb, s, d), lambda bi, g, i: (bi, h0 // hb + g, 0, 0))

    return pl.pallas_call(
        functools.partial(_stickbreak_kernel, hb=hb, t=t),
        out_shape=jax.ShapeDtypeStruct((b, s, H_MIX * d), out_dtype),
        grid=(b, H_MIX // hb, s // t),
        in_specs=[pl.BlockSpec((1, hb, t, d), lambda bi, g, i: (bi, q_head0 // hb + g, i, 0)),
                  heads(k_head0), heads(v_head0),
                  pl.BlockSpec((t, t), lambda bi, g, i: (0, 0))],
        out_specs=pl.BlockSpec((1, t, hb * d), lambda bi, g, i: (bi, i, g)),
        scratch_shapes=[pltpu.VMEM((hb, t, d), jnp.float32), pltpu.VMEM((hb, t, d), jnp.float32)],
        compiler_params=_cparams(("parallel", "parallel", "arbitrary")),
        name="stickbreak",
    )(hm, hm, hm, u)


def _gelu_tanh(x):
    return 0.5 * x * (1.0 + jnp.tanh(math.sqrt(2.0 / math.pi) * (x + 0.044715 * (x * x * x))))


def _compress_kernel(x_ref, pe_ref, w1_ref, w2_ref, o_ref):
    half = x_ref.shape[3]
    x = x_ref[0, 0]
    w1 = w1_ref[0].astype(jnp.bfloat16)
    first = _dot(x, w1[:half])
    second = _dot(x, w1[half:])
    pe = jnp.broadcast_to(pe_ref[0], (8, pe_ref.shape[2])).astype(jnp.bfloat16)
    pe_term = _dot(pe, w1)[0:1]
    n_chunks = x.shape[0]
    hid = first + pltpu.roll(second, n_chunks - 1, 0) + pe_term
    act = _gelu_tanh(hid)
    o_ref[0, 0] = _dot(act.astype(jnp.bfloat16), w2_ref[0].astype(jnp.bfloat16))


def _compress(chunks, pe, w1, w2):
    b, nh, nc, half = chunks.shape
    return pl.pallas_call(
        _compress_kernel,
        out_shape=jax.ShapeDtypeStruct((b, nh, nc, HEAD_DIM), jnp.float32),
        grid=(b, nh),
        in_specs=[pl.BlockSpec((1, 1, nc, half), lambda bi, h: (bi, h, 0, 0)),
                  pl.BlockSpec((1, 1, 2 * half), lambda bi, h: (h // NSA_KV, 0, 0)),
                  pl.BlockSpec((1, 2 * half, w1.shape[2]), lambda bi, h: (h // NSA_KV, 0, 0)),
                  pl.BlockSpec((1, w2.shape[1], HEAD_DIM), lambda bi, h: (h // NSA_KV, 0, 0))],
        out_specs=pl.BlockSpec((1, 1, nc, HEAD_DIM), lambda bi, h: (bi, h, 0, 0)),
        compiler_params=_cparams(("parallel", "parallel")),
        name="nsa_compress",
    )(chunks, pe, w1, w2)


def _cmp_select_kernel(q_ref, kc_ref, vc_ref, c2s_t_ref, o_ref, sel_ref, *, tq, n_cmp):
    i = pl.program_id(2)
    nc = kc_ref.shape[2]
    kc = kc_ref[0, 0].astype(jnp.bfloat16)
    vc = vc_ref[0, 0].astype(jnp.bfloat16)
    t = i * tq + lax.broadcasted_iota(jnp.int32, (1, tq), 1)
    cidx = lax.broadcasted_iota(jnp.int32, (nc, 1), 0)
    visible = (cidx * CMP_STRIDE + (CMP_LEN - 1) <= t) & (cidx < n_cmp)
    psum = jnp.zeros((nc, tq), jnp.float32)
    for hh in range(NSA_HPG):
        s = jnp.where(visible, _dot_nt(kc, q_ref[0, hh]), NEG)
        m = jnp.max(s, axis=0, keepdims=True)
        e = jnp.where(visible, jnp.exp2(s - m), 0.0)
        l = jnp.sum(e, axis=0, keepdims=True)
        p = e / jnp.where(l > 0.0, l, 1.0)
        psum = psum + p
        o_ref[0, :, hh * HEAD_DIM:(hh + 1) * HEAD_DIM] = _dot(p.T.astype(jnp.bfloat16), vc).astype(o_ref.dtype)

    imp = jnp.dot(c2s_t_ref[...], psum, preferred_element_type=jnp.float32, precision=lax.Precision.HIGHEST)
    w = imp.shape[0]
    n_slc = w // 2
    score_rows = slice(0, n_slc)
    jblk = lax.broadcasted_iota(jnp.int32, (n_slc, 1), 0)
    cur = lax.shift_right_logical(t, int(math.log2(SLC_LEN)))
    forced = (jblk == 0) | (jblk == cur) | (jblk == cur - 1)
    causal_blk = jblk * SLC_LEN <= t
    score = jnp.where(causal_blk, jnp.where(forced, FORCED_SCORE, imp[score_rows]), -1.0)
    rank = jnp.zeros((n_slc, tq), jnp.float32)
    for jp in range(n_slc):
        row = score[jp:jp + 1, :]
        before = (row > score) | ((row == score) & (jblk > jp))
        rank = rank + jnp.where(before, 1.0, 0.0)
    chosen = (rank < float(min(TOPK, n_slc))) & causal_blk
    add_mask = jnp.concatenate([jnp.where(chosen, 0.0, NEG), jnp.zeros((w - n_slc, tq), jnp.float32)], axis=0)
    sel_ref[0, 0] = add_mask.T.astype(sel_ref.dtype)


def _cmp_select(hm, cmp_kv, c2s, *, tq, n_cmp):
    b, _, s, d = hm.shape
    nc = cmp_kv.shape[2]
    w = c2s.shape[1]
    return pl.pallas_call(
        functools.partial(_cmp_select_kernel, tq=tq, n_cmp=n_cmp),
        out_shape=[jax.ShapeDtypeStruct((b, s, H_MIX * d), jnp.bfloat16),
                   jax.ShapeDtypeStruct((b, NSA_KV, s, w), jnp.bfloat16)],
        grid=(b, NSA_KV, s // tq),
        in_specs=[pl.BlockSpec((1, NSA_HPG, tq, d), lambda bi, g, i: (bi, g, i, 0)),
                  pl.BlockSpec((1, 1, nc, d), lambda bi, g, i: (bi, g, 0, 0)),
                  pl.BlockSpec((1, 1, nc, d), lambda bi, g, i: (bi, NSA_KV + g, 0, 0)),
                  pl.BlockSpec((w, nc), lambda bi, g, i: (0, 0))],
        out_specs=[pl.BlockSpec((1, tq, NSA_HPG * d), lambda bi, g, i: (bi, i, g)),
                   pl.BlockSpec((1, 1, tq, w), lambda bi, g, i: (bi, g, i, 0))],
        compiler_params=_cparams(("parallel", "parallel", "parallel")),
        name="nsa_cmp_select",
    )(hm, cmp_kv, cmp_kv, c2s.T)


def _nsa_combine_kernel(oc_ref, os_ref, ow_ref, g_ref, o_ref):
    gates = jax.nn.sigmoid(g_ref[0])
    for h in range(H_MIX):
        cols = slice(h * HEAD_DIM, (h + 1) * HEAD_DIM)
        c0 = GATE_COL0 + 3 * h
        out = (gates[:, c0:c0 + 1] * oc_ref[0, :, cols]
               + gates[:, c0 + 1:c0 + 2] * os_ref[0, :, cols]
               + gates[:, c0 + 2:c0 + 3] * ow_ref[0, :, cols])
        o_ref[0, :, cols] = out.astype(o_ref.dtype)


def _nsa_combine(o_cmp, o_slc, o_win, small, tq=COMBINE_TQ):
    b, s, w = o_cmp.shape
    big = pl.BlockSpec((1, tq, w), lambda bi, i: (bi, i, 0))
    return pl.pallas_call(
        _nsa_combine_kernel,
        out_shape=jax.ShapeDtypeStruct((b, s, w), jnp.bfloat16),
        grid=(b, s // tq),
        in_specs=[big, big, big, pl.BlockSpec((1, tq, small.shape[2]), lambda bi, i: (bi, i, 0))],
        out_specs=big,
        compiler_params=_cparams(("parallel", "parallel")),
        name="nsa_combine",
    )(o_cmp, o_slc, o_win, small)


def _dil_combine_kernel(*refs):
    n = (len(refs) - 1) // 2
    o_refs, lse_refs, out_ref = refs[:n], refs[n:2 * n], refs[-1]
    for h in range(H_MIX):
        lses = []
        for lse_ref in lse_refs:
            hb = H_MIX // lse_ref.shape[1]
            lses.append(lse_ref[0, h // hb][:, h % hb:h % hb + 1])
        m = functools.reduce(jnp.maximum, lses)
        es = [jnp.exp2(l - m) for l in lses]
        tot = functools.reduce(lambda a, b: a + b, es)
        out = functools.reduce(lambda a, b: a + b, [o_ref[0, h] * (e / tot) for o_ref, e in zip(o_refs, es)])
        out_ref[0, :, h * HEAD_DIM:(h + 1) * HEAD_DIM] = out.astype(out_ref.dtype)


def _dil_combine(outs, lses, tq=COMBINE_TQ):
    b, h, s, d = outs[0].shape
    o_spec = pl.BlockSpec((1, h, tq, d), lambda bi, i: (bi, 0, i, 0))
    lse_specs = [pl.BlockSpec((1, l.shape[1], tq, d), lambda bi, i: (bi, 0, i, 0)) for l in lses]
    return pl.pallas_call(
        _dil_combine_kernel,
        out_shape=jax.ShapeDtypeStruct((b, s, h * d), jnp.bfloat16),
        grid=(b, s // tq),
        in_specs=[o_spec] * len(outs) + lse_specs,
        out_specs=pl.BlockSpec((1, tq, h * d), lambda bi, i: (bi, i, 0)),
        compiler_params=_cparams(("parallel", "parallel")),
        name="dil_combine",
    )(*outs, *lses)


def _to_residue_major(t, dil):
    b, h, s, d = t.shape
    return t.reshape(b, h, s // dil, dil, d).transpose(0, 1, 3, 2, 4).reshape(b, h, s, d)


def _from_residue_major(t, dil):
    b, h, s, d = t.shape
    return t.reshape(b, h, dil, s // dil, d).transpose(0, 1, 3, 2, 4).reshape(b, h, s, d)


def _cmp_to_slc_matrix(n_chunks, n_cmp, n_slc):
    ratio = SLC_LEN // CMP_STRIDE
    span = CMP_LEN // CMP_STRIDE
    jj, mm, nn = np.meshgrid(np.arange(n_slc), np.arange(ratio), np.arange(span), indexing="ij")
    cc = ratio * jj + mm + nn
    keep = cc < n_cmp
    mat = np.zeros((n_chunks, 2 * n_slc), np.float32)
    np.add.at(mat, (cc[keep], jj[keep]), 1.0)
    return jnp.asarray(mat)


def _mixers(hm, small, fox_bf, cmp_pe_k, cmp_w1_k, cmp_w2_k, cmp_pe_v, cmp_w1_v, cmp_w2_v):
    b, _, s, d = hm.shape
    bf16 = jnp.bfloat16

    n_chunks = s // CMP_STRIDE
    n_cmp = (s - CMP_LEN) // CMP_STRIDE + 1
    n_slc = s // SLC_LEN
    chunks = hm[:, HD_KC:HD_KC + 2 * NSA_KV].reshape(b, 2 * NSA_KV, n_chunks, CMP_STRIDE * d)
    cmp_kv = _compress(chunks,
                       jnp.stack([cmp_pe_k, cmp_pe_v]).reshape(2, 1, CMP_LEN * d),
                       jnp.stack([cmp_w1_k, cmp_w1_v]), jnp.stack([cmp_w2_k, cmp_w2_v]))
    o_cmp, sel = _cmp_select(hm, cmp_kv, _cmp_to_slc_matrix(n_chunks, n_cmp, n_slc), tq=CMP_TQ, n_cmp=n_cmp)
    onehot = jnp.asarray((np.arange(s)[:, None] // SLC_LEN) == np.arange(2 * n_slc)[None, :], bf16)
    o_slc = _causal_flash(hm, mode="sel", q_head0=HD_QA, k_head0=HD_KS, v_head0=HD_VS, hb=NSA_HPG, t=FLASH_T,
                          sel=sel, onehot=onehot, out_dtype=bf16, name="nsa_selected")
    o_win = _band(hm, q_head0=HD_QA, k_head0=HD_KW, v_head0=HD_VW, hb=NSA_HPG, kv_heads=1, t=BAND_T, wpad=NSA_WINDOW,
                  window=NSA_WINDOW - 1, seg_len=s, has_lse=False, name="nsa_window")
    o_a = _nsa_combine(o_cmp, o_slc, o_win, small)

    t_fox = FLASH_T
    bias_row = jnp.zeros((1, small.shape[2]), jnp.float32).at[0, FORGET_COL0:FORGET_COL0 + H_MIX].set(fox_bf)
    csum = _logf_cumsum(small, bias_row)
    key_bias = (-LOG2E) * csum[:, :, FORGET_COL0:FORGET_COL0 + H_MIX].transpose(0, 2, 1)
    o_b = _causal_flash(hm, mode="fox", q_head0=HD_QB, k_head0=HD_KB, v_head0=HD_VB, hb=HEADS_PER_STEP, t=t_fox,
                        bias=key_bias.reshape(b, H_MIX, s // t_fox, 1, t_fox), out_dtype=bf16, name="fox")

    outs, lses = [], []
    for window, dil in DIL_CONFIGS:
        if dil == 1:
            src, heads0 = hm, (HD_QC, HD_KC2, HD_VC2)
        else:
            src, heads0 = _to_residue_major(hm[:, HD_QC:HD_QC + 3 * H_MIX], dil), (0, H_MIX, 2 * H_MIX)
        hb = HEADS_PER_STEP if dil == 1 else H_MIX
        o, lse = _band(src, q_head0=heads0[0], k_head0=heads0[1], v_head0=heads0[2], hb=hb, kv_heads=hb, t=BAND_T,
                       wpad=HEAD_DIM, window=window // dil, seg_len=s // dil, has_lse=True, name=f"dilated_{dil}")
        if dil != 1:
            o, lse = _from_residue_major(o, dil), _from_residue_major(lse, dil)
        outs.append(o)
        lses.append(lse)
    o_c = _dil_combine(outs, lses)

    o_d = _stickbreak(hm, q_head0=HD_QD, k_head0=HD_KD, v_head0=HD_VD, hb=HEADS_PER_STEP, t=SB_T, out_dtype=bf16)

    return o_a, o_b, o_c, o_d


def _small_w_in_t(w_in_t):
    pad = jnp.zeros((w_in_t.shape[0], HEAD_DIM - N_GATES - H_MIX, w_in_t.shape[2]), w_in_t.dtype)
    return jnp.concatenate([w_in_t[:, GATES_AT:GATES_AT + N_GATES], w_in_t[:, FORGET_AT:FORGET_AT + H_MIX], pad],
                           axis=1)


def _layer(h, layer, p, cos, sin, batch, norm_attn, w_in_t, w_small_t, fox_bf, cmp_pe_k, cmp_w1_k, cmp_w2_k,
           cmp_pe_v, cmp_w1_v, cmp_w2_v, w_o, norm_mlp, w_up, w_down, norm_ple, w_ple_gate, w_ple_proj):
    m, d = h.shape
    s = m // batch
    bf16 = jnp.bfloat16
    xn = _rmsnorm(h, norm_attn, bf16)
    hm = _in_proj_heads(xn, w_in_t, layer, cos, sin, batch)
    small = _in_proj_small(xn, w_small_t, layer)
    mix = _mixers(hm, small.reshape(batch, s, HEAD_DIM), fox_bf,
                  cmp_pe_k, cmp_w1_k, cmp_w2_k, cmp_pe_v, cmp_w1_v, cmp_w2_v)

    tm, tn = MM_TM, MM_TN
    res_spec = pl.BlockSpec((tm, tn), lambda i, j: (i, j))
    h = _out_proj([o.reshape(m, -1) for o in mix], w_o, layer, h, tm=tm, tn=tn)

    x2 = _rmsnorm(h, norm_mlp, bf16)
    mid = _matmul(x2, w_up, layer, bf16, _epi_relu2, tm=tm, tn=tn, name="mlp_up")
    h = _matmul_residual_ksplit(mid, w_down, layer, h, name="mlp_down")

    x3 = _rmsnorm(h, norm_ple, bf16)
    ple_dim = p.shape[2]
    h = _matmul(x3, w_ple_gate, layer, jnp.float32, _epi_ple, (h, p, w_ple_proj),
                (res_spec, pl.BlockSpec((None, tm, ple_dim), lambda i, j: (layer, i, 0)),
                 pl.BlockSpec((None, ple_dim, tn), lambda i, j: (layer, 0, j))),
                tm=tm, tn=tn, name="ple_gate")
    return h


def kernel(x, p, positions, norm_attn, w_in, fox_bf, cmp_pe_k, cmp_w1_k, cmp_w2_k, cmp_pe_v, cmp_w1_v, cmp_w2_v,
           w_o, norm_mlp, w_up, w_down, norm_ple, w_ple_gate, w_ple_proj, norm_final):
    batch, s, d = x.shape
    depth = p.shape[0]
    cos, sin = _rope_tables(positions)
    w_in_t = jnp.swapaxes(w_in, 1, 2)
    w_small_t = _small_w_in_t(w_in_t)
    p = p.reshape(depth, batch * s, -1)
    h = x.reshape(batch * s, d)
    for i in range(depth):
        h = _layer(h, i, p, cos, sin, batch, norm_attn[i], w_in_t, w_small_t, fox_bf[i],
                   cmp_pe_k[i], cmp_w1_k[i], cmp_w2_k[i], cmp_pe_v[i], cmp_w1_v[i], cmp_w2_v[i],
                   w_o, norm_mlp[i], w_up, w_down, norm_ple[i], w_ple_gate, w_ple_proj)
    return _rmsnorm(h, norm_final, x.dtype).reshape(batch, s, d)
```
